```python
import math
import jax
import jax.numpy as jnp
from jax import lax
import numpy as np

D_MODEL = 1024
BATCH = 8
SEQ = 4096
DEPTH = 1

HEAD_DIM = 64
ROPE_THETA = 10000.0
NORM_EPS = 1e-6
NEG_INF = -1e30
FORCE_SCORE = 1e9

DIFF_HEADS = 4
DIFF_VDIM = 2 * HEAD_DIM
DIFF_Q_BLOCK = 128

NSA_HEADS = 8
NSA_KV_GROUPS = 2
NSA_HPG = NSA_HEADS // NSA_KV_GROUPS
CMP_LEN = 32
CMP_STRIDE = 16
CMP_HIDDEN = 4 * HEAD_DIM
SEL_LEN = 64
SEL_TOPK = 16
WINDOW = 512
NSA_Q_BLOCK = 64

D_FF = int(math.ceil(8 * D_MODEL / 3 / 256)) * 256

DIFF_QK = DIFF_HEADS * 2 * HEAD_DIM
DIFF_V = DIFF_HEADS * DIFF_VDIM
NSA_Q = NSA_HEADS * HEAD_DIM
NSA_KV = NSA_KV_GROUPS * HEAD_DIM
NSA_GATES = 3 * NSA_HEADS
MERGE_GATES = 2 * D_MODEL
IN_SPLITS = (DIFF_QK, DIFF_QK, DIFF_V, NSA_Q, NSA_KV, NSA_KV, NSA_KV, NSA_KV, NSA_KV, NSA_KV, NSA_GATES, MERGE_GATES)
IN_DIM = sum(IN_SPLITS)
A_WIDTH = DIFF_V
B_WIDTH = NSA_HEADS * HEAD_DIM

kernel_name = "hybrid_diffattn_nsa_block"


def lambda_init(layer):
    return 0.8 - 0.6 * math.exp(-0.3 * layer)


def rms_norm(x, g):
    xf = x.astype(jnp.float32)
    y = xf * lax.rsqrt(jnp.mean(xf * xf, axis=-1, keepdims=True) + NORM_EPS)
    return (y * g.astype(jnp.float32)).astype(x.dtype)


def rope_tables(positions, dim):
    inv_freq = 1.0 / (ROPE_THETA ** (jnp.arange(0, dim, 2, dtype=jnp.float32) / dim))
    ang = positions.astype(jnp.float32)[..., None] * inv_freq
    return jnp.cos(ang), jnp.sin(ang)


def apply_rope(x, cos, sin):
    half = x.shape[-1] // 2
    xf = x.astype(jnp.float32)
    x1, x2 = xf[..., :half], xf[..., half:]
    return jnp.concatenate([x1 * cos - x2 * sin, x2 * cos + x1 * sin], axis=-1).astype(x.dtype)


def differential_attention(q, k, v, lam):
    B, S, H, _, dh = q.shape
    nblk = S // DIFF_Q_BLOCK
    scale = dh ** -0.5
    kpos = jnp.arange(S)

    def one_block(i):
        s0 = i * DIFF_Q_BLOCK
        qb = lax.dynamic_slice_in_dim(q, s0, DIFF_Q_BLOCK, axis=1)
        s = jnp.einsum('bqhcd,bkhcd->bhcqk', qb, k).astype(jnp.float32) * scale
        qpos = s0 + jnp.arange(DIFF_Q_BLOCK)
        causal = kpos[None, :] <= qpos[:, None]
        p = jax.nn.softmax(jnp.where(causal, s, NEG_INF), axis=-1)
        a = p[:, :, 0] - lam * p[:, :, 1]
        return jnp.einsum('bhqk,bkhe->bqhe', a.astype(v.dtype), v)

    o = lax.map(one_block, jnp.arange(nblk))
    return o.transpose(1, 0, 2, 3, 4).reshape(B, S, H, v.shape[-1])


def compress(x, pe, w1, w2):
    B, S, G, dh = x.shape
    nc = (S - CMP_LEN) // CMP_STRIDE + 1
    idx = np.arange(nc)[:, None] * CMP_STRIDE + np.arange(CMP_LEN)[None, :]
    blk = x[:, idx] + pe[None, None, :, None, :]
    blk = blk.transpose(0, 1, 3, 2, 4).reshape(B, nc, G, CMP_LEN * dh)
    return jax.nn.silu(blk @ w1) @ w2


def nsa_attention(q, kc, vc, ks, vs, kw, vw):
    B, S, G, hpg, dh = q.shape
    nc = kc.shape[1]
    nsb = S // SEL_LEN
    topk = min(SEL_TOPK, nsb)
    nblk = S // NSA_Q_BLOCK
    scale = dh ** -0.5
    cmp_end = np.arange(nc) * CMP_STRIDE + CMP_LEN - 1
    ci = np.arange(nc)[:, None] * CMP_STRIDE
    sj = np.arange(nsb)[None, :] * SEL_LEN
    overlap = jnp.asarray(((ci < sj + SEL_LEN) & (ci + CMP_LEN > sj)).astype(np.float32))
    kb = ks.reshape(B, nsb, SEL_LEN, G, dh).transpose(0, 3, 1, 2, 4)
    vb = vs.reshape(B, nsb, SEL_LEN, G, dh).transpose(0, 3, 1, 2, 4)
    kw_pad = jnp.pad(kw, ((0, 0), (WINDOW, 0), (0, 0), (0, 0)))
    vw_pad = jnp.pad(vw, ((0, 0), (WINDOW, 0), (0, 0), (0, 0)))
    b_idx = jnp.arange(B)[:, None, None, None]
    g_idx = jnp.arange(G)[None, None, :, None]
    blk_ids = jnp.arange(nsb)
    sel_off = jnp.arange(SEL_LEN)

    def one_block(i):
        s0 = i * NSA_Q_BLOCK
        qpos = s0 + jnp.arange(NSA_Q_BLOCK)
        qb = lax.dynamic_slice_in_dim(q, s0, NSA_Q_BLOCK, axis=1)
        sc = jnp.einsum('bqghd,bcgd->bqghc', qb, kc).astype(jnp.float32) * scale
        cm = (cmp_end[None, :] <= qpos[:, None])[None, :, None, None, :]
        pc = jax.nn.softmax(jnp.where(cm, sc, NEG_INF), axis=-1) * cm
        o_c = jnp.einsum('bqghc,bcgd->bqghd', pc.astype(vc.dtype), vc)
        imp = jnp.einsum('bqgc,cn->bqgn', pc.sum(axis=3), overlap)
        qblk = qpos // SEL_LEN
        forced = (blk_ids[None, :] == 0) | (blk_ids[None, :] == qblk[:, None]) | (blk_ids[None, :] == qblk[:, None] - 1)
        causal_blk = blk_ids[None, :] <= qblk[:, None]
        imp = jnp.where(forced[None, :, None, :], FORCE_SCORE, imp)
        imp = jnp.where(causal_blk[None, :, None, :], imp, NEG_INF)
        _, idx = lax.top_k(imp, topk)
        ksel = kb[b_idx, g_idx, idx]
        vsel = vb[b_idx, g_idx, idx]
        ss = jnp.einsum('bqghd,bqgnld->bqghnl', qb, ksel).astype(jnp.float32) * scale
        kp = idx[..., None] * SEL_LEN + sel_off
        smask = (kp <= qpos[None, :, None, None, None])[:, :, :, None]
        ss = jnp.where(smask, ss, NEG_INF).reshape(B, NSA_Q_BLOCK, G, hpg, topk * SEL_LEN)
        ps = jax.nn.softmax(ss, axis=-1).reshape(B, NSA_Q_BLOCK, G, hpg, topk, SEL_LEN)
        o_s = jnp.einsum('bqghnl,bqgnld->bqghd', ps.astype(vsel.dtype), vsel)
        kwb = lax.dynamic_slice_in_dim(kw_pad, s0, NSA_Q_BLOCK + WINDOW, axis=1)
        vwb = lax.dynamic_slice_in_dim(vw_pad, s0, NSA_Q_BLOCK + WINDOW, axis=1)
        kpos = s0 - WINDOW + jnp.arange(NSA_Q_BLOCK + WINDOW)
        dist = qpos[:, None] - kpos[None, :]
        wmask = ((dist >= 0) & (dist < WINDOW) & (kpos[None, :] >= 0))[None, :, None, None, :]
        sw = jnp.einsum('bqghd,bkgd->bqghk', qb, kwb).astype(jnp.float32) * scale
        pw = jax.nn.softmax(jnp.where(wmask, sw, NEG_INF), axis=-1)
        o_w = jnp.einsum('bqghk,bkgd->bqghd', pw.astype(vwb.dtype), vwb)
        return o_c, o_s, o_w

    o_c, o_s, o_w = lax.map(one_block, jnp.arange(nblk))

    def to_seq(o):
        return o.transpose(1, 0, 2, 3, 4, 5).reshape(B, S, G * hpg, dh)

    return to_seq(o_c), to_seq(o_s), to_seq(o_w)


def hybrid_mixer(u, cos, sin, layer, w_in, diff_lambda, diff_subln_g, cmp_pe_k, cmp_pe_v,
                 cmp_k_w1, cmp_k_w2, cmp_v_w1, cmp_v_w2, w_branch_a, w_branch_b, w_out):
    B, S, _ = u.shape
    G, dh = NSA_KV_GROUPS, HEAD_DIM
    split_at = np.cumsum(IN_SPLITS)[:-1].tolist()
    (dq, dk, dv, nq, kc_, vc_, ks_, vs_, kw_, vw_, ngate, mgate) = jnp.split(u @ w_in, split_at, axis=-1)
    c4, s4 = cos[:, :, None, None, :], sin[:, :, None, None, :]
    c3, s3 = cos[:, :, None, :], sin[:, :, None, :]
    dq = apply_rope(dq.reshape(B, S, DIFF_HEADS, 2, dh), c4, s4)
    dk = apply_rope(dk.reshape(B, S, DIFF_HEADS, 2, dh), c4, s4)
    dv = dv.reshape(B, S, DIFF_HEADS, DIFF_VDIM)
    lam_init = lambda_init(layer)
    lf = diff_lambda.astype(jnp.float32)
    lam = jnp.exp(jnp.sum(lf[0] * lf[1])) - jnp.exp(jnp.sum(lf[2] * lf[3])) + lam_init
    y_a = differential_attention(dq, dk, dv, lam)
    y_a = rms_norm(y_a, diff_subln_g) * (1.0 - lam_init)
    y_a = y_a.reshape(B, S, A_WIDTH) @ w_branch_a
    nq = apply_rope(nq.reshape(B, S, NSA_HEADS, dh), c3, s3).reshape(B, S, G, NSA_HPG, dh)

    def kv(t):
        return t.reshape(B, S, G, dh)

    k_cmp = compress(apply_rope(kv(kc_), c3, s3), cmp_pe_k, cmp_k_w1, cmp_k_w2)
    v_cmp = compress(kv(vc_), cmp_pe_v, cmp_v_w1, cmp_v_w2)
    k_sel = apply_rope(kv(ks_), c3, s3)
    k_win = apply_rope(kv(kw_), c3, s3)
    o_c, o_s, o_w = nsa_attention(nq, k_cmp, v_cmp, k_sel, kv(vs_), k_win, kv(vw_))
    g = jax.nn.sigmoid(ngate.astype(jnp.float32)).reshape(B, S, 3, NSA_HEADS, 1).astype(u.dtype)
    y_b = g[:, :, 0] * o_c + g[:, :, 1] * o_s + g[:, :, 2] * o_w
    y_b = y_b.reshape(B, S, B_WIDTH) @ w_branch_b
    mg = jax.nn.sigmoid(mgate.astype(jnp.float32)).reshape(B, S, 2, D_MODEL).astype(u.dtype)
    return (mg[:, :, 0] * y_a + mg[:, :, 1] * y_b) @ w_out


def swiglu(u, w_gate, w_up, w_down):
    return (jax.nn.silu(u @ w_gate) * (u @ w_up)) @ w_down


def setup_inputs(seed: int = 0) -> dict:
    key = jax.random.key(seed)
    ks = jax.random.split(key, 22)

    def nrm(k, shape, scale):
        return jax.random.normal(k, shape, jnp.float32) * scale

    return {
        "x": nrm(ks[0], (BATCH, SEQ, D_MODEL), 1.0),
        "positions": jnp.tile(jnp.arange(SEQ, dtype=jnp.int32)[None, :], (BATCH, 1)),
        "attn_norm_g": 1.0 + nrm(ks[1], (DEPTH, D_MODEL), 0.02),
        "w_in": nrm(ks[2], (DEPTH, D_MODEL, IN_DIM), D_MODEL ** -0.5),
        "diff_lambda": nrm(ks[3], (DEPTH, 4, HEAD_DIM), 0.1),
        "diff_subln_g": 1.0 + nrm(ks[4], (DEPTH, DIFF_VDIM), 0.02),
        "cmp_pe_k": nrm(ks[5], (DEPTH, CMP_LEN, HEAD_DIM), 0.1),
        "cmp_pe_v": nrm(ks[6], (DEPTH, CMP_LEN, HEAD_DIM), 0.1),
        "cmp_k_w1": nrm(ks[7], (DEPTH, CMP_LEN * HEAD_DIM, CMP_HIDDEN), (CMP_LEN * HEAD_DIM) ** -0.5),
        "cmp_k_w2": nrm(ks[8], (DEPTH, CMP_HIDDEN, HEAD_DIM), CMP_HIDDEN ** -0.5),
        "cmp_v_w1": nrm(ks[9], (DEPTH, CMP_LEN * HEAD_DIM, CMP_HIDDEN), (CMP_LEN * HEAD_DIM) ** -0.5),
        "cmp_v_w2": nrm(ks[10], (DEPTH, CMP_HIDDEN, HEAD_DIM), CMP_HIDDEN ** -0.5),
        "w_branch_a": nrm(ks[11], (DEPTH, A_WIDTH, D_MODEL), A_WIDTH ** -0.5),
        "w_branch_b": nrm(ks[12], (DEPTH, B_WIDTH, D_MODEL), B_WIDTH ** -0.5),
        "w_out": nrm(ks[13], (DEPTH, D_MODEL, D_MODEL), D_MODEL ** -0.5),
        "ffn_norm_g": 1.0 + nrm(ks[14], (DEPTH, D_MODEL), 0.02),
        "w_gate": nrm(ks[15], (DEPTH, D_MODEL, D_FF), D_MODEL ** -0.5),
        "w_up": nrm(ks[16], (DEPTH, D_MODEL, D_FF), D_MODEL ** -0.5),
        "w_down": nrm(ks[17], (DEPTH, D_FF, D_MODEL), D_FF ** -0.5),
        "final_norm_g": 1.0 + nrm(ks[18], (D_MODEL,), 0.02),
    }


def reference(x, positions, attn_norm_g, w_in, diff_lambda, diff_subln_g, cmp_pe_k, cmp_pe_v,
              cmp_k_w1, cmp_k_w2, cmp_v_w1, cmp_v_w2, w_branch_a, w_branch_b, w_out,
              ffn_norm_g, w_gate, w_up, w_down, final_norm_g):
    cos, sin = rope_tables(positions, HEAD_DIM)
    h = x
    for layer in range(DEPTH):
        u = rms_norm(h, attn_norm_g[layer])
        h = h + hybrid_mixer(u, cos, sin, layer, w_in[layer], diff_lambda[layer], diff_subln_g[layer],
                             cmp_pe_k[layer], cmp_pe_v[layer], cmp_k_w1[layer], cmp_k_w2[layer],
                             cmp_v_w1[layer], cmp_v_w2[layer], w_branch_a[layer], w_branch_b[layer],
                             w_out[layer])
        u = rms_norm(h, ffn_norm_g[layer])
        h = h + swiglu(u, w_gate[layer], w_up[layer], w_down[layer])
    return rms_norm(h, final_norm_g)
```

```python
import functools
import math

import jax
import jax.numpy as jnp
import numpy as np
from jax import lax
from jax.experimental import pallas as pl
from jax.experimental.pallas import tpu as pltpu

HEAD_DIM = 64
ROPE_THETA = 10000.0
NORM_EPS = 1e-6
NEG_INF = -1e30
FORCE_SCORE = 1e9

DIFF_HEADS = 4
DIFF_VDIM = 2 * HEAD_DIM
NSA_HEADS = 8
NSA_KV_GROUPS = 2
NSA_HPG = NSA_HEADS // NSA_KV_GROUPS
CMP_LEN = 32
CMP_STRIDE = 16
SEL_LEN = 64
SEL_TOPK = 16
WINDOW = 512

LANES = 128
VMEM_LIMIT = 56 * 1024 * 1024

BF16 = jnp.bfloat16
F32 = jnp.float32


def _lambda_init(layer):
    return 0.8 - 0.6 * math.exp(-0.3 * layer)


def _params(n_axes):
    return pltpu.CompilerParams(dimension_semantics=("arbitrary",) * n_axes, vmem_limit_bytes=VMEM_LIMIT)


def _const_spec(shape):
    nd = len(shape)
    return pl.BlockSpec(shape, lambda *_: (0,) * nd)


def _rope_table_kernel(pos_ref, invf_ref, cos_ref, sin_ref):
    ang = pos_ref[...].astype(F32) * invf_ref[...]
    cos_ref[...] = jnp.cos(ang)
    sin_ref[...] = jnp.sin(ang)


def _rope_tables(positions):
    half = HEAD_DIM // 2
    per_row = LANES // half
    t = positions.size
    rows = t // per_row
    pos_e = jnp.repeat(positions.reshape(rows, per_row), half, axis=1)
    inv_freq = 1.0 / (ROPE_THETA ** (jnp.arange(0, HEAD_DIM, 2, dtype=F32) / HEAD_DIM))
    invf = jnp.tile(inv_freq, per_row)[None, :]
    tr = min(rows, 1024)
    cos, sin = pl.pallas_call(
        _rope_table_kernel,
        grid=(rows // tr,),
        in_specs=[pl.BlockSpec((tr, LANES), lambda i: (i, 0)), _const_spec((1, LANES))],
        out_specs=[pl.BlockSpec((tr, LANES), lambda i: (i, 0))] * 2,
        out_shape=[jax.ShapeDtypeStruct((rows, LANES), F32)] * 2,
        compiler_params=_params(1),
        name="rope_tables",
    )(pos_e, invf)
    return cos.reshape(t, half), sin.reshape(t, half)


_ROPED_SEGS = (("dq", 512, BF16), ("dk", 512, BF16), ("nq", 512, BF16), ("ksd", 256, BF16), ("kwd", 256, BF16), ("kc", 128, F32))
_PLAIN_SEGS = (("dv", 512, BF16), ("vsd", 256, BF16), ("vwd", 256, BF16), ("vc", 128, F32), ("ng", 128, F32))


def _proj_kernel(x_ref, g_ref, cos_ref, sin_ref, wr_ref, wn_ref, *out_refs):
    x = x_ref[...]
    ms = jnp.mean(x * x, axis=-1, keepdims=True)
    u = (x * lax.rsqrt(ms + NORM_EPS) * g_ref[...]).astype(BF16)

    per_row = LANES // (HEAD_DIM // 2)
    c = jnp.concatenate([cos_ref[...]] * per_row, axis=1)
    s = jnp.concatenate([sin_ref[...]] * per_row, axis=1)
    lane = lax.broadcasted_iota(jnp.int32, c.shape, 1)
    upper = (lane % HEAD_DIM) >= (HEAD_DIM // 2)
    s_up = jnp.where(upper, s, 0.0)
    s_lo = jnp.where(upper, 0.0, -s)

    def rope(y):
        return y * c + pltpu.roll(y, HEAD_DIM // 2, 1) * s_up + pltpu.roll(y, LANES - HEAD_DIM // 2, 1) * s_lo

    refs = list(out_refs)
    for segs, w_ref, roped in ((_ROPED_SEGS, wr_ref, True), (_PLAIN_SEGS, wn_ref, False)):
        off = 0
        for name, width, _ in segs:
            o_ref = refs.pop(0)
            for c0 in range(0, width, 2 * LANES):
                cw = min(2 * LANES, width - c0)
                y = jnp.dot(u, w_ref[:, off + c0:off + c0 + cw], preferred_element_type=F32)
                for l0 in range(0, cw, LANES):
                    yl = y[:, l0:l0 + LANES]
                    if roped:
                        yl = rope(yl)
                    if name == "ng":
                        yl = jax.nn.sigmoid(yl)
                    o_ref[:, c0 + l0:c0 + l0 + LANES] = yl.astype(o_ref.dtype)
            off += width


def _dup_groups(w):
    d = w.shape[0]
    wg = w.reshape(d, NSA_KV_GROUPS, 1, HEAD_DIM)
    return jnp.broadcast_to(wg, (d, NSA_KV_GROUPS, 2, HEAD_DIM)).reshape(d, NSA_KV_GROUPS * 2 * HEAD_DIM)


def _project(x2, g, cos, sin, w_in):
    t, d = x2.shape
    qk = DIFF_HEADS * 2 * HEAD_DIM
    kv = NSA_KV_GROUPS * HEAD_DIM
    splits = (qk, qk, DIFF_HEADS * DIFF_VDIM, NSA_HEADS * HEAD_DIM, kv, kv, kv, kv, kv, kv, 3 * NSA_HEADS)
    offs = np.cumsum((0,) + splits)
    dq, dk, dv, nq, kc, vc, ks, vs, kw, vw, ng = (w_in[:, offs[i]:offs[i + 1]] for i in range(len(splits)))
    scale = HEAD_DIM ** -0.5
    ng = jnp.pad(ng, ((0, 0), (0, LANES - ng.shape[1])))
    wr = jnp.concatenate([dq * scale, dk, nq * scale, _dup_groups(ks), _dup_groups(kw), kc], axis=1).astype(BF16)
    wn = jnp.concatenate([dv, _dup_groups(vs), _dup_groups(vw), vc, ng], axis=1).astype(BF16)
    tm = min(t, 512)
    half = HEAD_DIM // 2
    segs = _ROPED_SEGS + _PLAIN_SEGS
    outs = pl.pallas_call(
        _proj_kernel,
        grid=(t // tm,),
        in_specs=[
            pl.BlockSpec((tm, d), lambda i: (i, 0)),
            _const_spec((1, d)),
            pl.BlockSpec((tm, half), lambda i: (i, 0)),
            pl.BlockSpec((tm, half), lambda i: (i, 0)),
            _const_spec(wr.shape),
            _const_spec(wn.shape),
        ],
        out_specs=[pl.BlockSpec((tm, w), lambda i: (i, 0)) for _, w, _ in segs],
        out_shape=[jax.ShapeDtypeStruct((t, w), dt) for _, w, dt in segs],
        compiler_params=_params(1),
        name="in_proj",
    )(x2, g[None, :], cos, sin, wr, wn)
    return dict(zip((n for n, _, _ in segs), outs))


def _compress_kernel(x_ref, pea_ref, peb_ref, w1a_ref, w1b_ref, w2_ref, o_ref):
    x = x_ref[...]
    a = jnp.dot((x + pea_ref[...]).astype(BF16), w1a_ref[...], preferred_element_type=F32)
    b = jnp.dot((x + peb_ref[...]).astype(BF16), w1b_ref[...], preferred_element_type=F32)
    ncp = x.shape[0]
    h = a + pltpu.roll(b, ncp - 1, 0)
    h = h * jax.nn.sigmoid(h)
    o_ref[...] = jnp.dot(h.astype(BF16), w2_ref[...], preferred_element_type=F32).astype(o_ref.dtype)


def _compress(xc, pe, w1, w2, b, s):
    g_, dh = NSA_KV_GROUPS, HEAD_DIM
    ncp = s // CMP_STRIDE
    hid = w1.shape[1]
    halves = CMP_LEN // CMP_STRIDE
    assert halves == 2
    xr = xc.reshape(b, ncp, CMP_STRIDE * g_ * dh)
    eye = jnp.eye(g_, dtype=F32)
    pe_r = pe.reshape(halves, CMP_STRIDE, 1, dh)
    pe_t = jnp.broadcast_to(pe_r, (halves, CMP_STRIDE, g_, dh)).reshape(halves, 1, CMP_STRIDE * g_ * dh)
    w1r = w1.reshape(halves, CMP_STRIDE, dh, hid)
    w1bd = jnp.einsum("hldj,pg->hlpdgj", w1r, eye).reshape(halves, CMP_STRIDE * g_ * dh, g_ * hid).astype(BF16)
    w2bd = jnp.einsum("jd,pg,r->pjgrd", w2, eye, jnp.ones((2,), F32)).reshape(g_ * hid, g_ * 2 * dh).astype(BF16)
    kdim = CMP_STRIDE * g_ * dh
    return pl.pallas_call(
        _compress_kernel,
        grid=(b,),
        in_specs=[
            pl.BlockSpec((None, ncp, kdim), lambda i: (i, 0, 0)),
            _const_spec((1, kdim)),
            _const_spec((1, kdim)),
            _const_spec((kdim, g_ * hid)),
            _const_spec((kdim, g_ * hid)),
            _const_spec((g_ * hid, g_ * 2 * dh)),
        ],
        out_specs=pl.BlockSpec((None, ncp, g_ * 2 * dh), lambda i: (i, 0, 0)),
        out_shape=jax.ShapeDtypeStruct((b, ncp, g_ * 2 * dh), BF16),
        compiler_params=_params(1),
        name="compress",
    )(xr, pe_t[0], pe_t[1], w1bd[0], w1bd[1], w2bd)


def _stack_halves(q):
    lane = lax.broadcasted_iota(jnp.int32, q.shape, 1)
    zero = jnp.zeros_like(q)
    return jnp.concatenate([jnp.where(lane < HEAD_DIM, q, zero), jnp.where(lane >= HEAD_DIM, q, zero)], axis=0)


def _softmax_step(carry, qs, k, v, mask):
    m, l, acc = carry
    s = lax.dot_general(k, qs, (((1,), (1,)), ((), ())), preferred_element_type=F32)
    if mask is not None:
        s = jnp.where(mask, s, NEG_INF)
    m_new = jnp.maximum(m, jnp.max(s, axis=0, keepdims=True))
    alpha = jnp.exp(m - m_new)
    p = jnp.exp(s - m_new)
    l = alpha * l + jnp.sum(p, axis=0, keepdims=True)
    pv = lax.dot_general(v, p.astype(BF16), (((0,), (0,)), ((), ())), preferred_element_type=F32)
    return m_new, l, alpha * acc + pv


def _softmax_init(dv, m_cols):
    return (jnp.full((1, m_cols), NEG_INF, F32), jnp.zeros((1, m_cols), F32), jnp.zeros((dv, m_cols), F32))


def _diff_kernel(lam_ref, g_ref, q_ref, k_ref, v_ref, o_ref, *, tq, tk, lam_init):
    qi = pl.program_id(2)
    qs = _stack_halves(q_ref[...])
    m_cols = 2 * tq
    qrel = lax.broadcasted_iota(jnp.int32, (1, m_cols), 1) % tq
    krel = lax.broadcasted_iota(jnp.int32, (tk, m_cols), 0)
    nsub = tq // tk

    def kv(kb):
        start = pl.multiple_of(kb * tk, tk)
        return k_ref[pl.ds(start, tk), :], v_ref[pl.ds(start, tk), :]

    carry = lax.fori_loop(0, qi * nsub, lambda kb, c: _softmax_step(c, qs, *kv(kb), None), _softmax_init(DIFF_VDIM, m_cols))
    for j in range(nsub):
        carry = _softmax_step(carry, qs, *kv(qi * nsub + j), krel + j * tk <= qrel)
    _, l, acc = carry
    o = acc / l
    lf = lam_ref[...]
    lam = (jnp.exp(jnp.sum(lf[0:1] * lf[1:2], axis=1, keepdims=True))
           - jnp.exp(jnp.sum(lf[2:3] * lf[3:4], axis=1, keepdims=True)) + lam_init)
    y = o[:, :tq] - lam * o[:, tq:]
    y = y * lax.rsqrt(jnp.mean(y * y, axis=0, keepdims=True) + NORM_EPS)
    y = y.T * g_ref[...] * (1.0 - lam_init)
    o_ref[...] = y.astype(o_ref.dtype)


def _diff_attention(dq, dk, dv, diff_lambda, subln_g, b, s, layer):
    tq, tk = 256, 256
    tq = min(tq, s)
    tk = min(tk, tq)
    hw = 2 * HEAD_DIM
    width = DIFF_HEADS * hw
    return pl.pallas_call(
        functools.partial(_diff_kernel, tq=tq, tk=tk, lam_init=_lambda_init(layer)),
        grid=(b, DIFF_HEADS, s // tq),
        in_specs=[
            _const_spec(diff_lambda.shape),
            _const_spec((1, DIFF_VDIM)),
            pl.BlockSpec((None, tq, hw), lambda bi, h, i: (bi, i, h)),
            pl.BlockSpec((None, s, hw), lambda bi, h, i: (bi, 0, h)),
            pl.BlockSpec((None, s, DIFF_VDIM), lambda bi, h, i: (bi, 0, h)),
        ],
        out_specs=pl.BlockSpec((None, tq, DIFF_VDIM), lambda bi, h, i: (bi, i, h)),
        out_shape=jax.ShapeDtypeStruct((b, s, DIFF_HEADS * DIFF_VDIM), BF16),
        compiler_params=_params(3),
        name="diff_attn",
    )(diff_lambda, subln_g[None, :], dq.reshape(b, s, width), dk.reshape(b, s, width), dv.reshape(b, s, width))


def _nsa_kernel(ovl_ref, q_ref, gate_ref, kc_ref, vc_ref, ks_ref, vs_ref, kw_ref, vw_ref, o_ref, imp_ref, sel_ref, *, tq, tk, s_len):
    qi = pl.program_id(2)
    grp = pl.program_id(1)
    q0 = qi * tq
    m_cols = NSA_HPG * tq
    q = q_ref[...]
    qs = jnp.concatenate([_stack_halves(q[:, :LANES]), _stack_halves(q[:, LANES:])], axis=0)
    qpos_t = q0 + lax.broadcasted_iota(jnp.int32, (1, tq), 1)
    qpos = jnp.concatenate([qpos_t] * NSA_HPG, axis=1)
    dh = HEAD_DIM

    ncp = kc_ref.shape[0]
    cmp_end = lax.broadcasted_iota(jnp.int32, (ncp, m_cols), 0) * CMP_STRIDE + (CMP_LEN - 1)
    cmask = cmp_end <= qpos
    sc = lax.dot_general(kc_ref[...], qs, (((1,), (1,)), ((), ())), preferred_element_type=F32)
    sc = jnp.where(cmask, sc, NEG_INF)
    pc = jnp.exp(sc - jnp.max(sc, axis=0, keepdims=True))
    pc = pc / jnp.sum(pc, axis=0, keepdims=True)
    pc = jnp.where(cmask, pc, 0.0)
    o_c = lax.dot_general(vc_ref[...], pc.astype(BF16), (((0,), (0,)), ((), ())), preferred_element_type=F32)[:dh]

    pc_sum = pc[:, 0:tq]
    for h in range(1, NSA_HPG):
        pc_sum = pc_sum + pc[:, h * tq:(h + 1) * tq]
    imp = jnp.dot(ovl_ref[...], pc_sum, preferred_element_type=F32, precision=lax.Precision.HIGHEST)
    nsb = imp.shape[0]
    blk = lax.broadcasted_iota(jnp.int32, (nsb, tq), 0)
    qblk = qpos_t // SEL_LEN
    forced = (blk == 0) | (blk == qblk) | (blk == qblk - 1)
    imp = jnp.where(forced, FORCE_SCORE, imp)
    imp = jnp.where(blk <= qblk, imp, NEG_INF)
    imp_ref[...] = imp
    topk = min(SEL_TOPK, nsb)

    def rank_step(j, cnt):
        row = imp_ref[pl.ds(j, 1), :]
        beats = (row > imp) | ((row == imp) & (j < blk))
        return cnt + beats.astype(jnp.int32)

    n_causal_blk = (q0 + tq - 1) // SEL_LEN + 1
    cnt = lax.fori_loop(0, n_causal_blk, rank_step, jnp.zeros((nsb, tq), jnp.int32))
    sel_ref[...] = jnp.where(cnt < topk, 1.0, 0.0)

    krel = lax.broadcasted_iota(jnp.int32, (tk, m_cols), 0)
    bpt = tk // SEL_LEN

    def sel_step(kb, carry):
        k0 = pl.multiple_of(kb * tk, tk)
        rows = []
        for r in range(bpt):
            srow = sel_ref[pl.ds(kb * bpt + r, 1), :]
            srow = jnp.concatenate([srow] * NSA_HPG, axis=1)
            rows.append(jnp.broadcast_to(srow, (SEL_LEN, m_cols)))
        chosen = jnp.concatenate(rows, axis=0) > 0.5
        mask = chosen & (krel <= qpos - k0)
        return _softmax_step(carry, qs, ks_ref[pl.ds(k0, tk), :], vs_ref[pl.ds(k0, tk), :], mask)

    n_sel_tiles = (q0 + tq - 1) // tk + 1
    _, l_s, acc_s = lax.fori_loop(0, n_sel_tiles, sel_step, _softmax_init(2 * dh, m_cols))
    o_s = (acc_s / l_s)[:dh]

    def win_step(kb, carry):
        k0 = pl.multiple_of(kb * tk, tk)
        dist = (qpos - k0) - krel
        mask = (dist >= 0) & (dist < WINDOW)
        return _softmax_step(carry, qs, kw_ref[pl.ds(k0, tk), :], vw_ref[pl.ds(k0, tk), :], mask)

    first = jnp.maximum(q0 - (WINDOW - 1), 0) // tk
    _, l_w, acc_w = lax.fori_loop(first, n_sel_tiles, win_step, _softmax_init(2 * dh, m_cols))
    o_w = (acc_w / l_w)[:dh]

    gates = gate_ref[...].T
    outs = []
    for h in range(NSA_HPG):
        hg = grp * NSA_HPG + h
        sl = slice(h * tq, (h + 1) * tq)
        y = (_gate_row(gates, 0, hg) * o_c[:, sl] + _gate_row(gates, 1, hg) * o_s[:, sl] + _gate_row(gates, 2, hg) * o_w[:, sl])
        outs.append(y)
    y = jnp.concatenate(outs, axis=0)
    o_ref[...] = y.T.astype(o_ref.dtype)


def _gate_row(gates, branch, head):
    idx = branch * NSA_HEADS + head
    row = lax.broadcasted_iota(jnp.int32, gates.shape, 0)
    return jnp.sum(jnp.where(row == idx, gates, 0.0), axis=0, keepdims=True)


def _overlap_matrix(s):
    nc = s // CMP_STRIDE
    nsb = s // SEL_LEN
    ci = np.arange(nc)[None, :] * CMP_STRIDE
    sj = np.arange(nsb)[:, None] * SEL_LEN
    ovl = ((ci < sj + SEL_LEN) & (ci + CMP_LEN > sj)).astype(np.float32)
    ovl[:, (s - CMP_LEN) // CMP_STRIDE + 1:] = 0.0
    return jnp.asarray(ovl)


def _nsa_attention(nq, ng, kcmp, vcmp, ksd, vsd, kwd, vwd, b, s):
    tq = min(128, s)
    tk = min(256, s)
    gw = NSA_HPG * HEAD_DIM
    kw_ = 2 * HEAD_DIM
    ncp = s // CMP_STRIDE
    nsb = s // SEL_LEN
    seq3 = lambda a: a.reshape(b, s, a.shape[-1])
    kv_spec = pl.BlockSpec((None, s, kw_), lambda bi, g, i: (bi, 0, g))
    cmp_spec = pl.BlockSpec((None, ncp, kw_), lambda bi, g, i: (bi, 0, g))
    return pl.pallas_call(
        functools.partial(_nsa_kernel, tq=tq, tk=tk, s_len=s),
        grid=(b, NSA_KV_GROUPS, s // tq),
        in_specs=[
            _const_spec((nsb, ncp)),
            pl.BlockSpec((None, tq, gw), lambda bi, g, i: (bi, i, g)),
            pl.BlockSpec((None, tq, LANES), lambda bi, g, i: (bi, i, 0)),
            cmp_spec, cmp_spec, kv_spec, kv_spec, kv_spec, kv_spec,
        ],
        out_specs=pl.BlockSpec((None, tq, gw), lambda bi, g, i: (bi, i, g)),
        out_shape=jax.ShapeDtypeStruct((b, s, NSA_HEADS * HEAD_DIM), BF16),
        scratch_shapes=[pltpu.VMEM((nsb, tq), F32), pltpu.VMEM((nsb, tq), F32)],
        compiler_params=_params(3),
        name="nsa_attn",
    )(_overlap_matrix(s), seq3(nq), seq3(ng), kcmp, vcmp, seq3(ksd), seq3(vsd), seq3(kwd), seq3(vwd))


def _merge_kernel(x_ref, g_ref, ya_ref, yb_ref, wmg_ref, wa_ref, wb_ref, wo_ref, o_ref):
    x = x_ref[...]
    d = x.shape[1]
    ms = jnp.mean(x * x, axis=-1, keepdims=True)
    u = (x * lax.rsqrt(ms + NORM_EPS) * g_ref[...]).astype(BF16)
    ya = jnp.dot(ya_ref[...], wa_ref[...], preferred_element_type=F32)
    yb = jnp.dot(yb_ref[...], wb_ref[...], preferred_element_type=F32)
    mg_a = jax.nn.sigmoid(jnp.dot(u, wmg_ref[:, :d], preferred_element_type=F32))
    mg_b = jax.nn.sigmoid(jnp.dot(u, wmg_ref[:, d:], preferred_element_type=F32))
    merged = (mg_a * ya + mg_b * yb).astype(BF16)
    o_ref[...] = x + jnp.dot(merged, wo_ref[...], preferred_element_type=F32)


def _merge(x2, g, ya, yb, w_mg, w_a, w_b, w_o):
    t, d = x2.shape
    tm = min(t, 512)
    ws = [w.astype(BF16) for w in (w_mg, w_a, w_b, w_o)]
    return pl.pallas_call(
        _merge_kernel,
        grid=(t // tm,),
        in_specs=[
            pl.BlockSpec((tm, d), lambda i: (i, 0)),
            _const_spec((1, d)),
            pl.BlockSpec((tm, ya.shape[1]), lambda i: (i, 0)),
            pl.BlockSpec((tm, yb.shape[1]), lambda i: (i, 0)),
        ] + [_const_spec(w.shape) for w in ws],
        out_specs=pl.BlockSpec((tm, d), lambda i: (i, 0)),
        out_shape=jax.ShapeDtypeStruct((t, d), F32),
        compiler_params=_params(1),
        name="merge_out_proj",
    )(x2, g[None, :], ya, yb, *ws)


def _ffn_kernel(h_ref, g_ref, gf_ref, wg_ref, wu_ref, wd_ref, o_ref, *, chunk, final_norm):
    h = h_ref[...]
    ms = jnp.mean(h * h, axis=-1, keepdims=True)
    u = (h * lax.rsqrt(ms + NORM_EPS) * g_ref[...]).astype(BF16)
    d_ff = wg_ref.shape[1]
    acc = h
    for c0 in range(0, d_ff, chunk):
        a = jnp.dot(u, wg_ref[:, c0:c0 + chunk], preferred_element_type=F32)
        up = jnp.dot(u, wu_ref[:, c0:c0 + chunk], preferred_element_type=F32)
        act = (a * jax.nn.sigmoid(a) * up).astype(BF16)
        acc = acc + jnp.dot(act, wd_ref[c0:c0 + chunk, :], preferred_element_type=F32)
    if final_norm:
        ms = jnp.mean(acc * acc, axis=-1, keepdims=True)
        acc = acc * lax.rsqrt(ms + NORM_EPS) * gf_ref[...]
    o_ref[...] = acc


def _ffn(h2, g, g_final, w_gate, w_up, w_down, final_norm):
    t, d = h2.shape
    tm = min(t, 512)
    d_ff = w_gate.shape[1]
    chunk = 256
    assert d_ff % chunk == 0
    ws = [w.astype(BF16) for w in (w_gate, w_up, w_down)]
    return pl.pallas_call(
        functools.partial(_ffn_kernel, chunk=chunk, final_norm=final_norm),
        grid=(t // tm,),
        in_specs=[pl.BlockSpec((tm, d), lambda i: (i, 0)), _const_spec((1, d)), _const_spec((1, d))]
        + [_const_spec(w.shape) for w in ws],
        out_specs=pl.BlockSpec((tm, d), lambda i: (i, 0)),
        out_shape=jax.ShapeDtypeStruct((t, d), F32),
        compiler_params=_params(1),
        name="swiglu_ffn",
    )(h2, g[None, :], g_final[None, :], *ws)


def kernel(x, positions, attn_norm_g, w_in, diff_lambda, diff_subln_g, cmp_pe_k, cmp_pe_v, cmp_k_w1, cmp_k_w2, cmp_v_w1, cmp_v_w2, w_branch_a, w_branch_b, w_out, ffn_norm_g, w_gate, w_up, w_down, final_norm_g):
    b, s, d = x.shape
    depth = w_in.shape[0]
    cos, sin = _rope_tables(positions)
    h = x.reshape(b * s, d)
    gates_off = w_in.shape[2] - 2 * d
    for layer in range(depth):
        p = _project(h, attn_norm_g[layer], cos, sin, w_in[layer, :, :gates_off])
        kcmp = _compress(p["kc"], cmp_pe_k[layer], cmp_k_w1[layer], cmp_k_w2[layer], b, s)
        vcmp = _compress(p["vc"], cmp_pe_v[layer], cmp_v_w1[layer], cmp_v_w2[layer], b, s)
        ya = _diff_attention(p["dq"], p["dk"], p["dv"], diff_lambda[layer], diff_subln_g[layer], b, s, layer)
        yb = _nsa_attention(p["nq"], p["ng"], kcmp, vcmp, p["ksd"], p["vsd"], p["kwd"], p["vwd"], b, s)
        h = _merge(h, attn_norm_g[layer], ya.reshape(b * s, -1), yb.reshape(b * s, -1),
                   w_in[layer, :, gates_off:], w_branch_a[layer], w_branch_b[layer], w_out[layer])
        h = _ffn(h, ffn_norm_g[layer], final_norm_g, w_gate[layer], w_up[layer], w_down[layer], layer == depth - 1)
    return h.reshape(b, s, d)
```

```python
import functools
import math

import jax
import jax.numpy as jnp
import numpy as np
from jax import lax
from jax.experimental import pallas as pl
from jax.experimental.pallas import tpu as pltpu

HEAD_DIM = 64
ROPE_THETA = 10000.0
NORM_EPS = 1e-6
NEG_INF = -1e30
FORCE_SCORE = 1e9

DIFF_HEADS = 4
DIFF_VDIM = 2 * HEAD_DIM
NSA_HEADS = 8
NSA_KV_GROUPS = 2
NSA_HPG = NSA_HEADS // NSA_KV_GROUPS
CMP_LEN = 32
CMP_STRIDE = 16
SEL_LEN = 64
SEL_TOPK = 16
WINDOW = 512

LANES = 128
VMEM_LIMIT = 56 * 1024 * 1024

BF16 = jnp.bfloat16
F32 = jnp.float32


def _lambda_init(layer):
    return 0.8 - 0.6 * math.exp(-0.3 * layer)


def _params(n_axes):
    return pltpu.CompilerParams(dimension_semantics=("arbitrary",) * n_axes, vmem_limit_bytes=VMEM_LIMIT)


def _const_spec(shape):
    nd = len(shape)
    return pl.BlockSpec(shape, lambda *_: (0,) * nd)


def _rope_table_kernel(pos_ref, invf_ref, cos_ref, sin_ref):
    ang = pos_ref[...].astype(F32) * invf_ref[...]
    cos_ref[...] = jnp.cos(ang)
    sin_ref[...] = jnp.sin(ang)


def _rope_tables(positions):
    half = HEAD_DIM // 2
    per_row = LANES // half
    t = positions.size
    rows = t // per_row
    pos_e = jnp.repeat(positions.reshape(rows, per_row), half, axis=1)
    inv_freq = 1.0 / (ROPE_THETA ** (jnp.arange(0, HEAD_DIM, 2, dtype=F32) / HEAD_DIM))
    invf = jnp.tile(inv_freq, per_row)[None, :]
    tr = min(rows, 1024)
    cos, sin = pl.pallas_call(
        _rope_table_kernel,
        grid=(rows // tr,),
        in_specs=[pl.BlockSpec((tr, LANES), lambda i: (i, 0)), _const_spec((1, LANES))],
        out_specs=[pl.BlockSpec((tr, LANES), lambda i: (i, 0))] * 2,
        out_shape=[jax.ShapeDtypeStruct((rows, LANES), F32)] * 2,
        compiler_params=_params(1),
        name="rope_tables",
    )(pos_e, invf)
    return cos.reshape(t, half), sin.reshape(t, half)


_ROPED_SEGS = (("dq", 512, BF16), ("dk", 512, BF16), ("nq", 512, BF16), ("ksd", 256, BF16), ("kwd", 256, BF16), ("kc", 128, F32))
_PLAIN_SEGS = (("dv", 512, BF16), ("vsd", 256, BF16), ("vwd", 256, BF16), ("vc", 128, F32), ("ng", 128, F32))


def _proj_kernel(x_ref, g_ref, cos_ref, sin_ref, wr_ref, wn_ref, *out_refs):
    x = x_ref[...]
    ms = jnp.mean(x * x, axis=-1, keepdims=True)
    u = (x * lax.rsqrt(ms + NORM_EPS) * g_ref[...]).astype(BF16)

    per_row = LANES // (HEAD_DIM // 2)
    c = jnp.concatenate([cos_ref[...]] * per_row, axis=1)
    s = jnp.concatenate([sin_ref[...]] * per_row, axis=1)
    lane = lax.broadcasted_iota(jnp.int32, c.shape, 1)
    upper = (lane % HEAD_DIM) >= (HEAD_DIM // 2)
    s_up = jnp.where(upper, s, 0.0)
    s_lo = jnp.where(upper, 0.0, -s)

    def rope(y):
        return y * c + pltpu.roll(y, HEAD_DIM // 2, 1) * s_up + pltpu.roll(y, LANES - HEAD_DIM // 2, 1) * s_lo

    refs = list(out_refs)
    for segs, w_ref, roped in ((_ROPED_SEGS, wr_ref, True), (_PLAIN_SEGS, wn_ref, False)):
        off = 0
        for name, width, _ in segs:
            o_ref = refs.pop(0)
            for c0 in range(0, width, 2 * LANES):
                cw = min(2 * LANES, width - c0)
                y = jnp.dot(u, w_ref[:, off + c0:off + c0 + cw], preferred_element_type=F32)
                for l0 in range(0, cw, LANES):
                    yl = y[:, l0:l0 + LANES]
                    if roped:
                        yl = rope(yl)
                    if name == "ng":
                        yl = jax.nn.sigmoid(yl)
                    o_ref[:, c0 + l0:c0 + l0 + LANES] = yl.astype(o_ref.dtype)
            off += width


def _dup_groups(w):
    d = w.shape[0]
    wg = w.reshape(d, NSA_KV_GROUPS, 1, HEAD_DIM)
    return jnp.broadcast_to(wg, (d, NSA_KV_GROUPS, 2, HEAD_DIM)).reshape(d, NSA_KV_GROUPS * 2 * HEAD_DIM)


def _project(x2, g, cos, sin, w_in):
    t, d = x2.shape
    qk = DIFF_HEADS * 2 * HEAD_DIM
    kv = NSA_KV_GROUPS * HEAD_DIM
    splits = (qk, qk, DIFF_HEADS * DIFF_VDIM, NSA_HEADS * HEAD_DIM, kv, kv, kv, kv, kv, kv, 3 * NSA_HEADS)
    offs = np.cumsum((0,) + splits)
    dq, dk, dv, nq, kc, vc, ks, vs, kw, vw, ng = (w_in[:, offs[i]:offs[i + 1]] for i in range(len(splits)))
    scale = HEAD_DIM ** -0.5
    ng = jnp.pad(ng, ((0, 0), (0, LANES - ng.shape[1])))
    wr = jnp.concatenate([dq * scale, dk, nq * scale, _dup_groups(ks), _dup_groups(kw), kc], axis=1).astype(BF16)
    wn = jnp.concatenate([dv, _dup_groups(vs), _dup_groups(vw), vc, ng], axis=1).astype(BF16)
    tm = min(t, 512)
    half = HEAD_DIM // 2
    segs = _ROPED_SEGS + _PLAIN_SEGS
    outs = pl.pallas_call(
        _proj_kernel,
        grid=(t // tm,),
        in_specs=[
            pl.BlockSpec((tm, d), lambda i: (i, 0)),
            _const_spec((1, d)),
            pl.BlockSpec((tm, half), lambda i: (i, 0)),
            pl.BlockSpec((tm, half), lambda i: (i, 0)),
            _const_spec(wr.shape),
            _const_spec(wn.shape),
        ],
        out_specs=[pl.BlockSpec((tm, w), lambda i: (i, 0)) for _, w, _ in segs],
        out_shape=[jax.ShapeDtypeStruct((t, w), dt) for _, w, dt in segs],
        compiler_params=_params(1),
        name="in_proj",
    )(x2, g[None, :], cos, sin, wr, wn)
    return dict(zip((n for n, _, _ in segs), outs))


def _compress_kernel(x_ref, pea_ref, peb_ref, w1a_ref, w1b_ref, w2_ref, o_ref):
    x = x_ref[...]
    a = jnp.dot((x + pea_ref[...]).astype(BF16), w1a_ref[...], preferred_element_type=F32)
    b = jnp.dot((x + peb_ref[...]).astype(BF16), w1b_ref[...], preferred_element_type=F32)
    ncp = x.shape[0]
    h = a + pltpu.roll(b, ncp - 1, 0)
    h = h * jax.nn.sigmoid(h)
    o_ref[...] = jnp.dot(h.astype(BF16), w2_ref[...], preferred_element_type=F32).astype(o_ref.dtype)


def _compress(xc, pe, w1, w2, b, s):
    g_, dh = NSA_KV_GROUPS, HEAD_DIM
    ncp = s // CMP_STRIDE
    hid = w1.shape[1]
    halves = CMP_LEN // CMP_STRIDE
    assert halves == 2
    xr = xc.reshape(b, ncp, CMP_STRIDE * g_ * dh)
    eye = jnp.eye(g_, dtype=F32)
    pe_r = pe.reshape(halves, CMP_STRIDE, 1, dh)
    pe_t = jnp.broadcast_to(pe_r, (halves, CMP_STRIDE, g_, dh)).reshape(halves, 1, CMP_STRIDE * g_ * dh)
    w1r = w1.reshape(halves, CMP_STRIDE, dh, hid)
    w1bd = jnp.einsum("hldj,pg->hlpdgj", w1r, eye).reshape(halves, CMP_STRIDE * g_ * dh, g_ * hid).astype(BF16)
    w2bd = jnp.einsum("jd,pg,r->pjgrd", w2, eye, jnp.ones((2,), F32)).reshape(g_ * hid, g_ * 2 * dh).astype(BF16)
    kdim = CMP_STRIDE * g_ * dh
    return pl.pallas_call(
        _compress_kernel,
        grid=(b,),
        in_specs=[
            pl.BlockSpec((None, ncp, kdim), lambda i: (i, 0, 0)),
            _const_spec((1, kdim)),
            _const_spec((1, kdim)),
            _const_spec((kdim, g_ * hid)),
            _const_spec((kdim, g_ * hid)),
            _const_spec((g_ * hid, g_ * 2 * dh)),
        ],
        out_specs=pl.BlockSpec((None, ncp, g_ * 2 * dh), lambda i: (i, 0, 0)),
        out_shape=jax.ShapeDtypeStruct((b, ncp, g_ * 2 * dh), BF16),
        compiler_params=_params(1),
        name="compress",
    )(xr, pe_t[0], pe_t[1], w1bd[0], w1bd[1], w2bd)


def _stack_halves(q):
    lane = lax.broadcasted_iota(jnp.int32, q.shape, 1)
    zero = jnp.zeros_like(q)
    return jnp.concatenate([jnp.where(lane < HEAD_DIM, q, zero), jnp.where(lane >= HEAD_DIM, q, zero)], axis=0)


def _flash_scratch(tk, m_cols, dv):
    row = pltpu.VMEM((1, m_cols), F32)
    return [pltpu.VMEM((tk, m_cols), F32), pltpu.VMEM((tk, m_cols), F32), row, row, row, row, pltpu.VMEM((dv, m_cols), F32)]


def _flash_tiles(qs, k_ref, v_ref, scratch, tk, lo, hi, mask_loop, mask_last):
    sa_ref, sb_ref, ta_ref, tb_ref, m_ref, l_ref, acc_ref = scratch
    dv = acc_ref.shape[0]
    bufs = {"a": (sa_ref, ta_ref), "b": (sb_ref, tb_ref)}

    def produce(t, slot, mask_fn):
        s_ref, t_ref = bufs[slot]
        k = k_ref[pl.ds(pl.multiple_of(t * tk, tk), tk), :]
        s = lax.dot_general(k, qs, (((1,), (1,)), ((), ())), preferred_element_type=F32)
        mask = mask_fn(t)
        if mask is not None:
            s = jnp.where(mask, s, NEG_INF)
        t_ref[...] = jnp.max(s, axis=0, keepdims=True)
        s_ref[...] = s

    def consume(t, slot):
        s_ref, t_ref = bufs[slot]
        m_old = m_ref[...]
        m_new = jnp.maximum(m_old, t_ref[...])
        alpha = jnp.exp(m_old - m_new)
        p = jnp.exp(s_ref[...] - m_new)
        l_ref[...] = alpha * l_ref[...] + jnp.sum(p, axis=0, keepdims=True)
        v = v_ref[pl.ds(pl.multiple_of(t * tk, tk), tk), :dv]
        pv = lax.dot_general(v, p.astype(BF16), (((0,), (0,)), ((), ())), preferred_element_type=F32)
        acc_ref[...] = alpha * acc_ref[...] + pv
        m_ref[...] = m_new

    m_ref[...] = jnp.full(m_ref.shape, NEG_INF, F32)
    l_ref[...] = jnp.zeros(l_ref.shape, F32)
    acc_ref[...] = jnp.zeros(acc_ref.shape, F32)
    n = hi - lo

    @pl.when(n >= 1)
    def _():
        produce(lo, "a", mask_loop)

    n_pairs = jnp.maximum(n - 1, 0) // 2

    def pair(j, carry):
        t = lo + 2 * j
        produce(t + 1, "b", mask_loop)
        consume(t, "a")
        produce(t + 2, "a", mask_loop)
        consume(t + 1, "b")
        return carry

    lax.fori_loop(0, n_pairs, pair, 0)
    t1 = lo + 2 * n_pairs

    @pl.when(n == 0)
    def _():
        produce(hi, "a", mask_last)
        consume(hi, "a")

    @pl.when((n >= 1) & (n % 2 == 1))
    def _():
        produce(hi, "b", mask_last)
        consume(t1, "a")
        consume(hi, "b")

    @pl.when((n >= 2) & (n % 2 == 0))
    def _():
        produce(t1 + 1, "b", mask_loop)
        consume(t1, "a")
        produce(hi, "a", mask_last)
        consume(t1 + 1, "b")
        consume(hi, "a")

    return l_ref[...], acc_ref[...]


def _diff_kernel(lam_ref, g_ref, q_ref, k_ref, v_ref, o_ref, *scratch, tq, tk, lam_init):
    qi = pl.program_id(2)
    qs = _stack_halves(q_ref[...])
    m_cols = 2 * tq
    qrel = lax.broadcasted_iota(jnp.int32, (1, m_cols), 1) % tq
    krel = lax.broadcasted_iota(jnp.int32, (tk, m_cols), 0)
    l, acc = _flash_tiles(qs, k_ref, v_ref, scratch, tk, 0, qi, lambda t: None, lambda t: krel <= qrel)
    o = acc / l
    lf = lam_ref[...]
    lam = (jnp.exp(jnp.sum(lf[0:1] * lf[1:2], axis=1, keepdims=True))
           - jnp.exp(jnp.sum(lf[2:3] * lf[3:4], axis=1, keepdims=True)) + lam_init)
    y = o[:, :tq] - lam * o[:, tq:]
    y = y * lax.rsqrt(jnp.mean(y * y, axis=0, keepdims=True) + NORM_EPS)
    y = y.T * g_ref[...] * (1.0 - lam_init)
    o_ref[...] = y.astype(o_ref.dtype)


def _diff_attention(dq, dk, dv, diff_lambda, subln_g, b, s, layer):
    tq = tk = min(256, s)
    hw = 2 * HEAD_DIM
    width = DIFF_HEADS * hw
    return pl.pallas_call(
        functools.partial(_diff_kernel, tq=tq, tk=tk, lam_init=_lambda_init(layer)),
        grid=(b, DIFF_HEADS, s // tq),
        in_specs=[
            _const_spec(diff_lambda.shape),
            _const_spec((1, DIFF_VDIM)),
            pl.BlockSpec((None, tq, hw), lambda bi, h, i: (bi, i, h)),
            pl.BlockSpec((None, s, hw), lambda bi, h, i: (bi, 0, h)),
            pl.BlockSpec((None, s, DIFF_VDIM), lambda bi, h, i: (bi, 0, h)),
        ],
        out_specs=pl.BlockSpec((None, tq, DIFF_VDIM), lambda bi, h, i: (bi, i, h)),
        out_shape=jax.ShapeDtypeStruct((b, s, DIFF_HEADS * DIFF_VDIM), BF16),
        scratch_shapes=_flash_scratch(tk, 2 * tq, DIFF_VDIM),
        compiler_params=_params(3),
        name="diff_attn",
    )(diff_lambda, subln_g[None, :], dq.reshape(b, s, width), dk.reshape(b, s, width), dv.reshape(b, s, width))


def _nsa_kernel(ovl_ref, q_ref, gate_ref, kc_ref, vc_ref, ks_ref, vs_ref, kw_ref, vw_ref, o_ref, imp_ref, sel_ref, *scratch, tq, tk):
    qi = pl.program_id(2)
    grp = pl.program_id(1)
    q0 = qi * tq
    m_cols = NSA_HPG * tq
    q = q_ref[...]
    qs = jnp.concatenate([_stack_halves(q[:, :LANES]), _stack_halves(q[:, LANES:])], axis=0)
    qpos_t = q0 + lax.broadcasted_iota(jnp.int32, (1, tq), 1)
    qpos = jnp.concatenate([qpos_t] * NSA_HPG, axis=1)
    dh = HEAD_DIM

    ncp = kc_ref.shape[0]
    cmp_end = lax.broadcasted_iota(jnp.int32, (ncp, m_cols), 0) * CMP_STRIDE + (CMP_LEN - 1)
    cmask = cmp_end <= qpos
    sc = lax.dot_general(kc_ref[...], qs, (((1,), (1,)), ((), ())), preferred_element_type=F32)
    sc = jnp.where(cmask, sc, NEG_INF)
    pc = jnp.exp(sc - jnp.max(sc, axis=0, keepdims=True))
    pc = pc / jnp.sum(pc, axis=0, keepdims=True)
    pc = jnp.where(cmask, pc, 0.0)
    o_c = lax.dot_general(vc_ref[...], pc.astype(BF16), (((0,), (0,)), ((), ())), preferred_element_type=F32)[:dh]

    pc_sum = pc[:, 0:tq]
    for h in range(1, NSA_HPG):
        pc_sum = pc_sum + pc[:, h * tq:(h + 1) * tq]
    imp = jnp.dot(ovl_ref[...], pc_sum, preferred_element_type=F32, precision=lax.Precision.HIGHEST)
    nsb = imp.shape[0]
    blk = lax.broadcasted_iota(jnp.int32, (nsb, tq), 0)
    qblk = qpos_t // SEL_LEN
    forced = (blk == 0) | (blk == qblk) | (blk == qblk - 1)
    imp = jnp.where(forced, FORCE_SCORE, imp)
    imp = jnp.where(blk <= qblk, imp, NEG_INF)
    imp_ref[...] = imp
    topk = min(SEL_TOPK, nsb)

    def rank_step(j, cnt):
        row = imp_ref[pl.ds(j, 1), :]
        beats = (row > imp) | ((row == imp) & (j < blk))
        return cnt + beats.astype(jnp.int32)

    n_causal_blk = (q0 + tq - 1) // SEL_LEN + 1
    cnt = lax.fori_loop(0, n_causal_blk, rank_step, jnp.zeros((nsb, tq), jnp.int32))
    sel_ref[...] = jnp.where(cnt < topk, 1.0, 0.0)

    krel = lax.broadcasted_iota(jnp.int32, (tk, m_cols), 0)
    bpt = tk // SEL_LEN

    def chosen(kb):
        rows = []
        for r in range(bpt):
            srow = sel_ref[pl.ds(kb * bpt + r, 1), :]
            srow = jnp.concatenate([srow] * NSA_HPG, axis=1)
            rows.append(jnp.broadcast_to(srow, (SEL_LEN, m_cols)))
        return jnp.concatenate(rows, axis=0) > 0.5

    last = (q0 + tq - 1) // tk
    l_s, acc_s = _flash_tiles(qs, ks_ref, vs_ref, scratch, tk, 0, last, chosen,
                              lambda kb: chosen(kb) & (krel <= qpos - kb * tk))
    o_s = acc_s / l_s

    def in_window(kb):
        dist = (qpos - kb * tk) - krel
        return (dist >= 0) & (dist < WINDOW)

    first = jnp.maximum(q0 - (WINDOW - 1), 0) // tk
    l_w, acc_w = _flash_tiles(qs, kw_ref, vw_ref, scratch, tk, first, last, in_window, in_window)
    o_w = acc_w / l_w

    gates = gate_ref[...].T
    outs = []
    for h in range(NSA_HPG):
        hg = grp * NSA_HPG + h
        sl = slice(h * tq, (h + 1) * tq)
        y = (_gate_row(gates, 0, hg) * o_c[:, sl] + _gate_row(gates, 1, hg) * o_s[:, sl] + _gate_row(gates, 2, hg) * o_w[:, sl])
        outs.append(y)
    y = jnp.concatenate(outs, axis=0)
    o_ref[...] = y.T.astype(o_ref.dtype)


def _gate_row(gates, branch, head):
    idx = branch * NSA_HEADS + head
    row = lax.broadcasted_iota(jnp.int32, gates.shape, 0)
    return jnp.sum(jnp.where(row == idx, gates, 0.0), axis=0, keepdims=True)


def _overlap_matrix(s):
    nc = s // CMP_STRIDE
    nsb = s // SEL_LEN
    ci = np.arange(nc)[None, :] * CMP_STRIDE
    sj = np.arange(nsb)[:, None] * SEL_LEN
    ovl = ((ci < sj + SEL_LEN) & (ci + CMP_LEN > sj)).astype(np.float32)
    ovl[:, (s - CMP_LEN) // CMP_STRIDE + 1:] = 0.0
    return jnp.asarray(ovl)


def _nsa_attention(nq, ng, kcmp, vcmp, ksd, vsd, kwd, vwd, b, s):
    tq = min(128, s)
    tk = min(256, s)
    gw = NSA_HPG * HEAD_DIM
    kw_ = 2 * HEAD_DIM
    ncp = s // CMP_STRIDE
    nsb = s // SEL_LEN
    seq3 = lambda a: a.reshape(b, s, a.shape[-1])
    kv_spec = pl.BlockSpec((None, s, kw_), lambda bi, g, i: (bi, 0, g))
    cmp_spec = pl.BlockSpec((None, ncp, kw_), lambda bi, g, i: (bi, 0, g))
    return pl.pallas_call(
        functools.partial(_nsa_kernel, tq=tq, tk=tk),
        grid=(b, NSA_KV_GROUPS, s // tq),
        in_specs=[
            _const_spec((nsb, ncp)),
            pl.BlockSpec((None, tq, gw), lambda bi, g, i: (bi, i, g)),
            pl.BlockSpec((None, tq, LANES), lambda bi, g, i: (bi, i, 0)),
            cmp_spec, cmp_spec, kv_spec, kv_spec, kv_spec, kv_spec,
        ],
        out_specs=pl.BlockSpec((None, tq, gw), lambda bi, g, i: (bi, i, g)),
        out_shape=jax.ShapeDtypeStruct((b, s, NSA_HEADS * HEAD_DIM), BF16),
        scratch_shapes=[pltpu.VMEM((nsb, tq), F32), pltpu.VMEM((nsb, tq), F32)] + _flash_scratch(tk, NSA_HPG * tq, HEAD_DIM),
        compiler_params=_params(3),
        name="nsa_attn",
    )(_overlap_matrix(s), seq3(nq), seq3(ng), kcmp, vcmp, seq3(ksd), seq3(vsd), seq3(kwd), seq3(vwd))


def _merge_kernel(x_ref, g_ref, ya_ref, yb_ref, wmg_ref, wa_ref, wb_ref, wo_ref, o_ref):
    x = x_ref[...]
    d = x.shape[1]
    ms = jnp.mean(x * x, axis=-1, keepdims=True)
    u = (x * lax.rsqrt(ms + NORM_EPS) * g_ref[...]).astype(BF16)
    ya = jnp.dot(ya_ref[...], wa_ref[...], preferred_element_type=F32)
    yb = jnp.dot(yb_ref[...], wb_ref[...], preferred_element_type=F32)
    mg_a = jax.nn.sigmoid(jnp.dot(u, wmg_ref[:, :d], preferred_element_type=F32))
    mg_b = jax.nn.sigmoid(jnp.dot(u, wmg_ref[:, d:], preferred_element_type=F32))
    merged = (mg_a * ya + mg_b * yb).astype(BF16)
    o_ref[...] = x + jnp.dot(merged, wo_ref[...], preferred_element_type=F32)


def _merge(x2, g, ya, yb, w_mg, w_a, w_b, w_o):
    t, d = x2.shape
    tm = min(t, 512)
    ws = [w.astype(BF16) for w in (w_mg, w_a, w_b, w_o)]
    return pl.pallas_call(
        _merge_kernel,
        grid=(t // tm,),
        in_specs=[
            pl.BlockSpec((tm, d), lambda i: (i, 0)),
            _const_spec((1, d)),
            pl.BlockSpec((tm, ya.shape[1]), lambda i: (i, 0)),
            pl.BlockSpec((tm, yb.shape[1]), lambda i: (i, 0)),
        ] + [_const_spec(w.shape) for w in ws],
        out_specs=pl.BlockSpec((tm, d), lambda i: (i, 0)),
        out_shape=jax.ShapeDtypeStruct((t, d), F32),
        compiler_params=_params(1),
        name="merge_out_proj",
    )(x2, g[None, :], ya, yb, *ws)


def _ffn_kernel(h_ref, g_ref, gf_ref, wg_ref, wu_ref, wd_ref, o_ref, *, chunk, final_norm):
    h = h_ref[...]
    ms = jnp.mean(h * h, axis=-1, keepdims=True)
    u = (h * lax.rsqrt(ms + NORM_EPS) * g_ref[...]).astype(BF16)
    d_ff = wg_ref.shape[1]
    acc = h
    for c0 in range(0, d_ff, chunk):
        a = jnp.dot(u, wg_ref[:, c0:c0 + chunk], preferred_element_type=F32)
        up = jnp.dot(u, wu_ref[:, c0:c0 + chunk], preferred_element_type=F32)
        act = (a * jax.nn.sigmoid(a) * up).astype(BF16)
        acc = acc + jnp.dot(act, wd_ref[c0:c0 + chunk, :], preferred_element_type=F32)
    if final_norm:
        ms = jnp.mean(acc * acc, axis=-1, keepdims=True)
        acc = acc * lax.rsqrt(ms + NORM_EPS) * gf_ref[...]
    o_ref[...] = acc


def _ffn(h2, g, g_final, w_gate, w_up, w_down, final_norm):
    t, d = h2.shape
    tm = min(t, 512)
    d_ff = w_gate.shape[1]
    chunk = 256
    assert d_ff % chunk == 0
    ws = [w.astype(BF16) for w in (w_gate, w_up, w_down)]
    return pl.pallas_call(
        functools.partial(_ffn_kernel, chunk=chunk, final_norm=final_norm),
        grid=(t // tm,),
        in_specs=[pl.BlockSpec((tm, d), lambda i: (i, 0)), _const_spec((1, d)), _const_spec((1, d))]
        + [_const_spec(w.shape) for w in ws],
        out_specs=pl.BlockSpec((tm, d), lambda i: (i, 0)),
        out_shape=jax.ShapeDtypeStruct((t, d), F32),
        compiler_params=_params(1),
        name="swiglu_ffn",
    )(h2, g[None, :], g_final[None, :], *ws)


def kernel(x, positions, attn_norm_g, w_in, diff_lambda, diff_subln_g, cmp_pe_k, cmp_pe_v, cmp_k_w1, cmp_k_w2, cmp_v_w1, cmp_v_w2, w_branch_a, w_branch_b, w_out, ffn_norm_g, w_gate, w_up, w_down, final_norm_g):
    b, s, d = x.shape
    depth = w_in.shape[0]
    cos, sin = _rope_tables(positions)
    h = x.reshape(b * s, d)
    gates_off = w_in.shape[2] - 2 * d
    for layer in range(depth):
        p = _project(h, attn_norm_g[layer], cos, sin, w_in[layer, :, :gates_off])
        kcmp = _compress(p["kc"], cmp_pe_k[layer], cmp_k_w1[layer], cmp_k_w2[layer], b, s)
        vcmp = _compress(p["vc"], cmp_pe_v[layer], cmp_v_w1[layer], cmp_v_w2[layer], b, s)
        ya = _diff_attention(p["dq"], p["dk"], p["dv"], diff_lambda[layer], diff_subln_g[layer], b, s, layer)
        yb = _nsa_attention(p["nq"], p["ng"], kcmp, vcmp, p["ksd"], p["vsd"], p["kwd"], p["vwd"], b, s)
        h = _merge(h, attn_norm_g[layer], ya.reshape(b * s, -1), yb.reshape(b * s, -1),
                   w_in[layer, :, gates_off:], w_branch_a[layer], w_branch_b[layer], w_out[layer])
        h = _ffn(h, ffn_norm_g[layer], final_norm_g, w_gate[layer], w_up[layer], w_down[layer], layer == depth - 1)
    return h.reshape(b, s, d)
```

```python
import functools
import math

import jax
import jax.numpy as jnp
import numpy as np
from jax import lax
from jax.experimental import pallas as pl
from jax.experimental.pallas import tpu as pltpu

HEAD_DIM = 64
ROPE_THETA = 10000.0
NORM_EPS = 1e-6
NEG_INF = -1e30
FORCE_SCORE = 1e9

DIFF_HEADS = 4
DIFF_VDIM = 2 * HEAD_DIM
NSA_HEADS = 8
NSA_KV_GROUPS = 2
NSA_HPG = NSA_HEADS // NSA_KV_GROUPS
CMP_LEN = 32
CMP_STRIDE = 16
SEL_LEN = 64
SEL_TOPK = 16
WINDOW = 512

LANES = 128
VMEM_LIMIT = 56 * 1024 * 1024
KEY_TILE = 256

BF16 = jnp.bfloat16
F32 = jnp.float32


def _lambda_init(layer):
    return 0.8 - 0.6 * math.exp(-0.3 * layer)


def _params(n_axes):
    return pltpu.CompilerParams(dimension_semantics=("arbitrary",) * n_axes, vmem_limit_bytes=VMEM_LIMIT)


def _const_spec(shape):
    nd = len(shape)
    return pl.BlockSpec(shape, lambda *_: (0,) * nd)


def _rope_table_kernel(pos_ref, invf_ref, cos_ref, sin_ref):
    ang = pos_ref[...].astype(F32) * invf_ref[...]
    cos_ref[...] = jnp.cos(ang)
    sin_ref[...] = jnp.sin(ang)


def _rope_tables(positions):
    half = HEAD_DIM // 2
    per_row = LANES // half
    t = positions.size
    rows = t // per_row
    pos_e = jnp.repeat(positions.reshape(rows, per_row), half, axis=1)
    inv_freq = 1.0 / (ROPE_THETA ** (jnp.arange(0, HEAD_DIM, 2, dtype=F32) / HEAD_DIM))
    invf = jnp.tile(inv_freq, per_row)[None, :]
    tr = min(rows, 1024)
    cos, sin = pl.pallas_call(
        _rope_table_kernel,
        grid=(rows // tr,),
        in_specs=[pl.BlockSpec((tr, LANES), lambda i: (i, 0)), _const_spec((1, LANES))],
        out_specs=[pl.BlockSpec((tr, LANES), lambda i: (i, 0))] * 2,
        out_shape=[jax.ShapeDtypeStruct((rows, LANES), F32)] * 2,
        compiler_params=_params(1),
        name="rope_tables",
    )(pos_e, invf)
    return cos.reshape(t, half), sin.reshape(t, half)


_ROW_SEGS = (("dk", 512, BF16, True), ("ksd", 256, BF16, True), ("kwd", 256, BF16, True), ("kc", 128, F32, True),
             ("vc", 128, F32, False), ("ng", 128, F32, False))
_COL_SEGS = (("dqT", 512, True, False), ("nqT", 512, True, False),
             ("dvT", 512, False, True), ("vsT", 128, False, True), ("vwT", 128, False, True))


def _proj_kernel(x_ref, g_ref, cos_ref, sin_ref, cost_ref, sint_ref, wr_ref, wt_ref, *out_refs):
    x = x_ref[...]
    ms = jnp.mean(x * x, axis=-1, keepdims=True)
    u = (x * lax.rsqrt(ms + NORM_EPS) * g_ref[...]).astype(BF16)
    half = HEAD_DIM // 2
    reps = LANES // half

    def rope_fn(c, s, axis):
        idx = lax.broadcasted_iota(jnp.int32, c.shape, axis)
        upper = (idx % HEAD_DIM) >= half
        s_up = jnp.where(upper, s, 0.0)
        s_lo = jnp.where(upper, 0.0, -s)
        return lambda y: y * c + pltpu.roll(y, half, axis) * s_up + pltpu.roll(y, LANES - half, axis) * s_lo

    rope_rows = rope_fn(jnp.concatenate([cos_ref[...]] * reps, axis=1), jnp.concatenate([sin_ref[...]] * reps, axis=1), 1)
    rope_cols = rope_fn(jnp.concatenate([cost_ref[...]] * reps, axis=0), jnp.concatenate([sint_ref[...]] * reps, axis=0), 0)

    refs = list(out_refs)
    off = 0
    for name, width, _, roped in _ROW_SEGS:
        o_ref = refs.pop(0)
        for c0 in range(0, width, 2 * LANES):
            cw = min(2 * LANES, width - c0)
            y = jnp.dot(u, wr_ref[:, off + c0:off + c0 + cw], preferred_element_type=F32)
            for l0 in range(0, cw, LANES):
                yl = y[:, l0:l0 + LANES]
                if roped:
                    yl = rope_rows(yl)
                if name == "ng":
                    yl = jax.nn.sigmoid(yl)
                o_ref[:, c0 + l0:c0 + l0 + LANES] = yl.astype(o_ref.dtype)
        off += width

    off = 0
    tm = x.shape[0]
    for name, rows, roped, tiled in _COL_SEGS:
        o_ref = refs.pop(0)
        for r0 in range(0, rows, 2 * LANES):
            rw = min(2 * LANES, rows - r0)
            yt = lax.dot_general(wt_ref[off + r0:off + r0 + rw, :], u, (((1,), (1,)), ((), ())), preferred_element_type=F32)
            for s0 in range(0, rw, LANES):
                ys = yt[s0:s0 + LANES]
                if roped:
                    ys = rope_cols(ys)
                ys = ys.astype(o_ref.dtype)
                if tiled:
                    for j in range(tm // KEY_TILE):
                        o_ref[j, r0 + s0:r0 + s0 + LANES, :] = ys[:, j * KEY_TILE:(j + 1) * KEY_TILE]
                else:
                    o_ref[r0 + s0:r0 + s0 + LANES, :] = ys
        off += rows


def _dup_groups(w):
    d = w.shape[0]
    wg = w.reshape(d, NSA_KV_GROUPS, 1, HEAD_DIM)
    return jnp.broadcast_to(wg, (d, NSA_KV_GROUPS, 2, HEAD_DIM)).reshape(d, NSA_KV_GROUPS * 2 * HEAD_DIM)


def _project(x2, g, cos, sin, w_in):
    t, d = x2.shape
    qk = DIFF_HEADS * 2 * HEAD_DIM
    kv = NSA_KV_GROUPS * HEAD_DIM
    splits = (qk, qk, DIFF_HEADS * DIFF_VDIM, NSA_HEADS * HEAD_DIM, kv, kv, kv, kv, kv, kv, 3 * NSA_HEADS)
    offs = np.cumsum((0,) + splits)
    dq, dk, dv, nq, kc, vc, ks, vs, kw, vw, ng = (w_in[:, offs[i]:offs[i + 1]] for i in range(len(splits)))
    scale = HEAD_DIM ** -0.5 * math.log2(math.e)
    ng = jnp.pad(ng, ((0, 0), (0, LANES - ng.shape[1])))
    wr = jnp.concatenate([dk, _dup_groups(ks), _dup_groups(kw), kc, vc, ng], axis=1).astype(BF16)
    wt = jnp.concatenate([dq * scale, nq * scale, dv, vs, vw], axis=1).T.astype(BF16)
    tm = min(t, 512)
    assert tm % KEY_TILE == 0
    half = HEAD_DIM // 2
    out_specs, out_shape = [], []
    for _, w, dt, _ in _ROW_SEGS:
        out_specs.append(pl.BlockSpec((tm, w), lambda i: (i, 0)))
        out_shape.append(jax.ShapeDtypeStruct((t, w), dt))
    for _, rows, _, tiled in _COL_SEGS:
        if tiled:
            out_specs.append(pl.BlockSpec((tm // KEY_TILE, rows, KEY_TILE), lambda i: (i, 0, 0)))
            out_shape.append(jax.ShapeDtypeStruct((t // KEY_TILE, rows, KEY_TILE), BF16))
        else:
            out_specs.append(pl.BlockSpec((rows, tm), lambda i: (0, i)))
            out_shape.append(jax.ShapeDtypeStruct((rows, t), BF16))
    outs = pl.pallas_call(
        _proj_kernel,
        grid=(t // tm,),
        in_specs=[
            pl.BlockSpec((tm, d), lambda i: (i, 0)),
            _const_spec((1, d)),
            pl.BlockSpec((tm, half), lambda i: (i, 0)),
            pl.BlockSpec((tm, half), lambda i: (i, 0)),
            pl.BlockSpec((half, tm), lambda i: (0, i)),
            pl.BlockSpec((half, tm), lambda i: (0, i)),
            _const_spec(wr.shape),
            _const_spec(wt.shape),
        ],
        out_specs=out_specs,
        out_shape=out_shape,
        compiler_params=_params(1),
        name="in_proj",
    )(x2, g[None, :], cos, sin, cos.T, sin.T, wr, wt)
    return dict(zip([n for n, *_ in _ROW_SEGS] + [n for n, *_ in _COL_SEGS], outs))


def _compress_kernel(x_ref, pea_ref, peb_ref, w1a_ref, w1b_ref, w2_ref, o_ref):
    x = x_ref[...]
    a = jnp.dot((x + pea_ref[...]).astype(BF16), w1a_ref[...], preferred_element_type=F32)
    b = jnp.dot((x + peb_ref[...]).astype(BF16), w1b_ref[...], preferred_element_type=F32)
    ncp = x.shape[0]
    h = a + pltpu.roll(b, ncp - 1, 0)
    h = h * jax.nn.sigmoid(h)
    o_ref[...] = jnp.dot(h.astype(BF16), w2_ref[...], preferred_element_type=F32).astype(o_ref.dtype)


def _compress(xc, pe, w1, w2, b, s):
    g_, dh = NSA_KV_GROUPS, HEAD_DIM
    ncp = s // CMP_STRIDE
    hid = w1.shape[1]
    halves = CMP_LEN // CMP_STRIDE
    assert halves == 2
    xr = xc.reshape(b, ncp, CMP_STRIDE * g_ * dh)
    eye = jnp.eye(g_, dtype=F32)
    pe_r = pe.reshape(halves, CMP_STRIDE, 1, dh)
    pe_t = jnp.broadcast_to(pe_r, (halves, CMP_STRIDE, g_, dh)).reshape(halves, 1, CMP_STRIDE * g_ * dh)
    w1r = w1.reshape(halves, CMP_STRIDE, dh, hid)
    w1bd = jnp.einsum("hldj,pg->hlpdgj", w1r, eye).reshape(halves, CMP_STRIDE * g_ * dh, g_ * hid).astype(BF16)
    w2bd = jnp.einsum("jd,pg,r->pjgrd", w2, eye, jnp.ones((2,), F32)).reshape(g_ * hid, g_ * 2 * dh).astype(BF16)
    kdim = CMP_STRIDE * g_ * dh
    return pl.pallas_call(
        _compress_kernel,
        grid=(b,),
        in_specs=[
            pl.BlockSpec((None, ncp, kdim), lambda i: (i, 0, 0)),
            _const_spec((1, kdim)),
            _const_spec((1, kdim)),
            _const_spec((kdim, g_ * hid)),
            _const_spec((kdim, g_ * hid)),
            _const_spec((g_ * hid, g_ * 2 * dh)),
        ],
        out_specs=pl.BlockSpec((None, ncp, g_ * 2 * dh), lambda i: (i, 0, 0)),
        out_shape=jax.ShapeDtypeStruct((b, ncp, g_ * 2 * dh), BF16),
        compiler_params=_params(1),
        name="compress",
    )(xr, pe_t[0], pe_t[1], w1bd[0], w1bd[1], w2bd)


def _split_heads(qt):
    row = lax.broadcasted_iota(jnp.int32, qt.shape, 0)
    zero = jnp.zeros_like(qt)
    return jnp.concatenate([jnp.where(row < HEAD_DIM, qt, zero), jnp.where(row >= HEAD_DIM, qt, zero)], axis=1)


def _flash_scratch(tk, m_cols, dv):
    row = pltpu.VMEM((1, m_cols), F32)
    sc = pltpu.VMEM((tk, m_cols), F32)
    pr = pltpu.VMEM((tk, m_cols), BF16)
    return [sc, sc, row, row, pr, pr, row, row, row, row, pltpu.VMEM((dv, m_cols), F32)]


def _flash_tiles(qst, k_ref, vt_ref, scratch, tk, lo, hi, mask_fn, inner_fn):
    sa_ref, sb_ref, ta_ref, tb_ref, pa_ref, pb_ref, ala_ref, alb_ref, m_ref, l_ref, acc_ref = scratch
    slots = {"a": (sa_ref, ta_ref, pa_ref, ala_ref), "b": (sb_ref, tb_ref, pb_ref, alb_ref)}
    n_k = k_ref.shape[0] // tk

    def clamp(t):
        return jnp.clip(t, 0, n_k - 1)

    def scores(t, slot, fn):
        s_ref, t_ref, _, _ = slots[slot]
        k = k_ref[pl.ds(pl.multiple_of(clamp(t) * tk, tk), tk), :]
        s = fn(t, jnp.dot(k, qst, preferred_element_type=F32))
        t_ref[...] = jnp.max(s, axis=0, keepdims=True)
        s_ref[...] = s

    def probs(slot):
        s_ref, t_ref, p_ref, al_ref = slots[slot]
        m_old = m_ref[...]
        m_new = jnp.maximum(m_old, t_ref[...])
        alpha = jnp.exp2(m_old - m_new)
        p = jnp.exp2(s_ref[...] - m_new)
        l_ref[...] = alpha * l_ref[...] + jnp.sum(p, axis=0, keepdims=True)
        p_ref[...] = p.astype(BF16)
        al_ref[...] = alpha
        m_ref[...] = m_new

    def values(t, slot):
        _, _, p_ref, al_ref = slots[slot]
        pv = jnp.dot(vt_ref[clamp(t)], p_ref[...], preferred_element_type=F32)
        acc_ref[...] = al_ref[...] * acc_ref[...] + pv

    def step(j, fn):
        u = lo + 2 * j
        values(u - 2, "a")
        values(u - 1, "b")
        probs("a")
        probs("b")
        scores(u + 2, "a", fn)
        scores(u + 3, "b", fn)

    m_ref[...] = jnp.full(m_ref.shape, NEG_INF, F32)
    l_ref[...] = jnp.zeros(l_ref.shape, F32)
    acc_ref[...] = jnp.zeros(acc_ref.shape, F32)
    for slot in ("a", "b"):
        slots[slot][2][...] = jnp.zeros(pa_ref.shape, BF16)
        slots[slot][3][...] = jnp.ones(ala_ref.shape, F32)

    scores(lo, "a", mask_fn)
    scores(lo + 1, "b", mask_fn)
    n_pairs = (hi - lo) // 2 + 1

    def body(j, carry):
        step(j, inner_fn)
        return carry

    lax.fori_loop(0, n_pairs - 2, body, 0)

    @pl.when(n_pairs >= 2)
    def _():
        step(n_pairs - 2, mask_fn)

    u = lo + 2 * (n_pairs - 1)
    values(u - 2, "a")
    values(u - 1, "b")
    probs("a")
    probs("b")
    values(u, "a")
    values(u + 1, "b")
    return l_ref[...], acc_ref[...]


def _diff_kernel(lam_ref, g_ref, qt_ref, k_ref, vt_ref, o_ref, *scratch, tq, tk, lam_init):
    qi = pl.program_id(2)
    qst = _split_heads(qt_ref[...])
    m_cols = 2 * tq
    qpos = qi * tq + lax.broadcasted_iota(jnp.int32, (1, m_cols), 1) % tq
    krel = lax.broadcasted_iota(jnp.int32, (tk, m_cols), 0)

    def causal(t, s):
        return jnp.where(krel <= qpos - t * tk, s, NEG_INF)

    l, acc = _flash_tiles(qst, k_ref, vt_ref, scratch, tk, 0, (qi * tq + tq - 1) // tk, causal, lambda t, s: s)
    o = acc * (1.0 / l)
    lf = lam_ref[...]
    lam = (jnp.exp(jnp.sum(lf[0:1] * lf[1:2], axis=1, keepdims=True))
           - jnp.exp(jnp.sum(lf[2:3] * lf[3:4], axis=1, keepdims=True)) + lam_init)
    y = o[:, :tq] - lam * o[:, tq:]
    y = y * lax.rsqrt(jnp.mean(y * y, axis=0, keepdims=True) + NORM_EPS)
    y = y.T * g_ref[...] * (1.0 - lam_init)
    o_ref[...] = y.astype(o_ref.dtype)


def _diff_attention(dqt, dk, dvt, diff_lambda, subln_g, b, s, layer):
    tk = KEY_TILE
    tq = 2 * tk
    hw = 2 * HEAD_DIM
    width = DIFF_HEADS * hw
    nq = s // tq
    return pl.pallas_call(
        functools.partial(_diff_kernel, tq=tq, tk=tk, lam_init=_lambda_init(layer)),
        grid=(b, DIFF_HEADS, nq),
        in_specs=[
            _const_spec(diff_lambda.shape),
            _const_spec((1, DIFF_VDIM)),
            pl.BlockSpec((hw, tq), lambda bi, h, i: (h, bi * nq + i)),
            pl.BlockSpec((None, s, hw), lambda bi, h, i: (bi, 0, h)),
            pl.BlockSpec((None, s // tk, DIFF_VDIM, tk), lambda bi, h, i: (bi, 0, h, 0)),
        ],
        out_specs=pl.BlockSpec((None, tq, DIFF_VDIM), lambda bi, h, i: (bi, i, h)),
        out_shape=jax.ShapeDtypeStruct((b, s, DIFF_HEADS * DIFF_VDIM), BF16),
        scratch_shapes=_flash_scratch(tk, 2 * tq, DIFF_VDIM),
        compiler_params=_params(3),
        name="diff_attn",
    )(diff_lambda, subln_g[None, :], dqt, dk.reshape(b, s, width), dvt.reshape(b, s // tk, width, tk))


def _nsa_kernel(ovl_ref, qt_ref, gate_ref, kc_ref, vc_ref, ks_ref, vst_ref, kw_ref, vwt_ref, o_ref, imp_ref, selb_ref, *scratch, tq, tk):
    qi = pl.program_id(2)
    grp = pl.program_id(1)
    q0 = qi * tq
    m_cols = NSA_HPG * tq
    qt = qt_ref[...]
    qst = jnp.concatenate([_split_heads(qt[:LANES]), _split_heads(qt[LANES:])], axis=1)
    qpos_t = q0 + lax.broadcasted_iota(jnp.int32, (1, tq), 1)
    qpos = jnp.concatenate([qpos_t] * NSA_HPG, axis=1)
    dh = HEAD_DIM

    ncp = kc_ref.shape[0]
    cmp_end = lax.broadcasted_iota(jnp.int32, (ncp, m_cols), 0) * CMP_STRIDE + (CMP_LEN - 1)
    cmask = cmp_end <= qpos
    sc = jnp.dot(kc_ref[...], qst, preferred_element_type=F32)
    sc = jnp.where(cmask, sc, NEG_INF)
    pc = jnp.exp2(sc - jnp.max(sc, axis=0, keepdims=True))
    pc = pc * jnp.where(qpos >= CMP_LEN - 1, 1.0 / jnp.sum(pc, axis=0, keepdims=True), 0.0)
    o_c = lax.dot_general(vc_ref[...], pc.astype(BF16), (((0,), (0,)), ((), ())), preferred_element_type=F32)[:dh]

    pc_sum = pc[:, 0:tq]
    for h in range(1, NSA_HPG):
        pc_sum = pc_sum + pc[:, h * tq:(h + 1) * tq]
    imp = jnp.dot(ovl_ref[...], pc_sum, preferred_element_type=F32, precision=lax.Precision.HIGHEST)
    nsb = imp.shape[0]
    blk = lax.broadcasted_iota(jnp.int32, (nsb, tq), 0)
    qblk = qpos_t // SEL_LEN
    forced = (blk == 0) | (blk == qblk) | (blk == qblk - 1)
    imp = jnp.where(forced, FORCE_SCORE, imp)
    imp = jnp.where(blk <= qblk, imp, NEG_INF)
    imp_ref[...] = imp
    topk = min(SEL_TOPK, nsb)

    def rank_step(j, cnt):
        row = imp_ref[pl.ds(j, 1), :]
        beats = (row > imp) | ((row == imp) & (j < blk))
        return cnt + beats.astype(jnp.int32)

    n_causal_blk = (q0 + tq - 1) // SEL_LEN + 1
    cnt = lax.fori_loop(0, n_causal_blk, rank_step, jnp.zeros((nsb, tq), jnp.int32))
    selb_ref[...] = jnp.where(cnt < topk, 0.0, NEG_INF)

    krel = lax.broadcasted_iota(jnp.int32, (tk, m_cols), 0)
    bpt = tk // SEL_LEN
    last = (q0 + tq - 1) // tk

    def biased(t, s):
        first_blk = jnp.clip(t, 0, nsb // bpt - 1) * bpt
        rows = []
        for r in range(bpt):
            brow = selb_ref[pl.ds(first_blk + r, 1), :]
            brow = jnp.concatenate([brow] * NSA_HPG, axis=1)
            rows.append(jnp.broadcast_to(brow, (SEL_LEN, m_cols)))
        return s + jnp.concatenate(rows, axis=0)

    def selected(t, s):
        return jnp.where(krel <= qpos - t * tk, biased(t, s), NEG_INF)

    l_s, acc_s = _flash_tiles(qst, ks_ref, vst_ref, scratch, tk, 0, last, selected, biased)
    o_s = acc_s * (1.0 / l_s)

    n_back = -(-(WINDOW - 1) // tk)
    win_tiles = [qi - n_back + r for r in range(n_back + 1)]
    s_w = []
    for t in win_tiles:
        k = kw_ref[pl.ds(pl.multiple_of(jnp.maximum(t, 0) * tk, tk), tk), :]
        dist = (qpos - t * tk) - krel
        width = jnp.where(t >= 0, WINDOW, 0)
        s_w.append(jnp.where((dist >= 0) & (dist < width), jnp.dot(k, qst, preferred_element_type=F32), NEG_INF))
    m_w = functools.reduce(jnp.maximum, [jnp.max(s, axis=0, keepdims=True) for s in s_w])
    l_w = jnp.zeros((1, m_cols), F32)
    acc_w = jnp.zeros((dh, m_cols), F32)
    for t, s in zip(win_tiles, s_w):
        p = jnp.exp2(s - m_w)
        l_w = l_w + jnp.sum(p, axis=0, keepdims=True)
        acc_w = acc_w + jnp.dot(vwt_ref[jnp.maximum(t, 0)], p.astype(BF16), preferred_element_type=F32)
    o_w = acc_w * (1.0 / l_w)

    gates = gate_ref[...].T
    outs = []
    for h in range(NSA_HPG):
        hg = grp * NSA_HPG + h
        sl = slice(h * tq, (h + 1) * tq)
        y = (_gate_row(gates, 0, hg) * o_c[:, sl] + _gate_row(gates, 1, hg) * o_s[:, sl] + _gate_row(gates, 2, hg) * o_w[:, sl])
        outs.append(y)
    y = jnp.concatenate(outs, axis=0)
    o_ref[...] = y.T.astype(o_ref.dtype)


def _gate_row(gates, branch, head):
    idx = branch * NSA_HEADS + head
    row = lax.broadcasted_iota(jnp.int32, gates.shape, 0)
    return jnp.sum(jnp.where(row == idx, gates, 0.0), axis=0, keepdims=True)


def _overlap_matrix(s):
    nc = s // CMP_STRIDE
    nsb = s // SEL_LEN
    ci = np.arange(nc)[None, :] * CMP_STRIDE
    sj = np.arange(nsb)[:, None] * SEL_LEN
    ovl = ((ci < sj + SEL_LEN) & (ci + CMP_LEN > sj)).astype(np.float32)
    ovl[:, (s - CMP_LEN) // CMP_STRIDE + 1:] = 0.0
    return jnp.asarray(ovl)


def _nsa_attention(nqt, ng, kcmp, vcmp, ksd, vst, kwd, vwt, b, s):
    tk = tq = KEY_TILE
    nq = s // tq
    gw = NSA_HPG * HEAD_DIM
    kw_ = 2 * HEAD_DIM
    ncp = s // CMP_STRIDE
    nsb = s // SEL_LEN
    seq3 = lambda a: a.reshape(b, s, a.shape[-1])
    vt4 = lambda a: a.reshape(b, s // tk, a.shape[-2], tk)
    k_spec = pl.BlockSpec((None, s, kw_), lambda bi, g, i: (bi, 0, g))
    vt_spec = pl.BlockSpec((None, s // tk, HEAD_DIM, tk), lambda bi, g, i: (bi, 0, g, 0))
    cmp_spec = pl.BlockSpec((None, ncp, kw_), lambda bi, g, i: (bi, 0, g))
    return pl.pallas_call(
        functools.partial(_nsa_kernel, tq=tq, tk=tk),
        grid=(b, NSA_KV_GROUPS, nq),
        in_specs=[
            _const_spec((nsb, ncp)),
            pl.BlockSpec((gw, tq), lambda bi, g, i: (g, bi * nq + i)),
            pl.BlockSpec((None, tq, LANES), lambda bi, g, i: (bi, i, 0)),
            cmp_spec, cmp_spec, k_spec, vt_spec, k_spec, vt_spec,
        ],
        out_specs=pl.BlockSpec((None, tq, gw), lambda bi, g, i: (bi, i, g)),
        out_shape=jax.ShapeDtypeStruct((b, s, NSA_HEADS * HEAD_DIM), BF16),
        scratch_shapes=[pltpu.VMEM((nsb, tq), F32), pltpu.VMEM((nsb, tq), F32)] + _flash_scratch(tk, NSA_HPG * tq, HEAD_DIM),
        compiler_params=_params(3),
        name="nsa_attn",
    )(_overlap_matrix(s), nqt, seq3(ng), kcmp, vcmp, seq3(ksd), vt4(vst), seq3(kwd), vt4(vwt))


def _merge_kernel(x_ref, g_ref, ya_ref, yb_ref, wmg_ref, wa_ref, wb_ref, wo_ref, o_ref):
    x = x_ref[...]
    d = x.shape[1]
    ms = jnp.mean(x * x, axis=-1, keepdims=True)
    u = (x * lax.rsqrt(ms + NORM_EPS) * g_ref[...]).astype(BF16)
    ya = jnp.dot(ya_ref[...], wa_ref[...], preferred_element_type=F32)
    yb = jnp.dot(yb_ref[...], wb_ref[...], preferred_element_type=F32)
    mg_a = jax.nn.sigmoid(jnp.dot(u, wmg_ref[:, :d], preferred_element_type=F32))
    mg_b = jax.nn.sigmoid(jnp.dot(u, wmg_ref[:, d:], preferred_element_type=F32))
    merged = (mg_a * ya + mg_b * yb).astype(BF16)
    o_ref[...] = x + jnp.dot(merged, wo_ref[...], preferred_element_type=F32)


def _merge(x2, g, ya, yb, w_mg, w_a, w_b, w_o):
    t, d = x2.shape
    tm = min(t, 512)
    ws = [w.astype(BF16) for w in (w_mg, w_a, w_b, w_o)]
    return pl.pallas_call(
        _merge_kernel,
        grid=(t // tm,),
        in_specs=[
            pl.BlockSpec((tm, d), lambda i: (i, 0)),
            _const_spec((1, d)),
            pl.BlockSpec((tm, ya.shape[1]), lambda i: (i, 0)),
            pl.BlockSpec((tm, yb.shape[1]), lambda i: (i, 0)),
        ] + [_const_spec(w.shape) for w in ws],
        out_specs=pl.BlockSpec((tm, d), lambda i: (i, 0)),
        out_shape=jax.ShapeDtypeStruct((t, d), F32),
        compiler_params=_params(1),
        name="merge_out_proj",
    )(x2, g[None, :], ya, yb, *ws)


def _ffn_kernel(h_ref, g_ref, gf_ref, wg_ref, wu_ref, wd_ref, o_ref, *, chunk, final_norm):
    h = h_ref[...]
    ms = jnp.mean(h * h, axis=-1, keepdims=True)
    u = (h * lax.rsqrt(ms + NORM_EPS) * g_ref[...]).astype(BF16)
    d_ff = wg_ref.shape[1]
    acc = h
    for c0 in range(0, d_ff, chunk):
        a = jnp.dot(u, wg_ref[:, c0:c0 + chunk], preferred_element_type=F32)
        up = jnp.dot(u, wu_ref[:, c0:c0 + chunk], preferred_element_type=F32)
        act = (a * jax.nn.sigmoid(a) * up).astype(BF16)
        acc = acc + jnp.dot(act, wd_ref[c0:c0 + chunk, :], preferred_element_type=F32)
    if final_norm:
        ms = jnp.mean(acc * acc, axis=-1, keepdims=True)
        acc = acc * lax.rsqrt(ms + NORM_EPS) * gf_ref[...]
    o_ref[...] = acc


def _ffn(h2, g, g_final, w_gate, w_up, w_down, final_norm):
    t, d = h2.shape
    tm = min(t, 512)
    d_ff = w_gate.shape[1]
    chunk = 256
    assert d_ff % chunk == 0
    ws = [w.astype(BF16) for w in (w_gate, w_up, w_down)]
    return pl.pallas_call(
        functools.partial(_ffn_kernel, chunk=chunk, final_norm=final_norm),
        grid=(t // tm,),
        in_specs=[pl.BlockSpec((tm, d), lambda i: (i, 0)), _const_spec((1, d)), _const_spec((1, d))]
        + [_const_spec(w.shape) for w in ws],
        out_specs=pl.BlockSpec((tm, d), lambda i: (i, 0)),
        out_shape=jax.ShapeDtypeStruct((t, d), F32),
        compiler_params=_params(1),
        name="swiglu_ffn",
    )(h2, g[None, :], g_final[None, :], *ws)


def kernel(x, positions, attn_norm_g, w_in, diff_lambda, diff_subln_g, cmp_pe_k, cmp_pe_v, cmp_k_w1, cmp_k_w2, cmp_v_w1, cmp_v_w2, w_branch_a, w_branch_b, w_out, ffn_norm_g, w_gate, w_up, w_down, final_norm_g):
    b, s, d = x.shape
    depth = w_in.shape[0]
    cos, sin = _rope_tables(positions)
    h = x.reshape(b * s, d)
    gates_off = w_in.shape[2] - 2 * d
    for layer in range(depth):
        p = _project(h, attn_norm_g[layer], cos, sin, w_in[layer, :, :gates_off])
        kcmp = _compress(p["kc"], cmp_pe_k[layer], cmp_k_w1[layer], cmp_k_w2[layer], b, s)
        vcmp = _compress(p["vc"], cmp_pe_v[layer], cmp_v_w1[layer], cmp_v_w2[layer], b, s)
        ya = _diff_attention(p["dqT"], p["dk"], p["dvT"], diff_lambda[layer], diff_subln_g[layer], b, s, layer)
        yb = _nsa_attention(p["nqT"], p["ng"], kcmp, vcmp, p["ksd"], p["vsT"], p["kwd"], p["vwT"], b, s)
        h = _merge(h, attn_norm_g[layer], ya.reshape(b * s, -1), yb.reshape(b * s, -1),
                   w_in[layer, :, gates_off:], w_branch_a[layer], w_branch_b[layer], w_out[layer])
        h = _ffn(h, ffn_norm_g[layer], final_norm_g, w_gate[layer], w_up[layer], w_down[layer], layer == depth - 1)
    return h.reshape(b, s, d)
```

```python
import functools
import math

import jax
import jax.numpy as jnp
import numpy as np
from jax import lax
from jax.experimental import pallas as pl
from jax.experimental.pallas import tpu as pltpu

HEAD_DIM = 64
ROPE_THETA = 10000.0
NORM_EPS = 1e-6
NEG_INF = -1e30
FORCE_SCORE = 1e9

DIFF_HEADS = 4
DIFF_VDIM = 2 * HEAD_DIM
NSA_HEADS = 8
NSA_KV_GROUPS = 2
NSA_HPG = NSA_HEADS // NSA_KV_GROUPS
CMP_LEN = 32
CMP_STRIDE = 16
SEL_LEN = 64
SEL_TOPK = 16
WINDOW = 512

LANES = 128
VMEM_LIMIT = 56 * 1024 * 1024
KEY_TILE = 256

BF16 = jnp.bfloat16
F32 = jnp.float32


def _lambda_init(layer):
    return 0.8 - 0.6 * math.exp(-0.3 * layer)


def _params(n_axes):
    return pltpu.CompilerParams(dimension_semantics=("arbitrary",) * n_axes, vmem_limit_bytes=VMEM_LIMIT)


def _const_spec(shape):
    nd = len(shape)
    return pl.BlockSpec(shape, lambda *_: (0,) * nd)


def _rope_table_kernel(pos_ref, invf_ref, cos_ref, sin_ref):
    ang = pos_ref[...].astype(F32) * invf_ref[...]
    cos_ref[...] = jnp.cos(ang)
    sin_ref[...] = jnp.sin(ang)


def _rope_tables(positions):
    half = HEAD_DIM // 2
    per_row = LANES // half
    t = positions.size
    rows = t // per_row
    pos_e = jnp.repeat(positions.reshape(rows, per_row), half, axis=1)
    inv_freq = 1.0 / (ROPE_THETA ** (jnp.arange(0, HEAD_DIM, 2, dtype=F32) / HEAD_DIM))
    invf = jnp.tile(inv_freq, per_row)[None, :]
    tr = min(rows, 1024)
    cos, sin = pl.pallas_call(
        _rope_table_kernel,
        grid=(rows // tr,),
        in_specs=[pl.BlockSpec((tr, LANES), lambda i: (i, 0)), _const_spec((1, LANES))],
        out_specs=[pl.BlockSpec((tr, LANES), lambda i: (i, 0))] * 2,
        out_shape=[jax.ShapeDtypeStruct((rows, LANES), F32)] * 2,
        compiler_params=_params(1),
        name="rope_tables",
    )(pos_e, invf)
    return cos.reshape(t, half), sin.reshape(t, half)


_ROW_SEGS = (("dk", 512, BF16, True), ("ksd", 256, BF16, True), ("kwd", 256, BF16, True), ("kc", 128, F32, True),
             ("vc", 128, F32, False), ("ng", 128, F32, False))
_COL_SEGS = (("dqT", 512, True, False), ("nqT", 512, True, False),
             ("dvT", 512, False, True), ("vsT", 128, False, True), ("vwT", 128, False, True))
_STRIDED_SEGS = ("kc", "vc")


def _proj_kernel(x_ref, g_ref, cos_ref, sin_ref, cost_ref, sint_ref, wr_ref, wt_ref, *out_refs):
    x = x_ref[...]
    ms = jnp.mean(x * x, axis=-1, keepdims=True)
    u = (x * lax.rsqrt(ms + NORM_EPS) * g_ref[...]).astype(BF16)
    half = HEAD_DIM // 2
    reps = LANES // half

    def rope_fn(c, s, axis):
        idx = lax.broadcasted_iota(jnp.int32, c.shape, axis)
        upper = (idx % HEAD_DIM) >= half
        s_up = jnp.where(upper, s, 0.0)
        s_lo = jnp.where(upper, 0.0, -s)
        return lambda y: y * c + pltpu.roll(y, half, axis) * s_up + pltpu.roll(y, LANES - half, axis) * s_lo

    rope_rows = rope_fn(jnp.concatenate([cos_ref[...]] * reps, axis=1), jnp.concatenate([sin_ref[...]] * reps, axis=1), 1)
    rope_cols = rope_fn(jnp.concatenate([cost_ref[...]] * reps, axis=0), jnp.concatenate([sint_ref[...]] * reps, axis=0), 0)

    *refs, stage_ref = out_refs
    tm = x.shape[0]
    off = 0
    for name, width, _, roped in _ROW_SEGS:
        o_ref = refs.pop(0)
        for c0 in range(0, width, 2 * LANES):
            cw = min(2 * LANES, width - c0)
            y = jnp.dot(u, wr_ref[:, off + c0:off + c0 + cw], preferred_element_type=F32)
            for l0 in range(0, cw, LANES):
                yl = y[:, l0:l0 + LANES]
                if roped:
                    yl = rope_rows(yl)
                if name == "ng":
                    yl = jax.nn.sigmoid(yl)
                if name in _STRIDED_SEGS:
                    stage_ref[...] = yl
                    for l in range(CMP_STRIDE):
                        o_ref[:, l * LANES:(l + 1) * LANES] = stage_ref[pl.ds(l, tm // CMP_STRIDE, stride=CMP_STRIDE), :]
                else:
                    o_ref[:, c0 + l0:c0 + l0 + LANES] = yl.astype(o_ref.dtype)
        off += width

    off = 0
    for name, rows, roped, tiled in _COL_SEGS:
        o_ref = refs.pop(0)
        for r0 in range(0, rows, 2 * LANES):
            rw = min(2 * LANES, rows - r0)
            yt = lax.dot_general(wt_ref[off + r0:off + r0 + rw, :], u, (((1,), (1,)), ((), ())), preferred_element_type=F32)
            for s0 in range(0, rw, LANES):
                ys = yt[s0:s0 + LANES]
                if roped:
                    ys = rope_cols(ys)
                ys = ys.astype(o_ref.dtype)
                if tiled:
                    for j in range(tm // KEY_TILE):
                        o_ref[j, r0 + s0:r0 + s0 + LANES, :] = ys[:, j * KEY_TILE:(j + 1) * KEY_TILE]
                else:
                    o_ref[r0 + s0:r0 + s0 + LANES, :] = ys
        off += rows


def _dup_groups(w):
    d = w.shape[0]
    wg = w.reshape(d, NSA_KV_GROUPS, 1, HEAD_DIM)
    return jnp.broadcast_to(wg, (d, NSA_KV_GROUPS, 2, HEAD_DIM)).reshape(d, NSA_KV_GROUPS * 2 * HEAD_DIM)


def _project(x2, g, cos, sin, w_in):
    t, d = x2.shape
    qk = DIFF_HEADS * 2 * HEAD_DIM
    kv = NSA_KV_GROUPS * HEAD_DIM
    splits = (qk, qk, DIFF_HEADS * DIFF_VDIM, NSA_HEADS * HEAD_DIM, kv, kv, kv, kv, kv, kv, 3 * NSA_HEADS)
    offs = np.cumsum((0,) + splits)
    dq, dk, dv, nq, kc, vc, ks, vs, kw, vw, ng = (w_in[:, offs[i]:offs[i + 1]] for i in range(len(splits)))
    scale = HEAD_DIM ** -0.5 * math.log2(math.e)
    ng = jnp.pad(ng, ((0, 0), (0, LANES - ng.shape[1])))
    wr = jnp.concatenate([dk, _dup_groups(ks), _dup_groups(kw), kc, vc, ng], axis=1).astype(BF16)
    wt = jnp.concatenate([dq * scale, nq * scale, dv, vs, vw], axis=1).T.astype(BF16)
    tm = min(t, 512)
    assert tm % KEY_TILE == 0
    half = HEAD_DIM // 2
    out_specs, out_shape = [], []
    for name, w, dt, _ in _ROW_SEGS:
        fold = CMP_STRIDE if name in _STRIDED_SEGS else 1
        out_specs.append(pl.BlockSpec((tm // fold, w * fold), lambda i: (i, 0)))
        out_shape.append(jax.ShapeDtypeStruct((t // fold, w * fold), dt))
    for _, rows, _, tiled in _COL_SEGS:
        if tiled:
            out_specs.append(pl.BlockSpec((tm // KEY_TILE, rows, KEY_TILE), lambda i: (i, 0, 0)))
            out_shape.append(jax.ShapeDtypeStruct((t // KEY_TILE, rows, KEY_TILE), BF16))
        else:
            out_specs.append(pl.BlockSpec((rows, tm), lambda i: (0, i)))
            out_shape.append(jax.ShapeDtypeStruct((rows, t), BF16))
    outs = pl.pallas_call(
        _proj_kernel,
        grid=(t // tm,),
        in_specs=[
            pl.BlockSpec((tm, d), lambda i: (i, 0)),
            _const_spec((1, d)),
            pl.BlockSpec((tm, half), lambda i: (i, 0)),
            pl.BlockSpec((tm, half), lambda i: (i, 0)),
            pl.BlockSpec((half, tm), lambda i: (0, i)),
            pl.BlockSpec((half, tm), lambda i: (0, i)),
            _const_spec(wr.shape),
            _const_spec(wt.shape),
        ],
        out_specs=out_specs,
        out_shape=out_shape,
        scratch_shapes=[pltpu.VMEM((tm, LANES), F32)],
        compiler_params=_params(1),
        name="in_proj",
    )(x2, g[None, :], cos, sin, cos.T, sin.T, wr, wt)
    return dict(zip([n for n, *_ in _ROW_SEGS] + [n for n, *_ in _COL_SEGS], outs))


def _compress_kernel(x_ref, pea_ref, peb_ref, w1a_ref, w1b_ref, w2_ref, o_ref):
    x = x_ref[...]
    a = jnp.dot((x + pea_ref[...]).astype(BF16), w1a_ref[...], preferred_element_type=F32)
    b = jnp.dot((x + peb_ref[...]).astype(BF16), w1b_ref[...], preferred_element_type=F32)
    ncp = x.shape[0]
    h = a + pltpu.roll(b, ncp - 1, 0)
    h = h * jax.nn.sigmoid(h)
    o_ref[...] = jnp.dot(h.astype(BF16), w2_ref[...], preferred_element_type=F32).astype(o_ref.dtype)


def _compress(xc, pe, w1, w2, b, s):
    g_, dh = NSA_KV_GROUPS, HEAD_DIM
    ncp = s // CMP_STRIDE
    hid = w1.shape[1]
    halves = CMP_LEN // CMP_STRIDE
    assert halves == 2
    xr = xc.reshape(b, ncp, CMP_STRIDE * g_ * dh)
    eye = jnp.eye(g_, dtype=F32)
    pe_r = pe.reshape(halves, CMP_STRIDE, 1, dh)
    pe_t = jnp.broadcast_to(pe_r, (halves, CMP_STRIDE, g_, dh)).reshape(halves, 1, CMP_STRIDE * g_ * dh)
    w1r = w1.reshape(halves, CMP_STRIDE, dh, hid)
    w1bd = jnp.einsum("hldj,pg->hlpdgj", w1r, eye).reshape(halves, CMP_STRIDE * g_ * dh, g_ * hid).astype(BF16)
    w2bd = jnp.einsum("jd,pg,r->pjgrd", w2, eye, jnp.ones((2,), F32)).reshape(g_ * hid, g_ * 2 * dh).astype(BF16)
    kdim = CMP_STRIDE * g_ * dh
    return pl.pallas_call(
        _compress_kernel,
        grid=(b,),
        in_specs=[
            pl.BlockSpec((None, ncp, kdim), lambda i: (i, 0, 0)),
            _const_spec((1, kdim)),
            _const_spec((1, kdim)),
            _const_spec((kdim, g_ * hid)),
            _const_spec((kdim, g_ * hid)),
            _const_spec((g_ * hid, g_ * 2 * dh)),
        ],
        out_specs=pl.BlockSpec((None, ncp, g_ * 2 * dh), lambda i: (i, 0, 0)),
        out_shape=jax.ShapeDtypeStruct((b, ncp, g_ * 2 * dh), BF16),
        compiler_params=_params(1),
        name="compress",
    )(xr, pe_t[0], pe_t[1], w1bd[0], w1bd[1], w2bd)


def _split_heads(qt):
    row = lax.broadcasted_iota(jnp.int32, qt.shape, 0)
    zero = jnp.zeros_like(qt)
    return jnp.concatenate([jnp.where(row < HEAD_DIM, qt, zero), jnp.where(row >= HEAD_DIM, qt, zero)], axis=1)


def _flash_scratch(tk, m_cols, dv):
    row = pltpu.VMEM((1, m_cols), F32)
    sc = pltpu.VMEM((tk, m_cols), F32)
    pr = pltpu.VMEM((tk, m_cols), BF16)
    return [sc, sc, row, row, pr, pr, row, row, row, row, pltpu.VMEM((dv, m_cols), F32)]


def _flash_tiles(qst, k_ref, vt_ref, scratch, tk, lo, hi, mask_fn, inner_fn):
    sa_ref, sb_ref, ta_ref, tb_ref, pa_ref, pb_ref, ala_ref, alb_ref, m_ref, l_ref, acc_ref = scratch
    slots = {"a": (sa_ref, ta_ref, pa_ref, ala_ref), "b": (sb_ref, tb_ref, pb_ref, alb_ref)}
    n_k = k_ref.shape[0] // tk

    def clamp(t):
        return jnp.clip(t, 0, n_k - 1)

    def scores(t, slot, fn):
        s_ref, t_ref, _, _ = slots[slot]
        k = k_ref[pl.ds(pl.multiple_of(clamp(t) * tk, tk), tk), :]
        s = fn(t, jnp.dot(k, qst, preferred_element_type=F32))
        t_ref[...] = jnp.max(s, axis=0, keepdims=True)
        s_ref[...] = s

    def probs(slot):
        s_ref, t_ref, p_ref, al_ref = slots[slot]
        m_old = m_ref[...]
        m_new = jnp.maximum(m_old, t_ref[...])
        alpha = jnp.exp2(m_old - m_new)
        p = jnp.exp2(s_ref[...] - m_new)
        l_ref[...] = alpha * l_ref[...] + jnp.sum(p, axis=0, keepdims=True)
        p_ref[...] = p.astype(BF16)
        al_ref[...] = alpha
        m_ref[...] = m_new

    def values(t, slot):
        _, _, p_ref, al_ref = slots[slot]
        pv = jnp.dot(vt_ref[clamp(t)], p_ref[...], preferred_element_type=F32)
        acc_ref[...] = al_ref[...] * acc_ref[...] + pv

    n_pairs = (hi - lo) // 2 + 1

    def first_tile(r):
        return lo + 2 * jnp.where(r == 0, n_pairs - 1, r - 1)

    m_ref[...] = jnp.full(m_ref.shape, NEG_INF, F32)
    l_ref[...] = jnp.zeros(l_ref.shape, F32)
    acc_ref[...] = jnp.zeros(acc_ref.shape, F32)
    for slot in ("a", "b"):
        slots[slot][2][...] = jnp.zeros(pa_ref.shape, BF16)
        slots[slot][3][...] = jnp.ones(ala_ref.shape, F32)

    scores(first_tile(0), "a", mask_fn)
    scores(first_tile(0) + 1, "b", mask_fn)

    def body(r, carry):
        u = first_tile(r - 1)
        values(u, "a")
        values(u + 1, "b")
        probs("a")
        probs("b")
        scores(lo + 2 * r, "a", inner_fn)
        scores(lo + 2 * r + 1, "b", inner_fn)
        return carry

    lax.fori_loop(0, n_pairs - 1, body, 0)
    u = first_tile(n_pairs - 2)
    values(u, "a")
    values(u + 1, "b")
    probs("a")
    probs("b")
    u = first_tile(n_pairs - 1)
    values(u, "a")
    values(u + 1, "b")
    return l_ref[...], acc_ref[...]


def _diff_kernel(lam_ref, g_ref, qt_ref, k_ref, vt_ref, o_ref, *scratch, tq, tk, lam_init):
    qi = pl.program_id(2)
    qst = _split_heads(qt_ref[...])
    m_cols = 2 * tq
    qpos = qi * tq + lax.broadcasted_iota(jnp.int32, (1, m_cols), 1) % tq
    krel = lax.broadcasted_iota(jnp.int32, (tk, m_cols), 0)

    def causal(t, s):
        return jnp.where(krel <= qpos - t * tk, s, NEG_INF)

    l, acc = _flash_tiles(qst, k_ref, vt_ref, scratch, tk, 0, (qi * tq + tq - 1) // tk, causal, lambda t, s: s)
    o = acc * (1.0 / l)
    lf = lam_ref[...]
    lam = (jnp.exp(jnp.sum(lf[0:1] * lf[1:2], axis=1, keepdims=True))
           - jnp.exp(jnp.sum(lf[2:3] * lf[3:4], axis=1, keepdims=True)) + lam_init)
    y = o[:, :tq] - lam * o[:, tq:]
    y = y * lax.rsqrt(jnp.mean(y * y, axis=0, keepdims=True) + NORM_EPS)
    y = y.T * g_ref[...] * (1.0 - lam_init)
    o_ref[...] = y.astype(o_ref.dtype)


def _diff_attention(dqt, dk, dvt, diff_lambda, subln_g, b, s, layer):
    tk = KEY_TILE
    tq = 2 * tk
    hw = 2 * HEAD_DIM
    width = DIFF_HEADS * hw
    nq = s // tq
    return pl.pallas_call(
        functools.partial(_diff_kernel, tq=tq, tk=tk, lam_init=_lambda_init(layer)),
        grid=(b, DIFF_HEADS, nq),
        in_specs=[
            _const_spec(diff_lambda.shape),
            _const_spec((1, DIFF_VDIM)),
            pl.BlockSpec((hw, tq), lambda bi, h, i: (h, bi * nq + i)),
            pl.BlockSpec((None, s, hw), lambda bi, h, i: (bi, 0, h)),
            pl.BlockSpec((None, s // tk, DIFF_VDIM, tk), lambda bi, h, i: (bi, 0, h, 0)),
        ],
        out_specs=pl.BlockSpec((None, tq, DIFF_VDIM), lambda bi, h, i: (bi, i, h)),
        out_shape=jax.ShapeDtypeStruct((b, s, DIFF_HEADS * DIFF_VDIM), BF16),
        scratch_shapes=_flash_scratch(tk, 2 * tq, DIFF_VDIM),
        compiler_params=_params(3),
        name="diff_attn",
    )(diff_lambda, subln_g[None, :], dqt, dk.reshape(b, s, width), dvt.reshape(b, s // tk, width, tk))


def _nsa_kernel(ovl_ref, qt_ref, gate_ref, kc_ref, vc_ref, ks_ref, vst_ref, kw_ref, vwt_ref, o_ref, imp_ref, selb_ref, cnt_ref, *scratch, tq, tk):
    qi = pl.program_id(2)
    grp = pl.program_id(1)
    q0 = qi * tq
    m_cols = NSA_HPG * tq
    qt = qt_ref[...]
    qst = jnp.concatenate([_split_heads(qt[:LANES]), _split_heads(qt[LANES:])], axis=1)
    qpos_t = q0 + lax.broadcasted_iota(jnp.int32, (1, tq), 1)
    qpos = jnp.concatenate([qpos_t] * NSA_HPG, axis=1)
    dh = HEAD_DIM

    ncp = kc_ref.shape[0]
    last_cmp = lax.shift_right_arithmetic(qpos - (CMP_LEN - 1), int(math.log2(CMP_STRIDE)))
    cmask = lax.broadcasted_iota(jnp.int32, (ncp, m_cols), 0) <= last_cmp
    sc = jnp.dot(kc_ref[...], qst, preferred_element_type=F32)
    sc = jnp.where(cmask, sc, NEG_INF)
    pc = jnp.exp2(sc - jnp.max(sc, axis=0, keepdims=True))
    pc = pc * jnp.where(qpos >= CMP_LEN - 1, 1.0 / jnp.sum(pc, axis=0, keepdims=True), 0.0)
    o_c = lax.dot_general(vc_ref[...], pc.astype(BF16), (((0,), (0,)), ((), ())), preferred_element_type=F32)[:dh]

    pc_sum = pc[:, 0:tq]
    for h in range(1, NSA_HPG):
        pc_sum = pc_sum + pc[:, h * tq:(h + 1) * tq]
    imp = jnp.dot(ovl_ref[...], pc_sum, preferred_element_type=F32, precision=lax.Precision.HIGHEST)
    nsb = imp.shape[0]
    blk = lax.broadcasted_iota(jnp.int32, (nsb, tq), 0)
    qblk = qpos_t // SEL_LEN
    forced = (blk == 0) | (blk == qblk) | (blk == qblk - 1)
    imp = jnp.where(forced, FORCE_SCORE, imp)
    imp = jnp.where(blk <= qblk, imp, NEG_INF)
    imp_ref[...] = imp
    topk = min(SEL_TOPK, nsb)

    sub = 8
    cnt_ref[...] = jnp.zeros(cnt_ref.shape, jnp.int32)
    last_blk = (q0 + tq - 1) // SEL_LEN
    for g0 in range(0, nsb, sub):
        @pl.when(g0 <= last_blk)
        def _(g0=g0):
            cnts = [cnt_ref[b0:b0 + sub, :] for b0 in range(0, nsb, sub)]
            for j in range(g0, g0 + sub):
                row = jnp.broadcast_to(imp_ref[j:j + 1, :], (sub, tq))
                for i, b0 in enumerate(range(0, nsb, sub)):
                    cur = imp_ref[b0:b0 + sub, :]
                    if b0 > j:
                        beats = row >= cur
                    elif b0 + sub - 1 < j:
                        beats = row > cur
                    else:
                        beats = (row > cur) | ((row == cur) & (lax.broadcasted_iota(jnp.int32, (sub, tq), 0) > j - b0))
                    cnts[i] = cnts[i] + jnp.where(beats, 1, 0)
            for i, b0 in enumerate(range(0, nsb, sub)):
                cnt_ref[b0:b0 + sub, :] = cnts[i]
    selb_ref[...] = jnp.where(cnt_ref[...] < topk, 0.0, NEG_INF)

    krel = lax.broadcasted_iota(jnp.int32, (tk, m_cols), 0)
    bpt = tk // SEL_LEN
    last = (q0 + tq - 1) // tk

    def biased(t, s):
        first_blk = jnp.clip(t, 0, nsb // bpt - 1) * bpt
        rows = []
        for r in range(bpt):
            brow = selb_ref[pl.ds(first_blk + r, 1), :]
            brow = jnp.concatenate([brow] * NSA_HPG, axis=1)
            rows.append(jnp.broadcast_to(brow, (SEL_LEN, m_cols)))
        return s + jnp.concatenate(rows, axis=0)

    def selected(t, s):
        return jnp.where(krel <= qpos - t * tk, biased(t, s), NEG_INF)

    l_s, acc_s = _flash_tiles(qst, ks_ref, vst_ref, scratch, tk, 0, last, selected, biased)
    o_s = acc_s * (1.0 / l_s)

    n_back = -(-(WINDOW - 1) // tk)
    win_tiles = [qi - n_back + r for r in range(n_back + 1)]
    s_w = []
    for r, t in enumerate(win_tiles):
        k = kw_ref[pl.ds(pl.multiple_of(jnp.maximum(t, 0) * tk, tk), tk), :]
        s = jnp.dot(k, qst, preferred_element_type=F32)
        d_min = (n_back - r) * tk - (tk - 1)
        d_max = (n_back - r) * tk + (tq - 1)
        if d_min < 0:
            s = jnp.where(krel <= qpos - t * tk, s, NEG_INF)
        if d_max >= WINDOW:
            s = jnp.where(krel > qpos - t * tk - WINDOW, s, NEG_INF)
        if r < n_back:
            s = s + jnp.where(t >= 0, 0.0, NEG_INF)
        s_w.append(s)
    m_w = functools.reduce(jnp.maximum, [jnp.max(s, axis=0, keepdims=True) for s in s_w])
    l_w = jnp.zeros((1, m_cols), F32)
    acc_w = jnp.zeros((dh, m_cols), F32)
    for t, s in zip(win_tiles, s_w):
        p = jnp.exp2(s - m_w)
        l_w = l_w + jnp.sum(p, axis=0, keepdims=True)
        acc_w = acc_w + jnp.dot(vwt_ref[jnp.maximum(t, 0)], p.astype(BF16), preferred_element_type=F32)
    o_w = acc_w * (1.0 / l_w)

    gates = gate_ref[...].T
    outs = []
    for h in range(NSA_HPG):
        hg = grp * NSA_HPG + h
        sl = slice(h * tq, (h + 1) * tq)
        y = (_gate_row(gates, 0, hg) * o_c[:, sl] + _gate_row(gates, 1, hg) * o_s[:, sl] + _gate_row(gates, 2, hg) * o_w[:, sl])
        outs.append(y)
    y = jnp.concatenate(outs, axis=0)
    o_ref[...] = y.T.astype(o_ref.dtype)


def _gate_row(gates, branch, head):
    idx = branch * NSA_HEADS + head
    row = lax.broadcasted_iota(jnp.int32, gates.shape, 0)
    return jnp.sum(jnp.where(row == idx, gates, 0.0), axis=0, keepdims=True)


def _overlap_matrix(s):
    nc = s // CMP_STRIDE
    nsb = s // SEL_LEN
    ci = np.arange(nc)[None, :] * CMP_STRIDE
    sj = np.arange(nsb)[:, None] * SEL_LEN
    ovl = ((ci < sj + SEL_LEN) & (ci + CMP_LEN > sj)).astype(np.float32)
    ovl[:, (s - CMP_LEN) // CMP_STRIDE + 1:] = 0.0
    return jnp.asarray(ovl)


def _nsa_attention(nqt, ng, kcmp, vcmp, ksd, vst, kwd, vwt, b, s):
    tk = tq = KEY_TILE
    nq = s // tq
    gw = NSA_HPG * HEAD_DIM
    kw_ = 2 * HEAD_DIM
    ncp = s // CMP_STRIDE
    nsb = s // SEL_LEN
    seq3 = lambda a: a.reshape(b, s, a.shape[-1])
    vt4 = lambda a: a.reshape(b, s // tk, a.shape[-2], tk)
    k_spec = pl.BlockSpec((None, s, kw_), lambda bi, g, i: (bi, 0, g))
    vt_spec = pl.BlockSpec((None, s // tk, HEAD_DIM, tk), lambda bi, g, i: (bi, 0, g, 0))
    cmp_spec = pl.BlockSpec((None, ncp, kw_), lambda bi, g, i: (bi, 0, g))
    return pl.pallas_call(
        functools.partial(_nsa_kernel, tq=tq, tk=tk),
        grid=(b, NSA_KV_GROUPS, nq),
        in_specs=[
            _const_spec((nsb, ncp)),
            pl.BlockSpec((gw, tq), lambda bi, g, i: (g, bi * nq + i)),
            pl.BlockSpec((None, tq, LANES), lambda bi, g, i: (bi, i, 0)),
            cmp_spec, cmp_spec, k_spec, vt_spec, k_spec, vt_spec,
        ],
        out_specs=pl.BlockSpec((None, tq, gw), lambda bi, g, i: (bi, i, g)),
        out_shape=jax.ShapeDtypeStruct((b, s, NSA_HEADS * HEAD_DIM), BF16),
        scratch_shapes=[pltpu.VMEM((nsb, tq), F32), pltpu.VMEM((nsb, tq), F32), pltpu.VMEM((nsb, tq), jnp.int32)] + _flash_scratch(tk, NSA_HPG * tq, HEAD_DIM),
        compiler_params=_params(3),
        name="nsa_attn",
    )(_overlap_matrix(s), nqt, seq3(ng), kcmp, vcmp, seq3(ksd), vt4(vst), seq3(kwd), vt4(vwt))


def _merge_kernel(x_ref, g_ref, ya_ref, yb_ref, wmg_ref, wa_ref, wb_ref, wo_ref, o_ref):
    x = x_ref[...]
    d = x.shape[1]
    ms = jnp.mean(x * x, axis=-1, keepdims=True)
    u = (x * lax.rsqrt(ms + NORM_EPS) * g_ref[...]).astype(BF16)
    ya = jnp.dot(ya_ref[...], wa_ref[...], preferred_element_type=F32)
    yb = jnp.dot(yb_ref[...], wb_ref[...], preferred_element_type=F32)
    mg_a = jax.nn.sigmoid(jnp.dot(u, wmg_ref[:, :d], preferred_element_type=F32))
    mg_b = jax.nn.sigmoid(jnp.dot(u, wmg_ref[:, d:], preferred_element_type=F32))
    merged = (mg_a * ya + mg_b * yb).astype(BF16)
    o_ref[...] = x + jnp.dot(merged, wo_ref[...], preferred_element_type=F32)


def _merge(x2, g, ya, yb, w_mg, w_a, w_b, w_o):
    t, d = x2.shape
    tm = min(t, 512)
    ws = [w.astype(BF16) for w in (w_mg, w_a, w_b, w_o)]
    return pl.pallas_call(
        _merge_kernel,
        grid=(t // tm,),
        in_specs=[
            pl.BlockSpec((tm, d), lambda i: (i, 0)),
            _const_spec((1, d)),
            pl.BlockSpec((tm, ya.shape[1]), lambda i: (i, 0)),
            pl.BlockSpec((tm, yb.shape[1]), lambda i: (i, 0)),
        ] + [_const_spec(w.shape) for w in ws],
        out_specs=pl.BlockSpec((tm, d), lambda i: (i, 0)),
        out_shape=jax.ShapeDtypeStruct((t, d), F32),
        compiler_params=_params(1),
        name="merge_out_proj",
    )(x2, g[None, :], ya, yb, *ws)


def _ffn_kernel(h_ref, g_ref, gf_ref, wg_ref, wu_ref, wd_ref, o_ref, *, chunk, final_norm):
    h = h_ref[...]
    ms = jnp.mean(h * h, axis=-1, keepdims=True)
    u = (h * lax.rsqrt(ms + NORM_EPS) * g_ref[...]).astype(BF16)
    d_ff = wg_ref.shape[1]
    acc = h
    for c0 in range(0, d_ff, chunk):
        a = jnp.dot(u, wg_ref[:, c0:c0 + chunk], preferred_element_type=F32)
        up = jnp.dot(u, wu_ref[:, c0:c0 + chunk], preferred_element_type=F32)
        act = (a * jax.nn.sigmoid(a) * up).astype(BF16)
        acc = acc + jnp.dot(act, wd_ref[c0:c0 + chunk, :], preferred_element_type=F32)
    if final_norm:
        ms = jnp.mean(acc * acc, axis=-1, keepdims=True)
        acc = acc * lax.rsqrt(ms + NORM_EPS) * gf_ref[...]
    o_ref[...] = acc


def _ffn(h2, g, g_final, w_gate, w_up, w_down, final_norm):
    t, d = h2.shape
    tm = min(t, 512)
    d_ff = w_gate.shape[1]
    chunk = 256
    assert d_ff % chunk == 0
    ws = [w.astype(BF16) for w in (w_gate, w_up, w_down)]
    return pl.pallas_call(
        functools.partial(_ffn_kernel, chunk=chunk, final_norm=final_norm),
        grid=(t // tm,),
        in_specs=[pl.BlockSpec((tm, d), lambda i: (i, 0)), _const_spec((1, d)), _const_spec((1, d))]
        + [_const_spec(w.shape) for w in ws],
        out_specs=pl.BlockSpec((tm, d), lambda i: (i, 0)),
        out_shape=jax.ShapeDtypeStruct((t, d), F32),
        compiler_params=_params(1),
        name="swiglu_ffn",
    )(h2, g[None, :], g_final[None, :], *ws)


def kernel(x, positions, attn_norm_g, w_in, diff_lambda, diff_subln_g, cmp_pe_k, cmp_pe_v, cmp_k_w1, cmp_k_w2, cmp_v_w1, cmp_v_w2, w_branch_a, w_branch_b, w_out, ffn_norm_g, w_gate, w_up, w_down, final_norm_g):
    b, s, d = x.shape
    depth = w_in.shape[0]
    cos, sin = _rope_tables(positions)
    h = x.reshape(b * s, d)
    gates_off = w_in.shape[2] - 2 * d
    for layer in range(depth):
        p = _project(h, attn_norm_g[layer], cos, sin, w_in[layer, :, :gates_off])
        kcmp = _compress(p["kc"], cmp_pe_k[layer], cmp_k_w1[layer], cmp_k_w2[layer], b, s)
        vcmp = _compress(p["vc"], cmp_pe_v[layer], cmp_v_w1[layer], cmp_v_w2[layer], b, s)
        ya = _diff_attention(p["dqT"], p["dk"], p["dvT"], diff_lambda[layer], diff_subln_g[layer], b, s, layer)
        yb = _nsa_attention(p["nqT"], p["ng"], kcmp, vcmp, p["ksd"], p["vsT"], p["kwd"], p["vwT"], b, s)
        h = _merge(h, attn_norm_g[layer], ya.reshape(b * s, -1), yb.reshape(b * s, -1),
                   w_in[layer, :, gates_off:], w_branch_a[layer], w_branch_b[layer], w_out[layer])
        h = _ffn(h, ffn_norm_g[layer], final_norm_g, w_gate[layer], w_up[layer], w_down[layer], layer == depth - 1)
    return h.reshape(b, s, d)
```

```python
import functools
import math

import jax
import jax.numpy as jnp
import numpy as np
from jax import lax
from jax.experimental import pallas as pl
from jax.experimental.pallas import tpu as pltpu

HEAD_DIM = 64
ROPE_THETA = 10000.0
NORM_EPS = 1e-6
NEG_INF = -1e30
FORCE_SCORE = 1e9

DIFF_HEADS = 4
DIFF_VDIM = 2 * HEAD_DIM
NSA_HEADS = 8
NSA_KV_GROUPS = 2
NSA_HPG = NSA_HEADS // NSA_KV_GROUPS
CMP_LEN = 32
CMP_STRIDE = 16
SEL_LEN = 64
SEL_TOPK = 16
WINDOW = 512

LANES = 128
VMEM_LIMIT = 56 * 1024 * 1024
KEY_TILE = 256

BF16 = jnp.bfloat16
F32 = jnp.float32


def _lambda_init(layer):
    return 0.8 - 0.6 * math.exp(-0.3 * layer)


def _params(n_axes):
    return pltpu.CompilerParams(dimension_semantics=("arbitrary",) * n_axes, vmem_limit_bytes=VMEM_LIMIT)


def _const_spec(shape):
    nd = len(shape)
    return pl.BlockSpec(shape, lambda *_: (0,) * nd)


def _rope_table_kernel(pos_ref, invf_ref, cos_ref, sin_ref):
    ang = pos_ref[...].astype(F32) * invf_ref[...]
    cos_ref[...] = jnp.cos(ang)
    sin_ref[...] = jnp.sin(ang)


def _rope_tables(positions):
    half = HEAD_DIM // 2
    per_row = LANES // half
    t = positions.size
    rows = t // per_row
    pos_e = jnp.repeat(positions.reshape(rows, per_row), half, axis=1)
    inv_freq = 1.0 / (ROPE_THETA ** (jnp.arange(0, HEAD_DIM, 2, dtype=F32) / HEAD_DIM))
    invf = jnp.tile(inv_freq, per_row)[None, :]
    tr = min(rows, 1024)
    cos, sin = pl.pallas_call(
        _rope_table_kernel,
        grid=(rows // tr,),
        in_specs=[pl.BlockSpec((tr, LANES), lambda i: (i, 0)), _const_spec((1, LANES))],
        out_specs=[pl.BlockSpec((tr, LANES), lambda i: (i, 0))] * 2,
        out_shape=[jax.ShapeDtypeStruct((rows, LANES), F32)] * 2,
        compiler_params=_params(1),
        name="rope_tables",
    )(pos_e, invf)
    return cos.reshape(t, half), sin.reshape(t, half)


_ROW_SEGS = (("dk", 512, BF16, True), ("ksd", 256, BF16, True), ("kwd", 256, BF16, True), ("kc", 128, F32, True),
             ("vc", 128, F32, False), ("ng", 128, F32, False))
KEY_PAIR = 2 * KEY_TILE
ONES_ROWS = 16
_COL_SEGS = (("dqT", 512, True, None, None), ("nqT", 512, True, None, None),
             ("dvT", 512, False, KEY_PAIR, DIFF_VDIM), ("vsT", 128, False, KEY_PAIR, HEAD_DIM), ("vwT", 128, False, KEY_TILE, HEAD_DIM))
_STRIDED_SEGS = ("kc", "vc")


def _proj_kernel(x_ref, g_ref, cos_ref, sin_ref, wr_ref, wt_ref, *out_refs):
    x = x_ref[...]
    ms = jnp.mean(x * x, axis=-1, keepdims=True)
    u = (x * lax.rsqrt(ms + NORM_EPS) * g_ref[...]).astype(BF16)
    half = HEAD_DIM // 2
    reps = LANES // half

    def rope_fn(c, s, axis):
        idx = lax.broadcasted_iota(jnp.int32, c.shape, axis)
        upper = (idx % HEAD_DIM) >= half
        s_up = jnp.where(upper, s, 0.0)
        s_lo = jnp.where(upper, 0.0, -s)
        return lambda y: y * c + pltpu.roll(y, half, axis) * s_up + pltpu.roll(y, LANES - half, axis) * s_lo

    c = jnp.concatenate([cos_ref[...]] * reps, axis=1)
    s = jnp.concatenate([sin_ref[...]] * reps, axis=1)
    rope_rows = rope_fn(c, s, 1)
    rope_cols = rope_fn(c.T, s.T, 0)

    *refs, stage_ref = out_refs
    tm = x.shape[0]
    off = 0
    for name, width, _, roped in _ROW_SEGS:
        o_ref = refs.pop(0)
        for c0 in range(0, width, 2 * LANES):
            cw = min(2 * LANES, width - c0)
            y = jnp.dot(u, wr_ref[:, off + c0:off + c0 + cw], preferred_element_type=F32)
            for l0 in range(0, cw, LANES):
                yl = y[:, l0:l0 + LANES]
                if roped:
                    yl = rope_rows(yl)
                if name == "ng":
                    yl = jax.nn.sigmoid(yl)
                if name in _STRIDED_SEGS:
                    stage_ref[...] = yl
                    for l in range(CMP_STRIDE):
                        o_ref[:, l * LANES:(l + 1) * LANES] = stage_ref[pl.ds(l, tm // CMP_STRIDE, stride=CMP_STRIDE), :]
                else:
                    o_ref[:, c0 + l0:c0 + l0 + LANES] = yl.astype(o_ref.dtype)
        off += width

    off = 0
    for name, rows, roped, tile, group in _COL_SEGS:
        o_ref = refs.pop(0)
        for r0 in range(0, rows, 2 * LANES):
            rw = min(2 * LANES, rows - r0)
            yt = lax.dot_general(wt_ref[off + r0:off + r0 + rw, :], u, (((1,), (1,)), ((), ())), preferred_element_type=F32)
            for s0 in range(0, rw, LANES):
                ys = yt[s0:s0 + LANES]
                if roped:
                    ys = rope_cols(ys)
                ys = ys.astype(o_ref.dtype)
                if not tile:
                    o_ref[r0 + s0:r0 + s0 + LANES, :] = ys
                    continue
                for g0 in range(0, LANES, group):
                    dst = (r0 + s0 + g0) // group * (group + ONES_ROWS)
                    for j in range(tm // tile):
                        o_ref[j, dst:dst + group, :] = ys[g0:g0 + group, j * tile:(j + 1) * tile]
                        o_ref[j, dst + group:dst + group + ONES_ROWS, :] = jnp.ones((ONES_ROWS, tile), o_ref.dtype)
        off += rows


def _dup_groups(w):
    d = w.shape[0]
    wg = w.reshape(d, NSA_KV_GROUPS, 1, HEAD_DIM)
    return jnp.broadcast_to(wg, (d, NSA_KV_GROUPS, 2, HEAD_DIM)).reshape(d, NSA_KV_GROUPS * 2 * HEAD_DIM)


def _project(x2, g, cos, sin, w_in):
    t, d = x2.shape
    qk = DIFF_HEADS * 2 * HEAD_DIM
    kv = NSA_KV_GROUPS * HEAD_DIM
    splits = (qk, qk, DIFF_HEADS * DIFF_VDIM, NSA_HEADS * HEAD_DIM, kv, kv, kv, kv, kv, kv, 3 * NSA_HEADS)
    offs = np.cumsum((0,) + splits)
    dq, dk, dv, nq, kc, vc, ks, vs, kw, vw, ng = (w_in[:, offs[i]:offs[i + 1]] for i in range(len(splits)))
    scale = HEAD_DIM ** -0.5 * math.log2(math.e)
    ng = jnp.pad(ng, ((0, 0), (0, LANES - ng.shape[1])))
    wr = jnp.concatenate([dk, _dup_groups(ks), _dup_groups(kw), kc, vc, ng], axis=1).astype(BF16)
    wt = jnp.concatenate([dq * scale, nq * scale, dv, vs, vw], axis=1).T.astype(BF16)
    tm = min(t, 512)
    assert tm % KEY_TILE == 0
    half = HEAD_DIM // 2
    out_specs, out_shape = [], []
    for name, w, dt, _ in _ROW_SEGS:
        fold = CMP_STRIDE if name in _STRIDED_SEGS else 1
        out_specs.append(pl.BlockSpec((tm // fold, w * fold), lambda i: (i, 0)))
        out_shape.append(jax.ShapeDtypeStruct((t // fold, w * fold), dt))
    for _, rows, _, tile, group in _COL_SEGS:
        if tile:
            assert tm % tile == 0
            padded = rows // group * (group + ONES_ROWS)
            out_specs.append(pl.BlockSpec((tm // tile, padded, tile), lambda i: (i, 0, 0)))
            out_shape.append(jax.ShapeDtypeStruct((t // tile, padded, tile), BF16))
        else:
            out_specs.append(pl.BlockSpec((rows, tm), lambda i: (0, i)))
            out_shape.append(jax.ShapeDtypeStruct((rows, t), BF16))
    outs = pl.pallas_call(
        _proj_kernel,
        grid=(t // tm,),
        in_specs=[
            pl.BlockSpec((tm, d), lambda i: (i, 0)),
            _const_spec((1, d)),
            pl.BlockSpec((tm, half), lambda i: (i, 0)),
            pl.BlockSpec((tm, half), lambda i: (i, 0)),
            _const_spec(wr.shape),
            _const_spec(wt.shape),
        ],
        out_specs=out_specs,
        out_shape=out_shape,
        scratch_shapes=[pltpu.VMEM((tm, LANES), F32)],
        compiler_params=_params(1),
        name="in_proj",
    )(x2, g[None, :], cos, sin, wr, wt)
    return dict(zip([n for n, *_ in _ROW_SEGS] + [n for n, *_ in _COL_SEGS], outs))


def _compress_kernel(x_ref, pea_ref, peb_ref, w1a_ref, w1b_ref, w2_ref, o_ref):
    x = x_ref[...]
    a = jnp.dot((x + pea_ref[...]).astype(BF16), w1a_ref[...], preferred_element_type=F32)
    b = jnp.dot((x + peb_ref[...]).astype(BF16), w1b_ref[...], preferred_element_type=F32)
    ncp = x.shape[0]
    h = a + pltpu.roll(b, ncp - 1, 0)
    h = h * jax.nn.sigmoid(h)
    o_ref[...] = jnp.dot(h.astype(BF16), w2_ref[...], preferred_element_type=F32).astype(o_ref.dtype)


def _compress(xc, pe, w1, w2, b, s):
    g_, dh = NSA_KV_GROUPS, HEAD_DIM
    ncp = s // CMP_STRIDE
    hid = w1.shape[1]
    halves = CMP_LEN // CMP_STRIDE
    assert halves == 2
    xr = xc.reshape(b, ncp, CMP_STRIDE * g_ * dh)
    eye = jnp.eye(g_, dtype=F32)
    pe_r = pe.reshape(halves, CMP_STRIDE, 1, dh)
    pe_t = jnp.broadcast_to(pe_r, (halves, CMP_STRIDE, g_, dh)).reshape(halves, 1, CMP_STRIDE * g_ * dh)
    w1r = w1.reshape(halves, CMP_STRIDE, dh, hid)
    w1bd = jnp.einsum("hldj,pg->hlpdgj", w1r, eye).reshape(halves, CMP_STRIDE * g_ * dh, g_ * hid).astype(BF16)
    w2bd = jnp.einsum("jd,pg,r->pjgrd", w2, eye, jnp.ones((2,), F32)).reshape(g_ * hid, g_ * 2 * dh).astype(BF16)
    kdim = CMP_STRIDE * g_ * dh
    return pl.pallas_call(
        _compress_kernel,
        grid=(b,),
        in_specs=[
            pl.BlockSpec((None, ncp, kdim), lambda i: (i, 0, 0)),
            _const_spec((1, kdim)),
            _const_spec((1, kdim)),
            _const_spec((kdim, g_ * hid)),
            _const_spec((kdim, g_ * hid)),
            _const_spec((g_ * hid, g_ * 2 * dh)),
        ],
        out_specs=pl.BlockSpec((None, ncp, g_ * 2 * dh), lambda i: (i, 0, 0)),
        out_shape=jax.ShapeDtypeStruct((b, ncp, g_ * 2 * dh), BF16),
        compiler_params=_params(1),
        name="compress",
    )(xr, pe_t[0], pe_t[1], w1bd[0], w1bd[1], w2bd)


def _split_heads(qt):
    row = lax.broadcasted_iota(jnp.int32, qt.shape, 0)
    zero = jnp.zeros_like(qt)
    return jnp.concatenate([jnp.where(row < HEAD_DIM, qt, zero), jnp.where(row >= HEAD_DIM, qt, zero)], axis=1)


def _flash_scratch(tk, m_cols, dv):
    row = pltpu.VMEM((1, m_cols), F32)
    sc = pltpu.VMEM((tk, m_cols), F32)
    return [sc, sc, row, row, pltpu.VMEM((2 * tk, m_cols), BF16), row, row, pltpu.VMEM((dv + ONES_ROWS, m_cols), F32)]


def _flash_tiles(qst, k_ref, vt_ref, scratch, tk, hi, mask_fn, inner_fn):
    sa_ref, sb_ref, ta_ref, tb_ref, p_ref, al_ref, m_ref, acc_ref = scratch
    dv = acc_ref.shape[0] - ONES_ROWS
    n_k = k_ref.shape[0] // tk

    def scores(t, s_ref, t_ref, fn):
        k = k_ref[pl.ds(pl.multiple_of(jnp.clip(t, 0, n_k - 1) * tk, tk), tk), :]
        s = fn(t, jnp.dot(k, qst, preferred_element_type=F32))
        t_ref[...] = jnp.max(s, axis=0, keepdims=True)
        s_ref[...] = s

    def probs():
        m_old = m_ref[...]
        m_new = jnp.maximum(m_old, jnp.maximum(ta_ref[...], tb_ref[...]))
        p_ref[:tk, :] = jnp.exp2(sa_ref[...] - m_new).astype(BF16)
        p_ref[tk:, :] = jnp.exp2(sb_ref[...] - m_new).astype(BF16)
        al_ref[...] = jnp.exp2(m_old - m_new)
        m_ref[...] = m_new

    def values(pair):
        vt = vt_ref[jnp.clip(pair, 0, n_k // 2 - 1)]
        acc_ref[...] = al_ref[...] * acc_ref[...] + jnp.dot(vt, p_ref[...], preferred_element_type=F32)

    n_pairs = hi // 2 + 1

    def pair_at(r):
        return jnp.where(r == 0, n_pairs - 1, r - 1)

    m_ref[...] = jnp.full(m_ref.shape, NEG_INF, F32)
    acc_ref[...] = jnp.zeros(acc_ref.shape, F32)
    p_ref[...] = jnp.zeros(p_ref.shape, BF16)
    al_ref[...] = jnp.ones(al_ref.shape, F32)

    scores(2 * pair_at(0), sa_ref, ta_ref, mask_fn)
    scores(2 * pair_at(0) + 1, sb_ref, tb_ref, mask_fn)

    def body(r, carry):
        values(pair_at(r - 1))
        probs()
        scores(2 * r, sa_ref, ta_ref, inner_fn)
        scores(2 * r + 1, sb_ref, tb_ref, inner_fn)
        return carry

    lax.fori_loop(0, n_pairs - 1, body, 0)
    values(pair_at(n_pairs - 2))
    probs()
    values(pair_at(n_pairs - 1))
    acc = acc_ref[...]
    return acc[dv:dv + 1], acc[:dv]


def _diff_kernel(lam_ref, g_ref, qt_ref, k_ref, vt_ref, o_ref, *scratch, tq, tk, lam_init):
    qi = pl.program_id(2)
    qst = _split_heads(qt_ref[...])
    m_cols = 2 * tq
    qpos = qi * tq + lax.broadcasted_iota(jnp.int32, (1, m_cols), 1) % tq
    krel = lax.broadcasted_iota(jnp.int32, (tk, m_cols), 0)

    def causal(t, s):
        return jnp.where(krel <= qpos - t * tk, s, NEG_INF)

    l, acc = _flash_tiles(qst, k_ref, vt_ref, scratch, tk, (qi * tq + tq - 1) // tk, causal, lambda t, s: s)
    o = acc * (1.0 / l)
    lf = lam_ref[...]
    lam = (jnp.exp(jnp.sum(lf[0:1] * lf[1:2], axis=1, keepdims=True))
           - jnp.exp(jnp.sum(lf[2:3] * lf[3:4], axis=1, keepdims=True)) + lam_init)
    y = o[:, :tq] - lam * o[:, tq:]
    y = y * lax.rsqrt(jnp.mean(y * y, axis=0, keepdims=True) + NORM_EPS)
    y = y.T * g_ref[...] * (1.0 - lam_init)
    o_ref[...] = y.astype(o_ref.dtype)


def _diff_attention(dqt, dk, dvt, diff_lambda, subln_g, b, s, layer):
    tk = KEY_TILE
    tq = 2 * tk
    hw = 2 * HEAD_DIM
    width = DIFF_HEADS * hw
    nq = s // tq
    return pl.pallas_call(
        functools.partial(_diff_kernel, tq=tq, tk=tk, lam_init=_lambda_init(layer)),
        grid=(b, DIFF_HEADS, nq),
        in_specs=[
            _const_spec(diff_lambda.shape),
            _const_spec((1, DIFF_VDIM)),
            pl.BlockSpec((hw, tq), lambda bi, h, i: (h, bi * nq + i)),
            pl.BlockSpec((None, s, hw), lambda bi, h, i: (bi, 0, h)),
            pl.BlockSpec((None, s // KEY_PAIR, DIFF_VDIM + ONES_ROWS, KEY_PAIR), lambda bi, h, i: (bi, 0, h, 0)),
        ],
        out_specs=pl.BlockSpec((None, tq, DIFF_VDIM), lambda bi, h, i: (bi, i, h)),
        out_shape=jax.ShapeDtypeStruct((b, s, DIFF_HEADS * DIFF_VDIM), BF16),
        scratch_shapes=_flash_scratch(tk, 2 * tq, DIFF_VDIM),
        compiler_params=_params(3),
        name="diff_attn",
    )(diff_lambda, subln_g[None, :], dqt, dk.reshape(b, s, width), dvt.reshape(b, s // KEY_PAIR, -1, KEY_PAIR))


def _nsa_kernel(ovl_ref, qt_ref, gate_ref, kc_ref, vc_ref, ks_ref, vst_ref, kw_ref, vwt_ref, o_ref, imp_ref, selb_ref, cnt_ref, *scratch, tq, tk):
    qi = pl.program_id(2)
    grp = pl.program_id(1)
    q0 = qi * tq
    m_cols = NSA_HPG * tq
    qt = qt_ref[...]
    qst = jnp.concatenate([_split_heads(qt[:LANES]), _split_heads(qt[LANES:])], axis=1)
    qpos_t = q0 + lax.broadcasted_iota(jnp.int32, (1, tq), 1)
    qpos = jnp.concatenate([qpos_t] * NSA_HPG, axis=1)
    dh = HEAD_DIM

    ncp = kc_ref.shape[0]
    last_cmp = lax.shift_right_arithmetic(qpos - (CMP_LEN - 1), int(math.log2(CMP_STRIDE)))
    cmask = lax.broadcasted_iota(jnp.int32, (ncp, m_cols), 0) <= last_cmp
    sc = jnp.dot(kc_ref[...], qst, preferred_element_type=F32)
    sc = jnp.where(cmask, sc, NEG_INF)
    pc = jnp.exp2(sc - jnp.max(sc, axis=0, keepdims=True))
    pc = pc * jnp.where(qpos >= CMP_LEN - 1, 1.0 / jnp.sum(pc, axis=0, keepdims=True), 0.0)
    o_c = lax.dot_general(vc_ref[...], pc.astype(BF16), (((0,), (0,)), ((), ())), preferred_element_type=F32)[:dh]

    pc_sum = pc[:, 0:tq]
    for h in range(1, NSA_HPG):
        pc_sum = pc_sum + pc[:, h * tq:(h + 1) * tq]
    imp = jnp.dot(ovl_ref[...], pc_sum, preferred_element_type=F32, precision=lax.Precision.HIGHEST)
    nsb = imp.shape[0]
    blk = lax.broadcasted_iota(jnp.int32, (nsb, tq), 0)
    qblk = qpos_t // SEL_LEN
    forced = (blk == 0) | (blk == qblk) | (blk == qblk - 1)
    imp = jnp.where(forced, FORCE_SCORE, imp)
    imp = jnp.where(blk <= qblk, imp, NEG_INF)
    imp_ref[...] = imp
    topk = min(SEL_TOPK, nsb)

    sub = 8
    cnt_ref[...] = jnp.zeros(cnt_ref.shape, jnp.int32)
    last_blk = (q0 + tq - 1) // SEL_LEN
    for g0 in range(0, nsb, sub):
        @pl.when(g0 <= last_blk)
        def _(g0=g0):
            cnts = [cnt_ref[b0:b0 + sub, :] for b0 in range(0, nsb, sub)]
            for j in range(g0, g0 + sub):
                row = jnp.broadcast_to(imp_ref[j:j + 1, :], (sub, tq))
                for i, b0 in enumerate(range(0, nsb, sub)):
                    cur = imp_ref[b0:b0 + sub, :]
                    if b0 > j:
                        beats = row >= cur
                    elif b0 + sub - 1 < j:
                        beats = row > cur
                    else:
                        beats = (row > cur) | ((row == cur) & (lax.broadcasted_iota(jnp.int32, (sub, tq), 0) > j - b0))
                    cnts[i] = cnts[i] + jnp.where(beats, 1, 0)
            for i, b0 in enumerate(range(0, nsb, sub)):
                cnt_ref[b0:b0 + sub, :] = cnts[i]
    selb_ref[...] = jnp.where(cnt_ref[...] < topk, 0.0, NEG_INF)

    krel = lax.broadcasted_iota(jnp.int32, (tk, m_cols), 0)
    bpt = tk // SEL_LEN
    last = (q0 + tq - 1) // tk

    def biased(t, s):
        first_blk = jnp.clip(t, 0, nsb // bpt - 1) * bpt
        rows = []
        for r in range(bpt):
            brow = selb_ref[pl.ds(first_blk + r, 1), :]
            brow = jnp.concatenate([brow] * NSA_HPG, axis=1)
            rows.append(jnp.broadcast_to(brow, (SEL_LEN, m_cols)))
        return s + jnp.concatenate(rows, axis=0)

    def selected(t, s):
        return jnp.where(krel <= qpos - t * tk, biased(t, s), NEG_INF)

    l_s, acc_s = _flash_tiles(qst, ks_ref, vst_ref, scratch, tk, last, selected, biased)
    o_s = acc_s * (1.0 / l_s)

    n_back = -(-(WINDOW - 1) // tk)
    win_tiles = [qi - n_back + r for r in range(n_back + 1)]
    s_w = []
    for r, t in enumerate(win_tiles):
        k = kw_ref[pl.ds(pl.multiple_of(jnp.maximum(t, 0) * tk, tk), tk), :]
        s = jnp.dot(k, qst, preferred_element_type=F32)
        d_min = (n_back - r) * tk - (tk - 1)
        d_max = (n_back - r) * tk + (tq - 1)
        if d_min < 0:
            s = jnp.where(krel <= qpos - t * tk, s, NEG_INF)
        if d_max >= WINDOW:
            s = jnp.where(krel > qpos - t * tk - WINDOW, s, NEG_INF)
        if r < n_back:
            s = s + jnp.where(t >= 0, 0.0, NEG_INF)
        s_w.append(s)
    m_w = functools.reduce(jnp.maximum, [jnp.max(s, axis=0, keepdims=True) for s in s_w])
    acc_w = jnp.zeros((dh + ONES_ROWS, m_cols), F32)
    for t, s in zip(win_tiles, s_w):
        p = jnp.exp2(s - m_w).astype(BF16)
        acc_w = acc_w + jnp.dot(vwt_ref[jnp.maximum(t, 0)], p, preferred_element_type=F32)
    o_w = acc_w[:dh] * (1.0 / acc_w[dh:dh + 1])

    gates = gate_ref[...].T
    outs = []
    for h in range(NSA_HPG):
        hg = grp * NSA_HPG + h
        sl = slice(h * tq, (h + 1) * tq)
        y = (_gate_row(gates, 0, hg) * o_c[:, sl] + _gate_row(gates, 1, hg) * o_s[:, sl] + _gate_row(gates, 2, hg) * o_w[:, sl])
        outs.append(y)
    y = jnp.concatenate(outs, axis=0)
    o_ref[...] = y.T.astype(o_ref.dtype)


def _gate_row(gates, branch, head):
    idx = branch * NSA_HEADS + head
    row = lax.broadcasted_iota(jnp.int32, gates.shape, 0)
    return jnp.sum(jnp.where(row == idx, gates, 0.0), axis=0, keepdims=True)


def _overlap_matrix(s):
    nc = s // CMP_STRIDE
    nsb = s // SEL_LEN
    ci = np.arange(nc)[None, :] * CMP_STRIDE
    sj = np.arange(nsb)[:, None] * SEL_LEN
    ovl = ((ci < sj + SEL_LEN) & (ci + CMP_LEN > sj)).astype(np.float32)
    ovl[:, (s - CMP_LEN) // CMP_STRIDE + 1:] = 0.0
    return jnp.asarray(ovl)


def _nsa_attention(nqt, ng, kcmp, vcmp, ksd, vst, kwd, vwt, b, s):
    tk = tq = KEY_TILE
    nq = s // tq
    gw = NSA_HPG * HEAD_DIM
    kw_ = 2 * HEAD_DIM
    ncp = s // CMP_STRIDE
    nsb = s // SEL_LEN
    seq3 = lambda a: a.reshape(b, s, a.shape[-1])
    vt4 = lambda a: a.reshape(b, -1, a.shape[-2], a.shape[-1])
    k_spec = pl.BlockSpec((None, s, kw_), lambda bi, g, i: (bi, 0, g))
    vt_spec = lambda tile: pl.BlockSpec((None, s // tile, HEAD_DIM + ONES_ROWS, tile), lambda bi, g, i: (bi, 0, g, 0))
    cmp_spec = pl.BlockSpec((None, ncp, kw_), lambda bi, g, i: (bi, 0, g))
    return pl.pallas_call(
        functools.partial(_nsa_kernel, tq=tq, tk=tk),
        grid=(b, NSA_KV_GROUPS, nq),
        in_specs=[
            _const_spec((nsb, ncp)),
            pl.BlockSpec((gw, tq), lambda bi, g, i: (g, bi * nq + i)),
            pl.BlockSpec((None, tq, LANES), lambda bi, g, i: (bi, i, 0)),
            cmp_spec, cmp_spec, k_spec, vt_spec(KEY_PAIR), k_spec, vt_spec(KEY_TILE),
        ],
        out_specs=pl.BlockSpec((None, tq, gw), lambda bi, g, i: (bi, i, g)),
        out_shape=jax.ShapeDtypeStruct((b, s, NSA_HEADS * HEAD_DIM), BF16),
        scratch_shapes=[pltpu.VMEM((nsb, tq), F32), pltpu.VMEM((nsb, tq), F32), pltpu.VMEM((nsb, tq), jnp.int32)] + _flash_scratch(tk, NSA_HPG * tq, HEAD_DIM),
        compiler_params=_params(3),
        name="nsa_attn",
    )(_overlap_matrix(s), nqt, seq3(ng), kcmp, vcmp, seq3(ksd), vt4(vst), seq3(kwd), vt4(vwt))


def _merge_kernel(x_ref, g_ref, ya_ref, yb_ref, wmg_ref, wa_ref, wb_ref, wo_ref, o_ref):
    x = x_ref[...]
    d = x.shape[1]
    ms = jnp.mean(x * x, axis=-1, keepdims=True)
    u = (x * lax.rsqrt(ms + NORM_EPS) * g_ref[...]).astype(BF16)
    ya = jnp.dot(ya_ref[...], wa_ref[...], preferred_element_type=F32)
    yb = jnp.dot(yb_ref[...], wb_ref[...], preferred_element_type=F32)
    mg_a = jax.nn.sigmoid(jnp.dot(u, wmg_ref[:, :d], preferred_element_type=F32))
    mg_b = jax.nn.sigmoid(jnp.dot(u, wmg_ref[:, d:], preferred_element_type=F32))
    merged = (mg_a * ya + mg_b * yb).astype(BF16)
    o_ref[...] = x + jnp.dot(merged, wo_ref[...], preferred_element_type=F32)


def _merge(x2, g, ya, yb, w_mg, w_a, w_b, w_o):
    t, d = x2.shape
    tm = min(t, 512)
    ws = [w.astype(BF16) for w in (w_mg, w_a, w_b, w_o)]
    return pl.pallas_call(
        _merge_kernel,
        grid=(t // tm,),
        in_specs=[
            pl.BlockSpec((tm, d), lambda i: (i, 0)),
            _const_spec((1, d)),
            pl.BlockSpec((tm, ya.shape[1]), lambda i: (i, 0)),
            pl.BlockSpec((tm, yb.shape[1]), lambda i: (i, 0)),
        ] + [_const_spec(w.shape) for w in ws],
        out_specs=pl.BlockSpec((tm, d), lambda i: (i, 0)),
        out_shape=jax.ShapeDtypeStruct((t, d), F32),
        compiler_params=_params(1),
        name="merge_out_proj",
    )(x2, g[None, :], ya, yb, *ws)


def _ffn_kernel(h_ref, g_ref, gf_ref, wg_ref, wu_ref, wd_ref, o_ref, *, chunk, final_norm):
    h = h_ref[...]
    ms = jnp.mean(h * h, axis=-1, keepdims=True)
    u = (h * lax.rsqrt(ms + NORM_EPS) * g_ref[...]).astype(BF16)
    d_ff = wg_ref.shape[1]
    acc = h
    for c0 in range(0, d_ff, chunk):
        a = jnp.dot(u, wg_ref[:, c0:c0 + chunk], preferred_element_type=F32)
        up = jnp.dot(u, wu_ref[:, c0:c0 + chunk], preferred_element_type=F32)
        act = (a * jax.nn.sigmoid(a) * up).astype(BF16)
        acc = acc + jnp.dot(act, wd_ref[c0:c0 + chunk, :], preferred_element_type=F32)
    if final_norm:
        ms = jnp.mean(acc * acc, axis=-1, keepdims=True)
        acc = acc * lax.rsqrt(ms + NORM_EPS) * gf_ref[...]
    o_ref[...] = acc


def _ffn(h2, g, g_final, w_gate, w_up, w_down, final_norm):
    t, d = h2.shape
    tm = min(t, 512)
    d_ff = w_gate.shape[1]
    chunk = 256
    assert d_ff % chunk == 0
    ws = [w.astype(BF16) for w in (w_gate, w_up, w_down)]
    return pl.pallas_call(
        functools.partial(_ffn_kernel, chunk=chunk, final_norm=final_norm),
        grid=(t // tm,),
        in_specs=[pl.BlockSpec((tm, d), lambda i: (i, 0)), _const_spec((1, d)), _const_spec((1, d))]
        + [_const_spec(w.shape) for w in ws],
        out_specs=pl.BlockSpec((tm, d), lambda i: (i, 0)),
        out_shape=jax.ShapeDtypeStruct((t, d), F32),
        compiler_params=_params(1),
        name="swiglu_ffn",
    )(h2, g[None, :], g_final[None, :], *ws)


def kernel(x, positions, attn_norm_g, w_in, diff_lambda, diff_subln_g, cmp_pe_k, cmp_pe_v, cmp_k_w1, cmp_k_w2, cmp_v_w1, cmp_v_w2, w_branch_a, w_branch_b, w_out, ffn_norm_g, w_gate, w_up, w_down, final_norm_g):
    b, s, d = x.shape
    depth = w_in.shape[0]
    cos, sin = _rope_tables(positions)
    h = x.reshape(b * s, d)
    gates_off = w_in.shape[2] - 2 * d
    for layer in range(depth):
        p = _project(h, attn_norm_g[layer], cos, sin, w_in[layer, :, :gates_off])
        kcmp = _compress(p["kc"], cmp_pe_k[layer], cmp_k_w1[layer], cmp_k_w2[layer], b, s)
        vcmp = _compress(p["vc"], cmp_pe_v[layer], cmp_v_w1[layer], cmp_v_w2[layer], b, s)
        ya = _diff_attention(p["dqT"], p["dk"], p["dvT"], diff_lambda[layer], diff_subln_g[layer], b, s, layer)
        yb = _nsa_attention(p["nqT"], p["ng"], kcmp, vcmp, p["ksd"], p["vsT"], p["kwd"], p["vwT"], b, s)
        h = _merge(h, attn_norm_g[layer], ya.reshape(b * s, -1), yb.reshape(b * s, -1),
                   w_in[layer, :, gates_off:], w_branch_a[layer], w_branch_b[layer], w_out[layer])
        h = _ffn(h, ffn_norm_g[layer], final_norm_g, w_gate[layer], w_up[layer], w_down[layer], layer == depth - 1)
    return h.reshape(b, s, d)
```

```python
import functools
import math
from typing import Callable, NamedTuple, Sequence

import jax
import jax.numpy as jnp
import numpy as np
from jax import lax
from jax.experimental import pallas as pl
from jax.experimental.pallas import tpu as pltpu

HEAD_DIM = 64
ROPE_THETA = 10000.0
NORM_EPS = 1e-6
NEG_INF = -1e30
FORCE_SCORE = 1e9

DIFF_HEADS = 4
DIFF_VDIM = 2 * HEAD_DIM
NSA_HEADS = 8
NSA_KV_GROUPS = 2
NSA_HPG = NSA_HEADS // NSA_KV_GROUPS
CMP_LEN = 32
CMP_STRIDE = 16
SEL_LEN = 64
SEL_TOPK = 16
WINDOW = 512

LANES = 128
VMEM_LIMIT = 56 * 1024 * 1024
KEY_TILE = 256

BF16 = jnp.bfloat16
F32 = jnp.float32


def _lambda_init(layer):
    return 0.8 - 0.6 * math.exp(-0.3 * layer)


def _params(n_axes):
    return pltpu.CompilerParams(dimension_semantics=("arbitrary",) * n_axes, vmem_limit_bytes=VMEM_LIMIT)


def _const_spec(shape):
    nd = len(shape)
    return pl.BlockSpec(shape, lambda *_: (0,) * nd)


def _rope_table_kernel(pos_ref, invf_ref, cos_ref, sin_ref):
    ang = pos_ref[...].astype(F32) * invf_ref[...]
    cos_ref[...] = jnp.cos(ang)
    sin_ref[...] = jnp.sin(ang)


def _rope_tables(positions):
    half = HEAD_DIM // 2
    per_row = LANES // half
    t = positions.size
    rows = t // per_row
    pos_e = jnp.repeat(positions.reshape(rows, per_row), half, axis=1)
    inv_freq = 1.0 / (ROPE_THETA ** (jnp.arange(0, HEAD_DIM, 2, dtype=F32) / HEAD_DIM))
    invf = jnp.tile(inv_freq, per_row)[None, :]
    tr = min(rows, 1024)
    cos, sin = pl.pallas_call(
        _rope_table_kernel,
        grid=(rows // tr,),
        in_specs=[pl.BlockSpec((tr, LANES), lambda i: (i, 0)), _const_spec((1, LANES))],
        out_specs=[pl.BlockSpec((tr, LANES), lambda i: (i, 0))] * 2,
        out_shape=[jax.ShapeDtypeStruct((rows, LANES), F32)] * 2,
        compiler_params=_params(1),
        name="rope_tables",
    )(pos_e, invf)
    return cos.reshape(t, half), sin.reshape(t, half)


_ROW_SEGS = (("dk", 512, BF16, True), ("ksd", 256, BF16, True), ("kwd", 256, BF16, True), ("kc", 128, F32, True),
             ("vc", 128, F32, False), ("ng", 128, F32, False))
KEY_PAIR = 2 * KEY_TILE
ONES_ROWS = 16
_COL_SEGS = (("dqT", 512, True, None, None), ("nqT", 512, True, None, None),
             ("dvT", 512, False, KEY_PAIR, DIFF_VDIM), ("vsT", 128, False, KEY_PAIR, HEAD_DIM), ("vwT", 128, False, KEY_TILE, HEAD_DIM))
_STRIDED_SEGS = ("kc", "vc")


def _proj_kernel(x_ref, g_ref, cos_ref, sin_ref, wr_ref, wt_ref, *out_refs):
    x = x_ref[...]
    ms = jnp.mean(x * x, axis=-1, keepdims=True)
    u = (x * lax.rsqrt(ms + NORM_EPS) * g_ref[...]).astype(BF16)
    half = HEAD_DIM // 2
    reps = LANES // half

    def rope_fn(c, s, axis):
        idx = lax.broadcasted_iota(jnp.int32, c.shape, axis)
        upper = (idx % HEAD_DIM) >= half
        s_up = jnp.where(upper, s, 0.0)
        s_lo = jnp.where(upper, 0.0, -s)
        return lambda y: y * c + pltpu.roll(y, half, axis) * s_up + pltpu.roll(y, LANES - half, axis) * s_lo

    c = jnp.concatenate([cos_ref[...]] * reps, axis=1)
    s = jnp.concatenate([sin_ref[...]] * reps, axis=1)
    rope_rows = rope_fn(c, s, 1)
    rope_cols = rope_fn(c.T, s.T, 0)

    *refs, stage_ref = out_refs
    tm = x.shape[0]
    off = 0
    for name, width, _, roped in _ROW_SEGS:
        o_ref = refs.pop(0)
        for c0 in range(0, width, 2 * LANES):
            cw = min(2 * LANES, width - c0)
            y = jnp.dot(u, wr_ref[:, off + c0:off + c0 + cw], preferred_element_type=F32)
            for l0 in range(0, cw, LANES):
                yl = y[:, l0:l0 + LANES]
                if roped:
                    yl = rope_rows(yl)
                if name == "ng":
                    yl = jax.nn.sigmoid(yl)
                if name in _STRIDED_SEGS:
                    stage_ref[...] = yl
                    for l in range(CMP_STRIDE):
                        o_ref[:, l * LANES:(l + 1) * LANES] = stage_ref[pl.ds(l, tm // CMP_STRIDE, stride=CMP_STRIDE), :]
                else:
                    o_ref[:, c0 + l0:c0 + l0 + LANES] = yl.astype(o_ref.dtype)
        off += width

    off = 0
    for name, rows, roped, tile, group in _COL_SEGS:
        o_ref = refs.pop(0)
        for r0 in range(0, rows, 2 * LANES):
            rw = min(2 * LANES, rows - r0)
            yt = lax.dot_general(wt_ref[off + r0:off + r0 + rw, :], u, (((1,), (1,)), ((), ())), preferred_element_type=F32)
            for s0 in range(0, rw, LANES):
                ys = yt[s0:s0 + LANES]
                if roped:
                    ys = rope_cols(ys)
                ys = ys.astype(o_ref.dtype)
                if not tile:
                    o_ref[r0 + s0:r0 + s0 + LANES, :] = ys
                    continue
                for g0 in range(0, LANES, group):
                    dst = (r0 + s0 + g0) // group * (group + ONES_ROWS)
                    for j in range(tm // tile):
                        o_ref[j, dst:dst + group, :] = ys[g0:g0 + group, j * tile:(j + 1) * tile]
                        o_ref[j, dst + group:dst + group + ONES_ROWS, :] = jnp.ones((ONES_ROWS, tile), o_ref.dtype)
        off += rows


def _dup_groups(w):
    d = w.shape[0]
    wg = w.reshape(d, NSA_KV_GROUPS, 1, HEAD_DIM)
    return jnp.broadcast_to(wg, (d, NSA_KV_GROUPS, 2, HEAD_DIM)).reshape(d, NSA_KV_GROUPS * 2 * HEAD_DIM)


def _project(x2, g, cos, sin, w_in):
    t, d = x2.shape
    qk = DIFF_HEADS * 2 * HEAD_DIM
    kv = NSA_KV_GROUPS * HEAD_DIM
    splits = (qk, qk, DIFF_HEADS * DIFF_VDIM, NSA_HEADS * HEAD_DIM, kv, kv, kv, kv, kv, kv, 3 * NSA_HEADS)
    offs = np.cumsum((0,) + splits)
    dq, dk, dv, nq, kc, vc, ks, vs, kw, vw, ng = (w_in[:, offs[i]:offs[i + 1]] for i in range(len(splits)))
    scale = HEAD_DIM ** -0.5 * math.log2(math.e)
    ng = jnp.pad(ng, ((0, 0), (0, LANES - ng.shape[1])))
    wr = jnp.concatenate([dk, _dup_groups(ks), _dup_groups(kw), kc, vc, ng], axis=1).astype(BF16)
    wt = jnp.concatenate([dq * scale, nq * scale, dv, vs, vw], axis=1).T.astype(BF16)
    tm = min(t, 512)
    assert tm % KEY_TILE == 0
    half = HEAD_DIM // 2
    out_specs, out_shape = [], []
    for name, w, dt, _ in _ROW_SEGS:
        fold = CMP_STRIDE if name in _STRIDED_SEGS else 1
        out_specs.append(pl.BlockSpec((tm // fold, w * fold), lambda i: (i, 0)))
        out_shape.append(jax.ShapeDtypeStruct((t // fold, w * fold), dt))
    for _, rows, _, tile, group in _COL_SEGS:
        if tile:
            assert tm % tile == 0
            padded = rows // group * (group + ONES_ROWS)
            out_specs.append(pl.BlockSpec((tm // tile, padded, tile), lambda i: (i, 0, 0)))
            out_shape.append(jax.ShapeDtypeStruct((t // tile, padded, tile), BF16))
        else:
            out_specs.append(pl.BlockSpec((rows, tm), lambda i: (0, i)))
            out_shape.append(jax.ShapeDtypeStruct((rows, t), BF16))
    outs = pl.pallas_call(
        _proj_kernel,
        grid=(t // tm,),
        in_specs=[
            pl.BlockSpec((tm, d), lambda i: (i, 0)),
            _const_spec((1, d)),
            pl.BlockSpec((tm, half), lambda i: (i, 0)),
            pl.BlockSpec((tm, half), lambda i: (i, 0)),
            _const_spec(wr.shape),
            _const_spec(wt.shape),
        ],
        out_specs=out_specs,
        out_shape=out_shape,
        scratch_shapes=[pltpu.VMEM((tm, LANES), F32)],
        compiler_params=_params(1),
        name="in_proj",
    )(x2, g[None, :], cos, sin, wr, wt)
    return dict(zip([n for n, *_ in _ROW_SEGS] + [n for n, *_ in _COL_SEGS], outs))


def _compress_kernel(x_ref, pea_ref, peb_ref, w1a_ref, w1b_ref, w2_ref, o_ref):
    x = x_ref[...]
    a = jnp.dot((x + pea_ref[...]).astype(BF16), w1a_ref[...], preferred_element_type=F32)
    b = jnp.dot((x + peb_ref[...]).astype(BF16), w1b_ref[...], preferred_element_type=F32)
    ncp = x.shape[0]
    h = a + pltpu.roll(b, ncp - 1, 0)
    h = h * jax.nn.sigmoid(h)
    o_ref[...] = jnp.dot(h.astype(BF16), w2_ref[...], preferred_element_type=F32).astype(o_ref.dtype)


def _compress(xc, pe, w1, w2, b, s):
    g_, dh = NSA_KV_GROUPS, HEAD_DIM
    ncp = s // CMP_STRIDE
    hid = w1.shape[1]
    halves = CMP_LEN // CMP_STRIDE
    assert halves == 2
    xr = xc.reshape(b, ncp, CMP_STRIDE * g_ * dh)
    eye = jnp.eye(g_, dtype=F32)
    pe_r = pe.reshape(halves, CMP_STRIDE, 1, dh)
    pe_t = jnp.broadcast_to(pe_r, (halves, CMP_STRIDE, g_, dh)).reshape(halves, 1, CMP_STRIDE * g_ * dh)
    w1r = w1.reshape(halves, CMP_STRIDE, dh, hid)
    w1bd = jnp.einsum("hldj,pg->hlpdgj", w1r, eye).reshape(halves, CMP_STRIDE * g_ * dh, g_ * hid).astype(BF16)
    w2bd = jnp.einsum("jd,pg,r->pjgrd", w2, eye, jnp.ones((2,), F32)).reshape(g_ * hid, g_ * 2 * dh).astype(BF16)
    kdim = CMP_STRIDE * g_ * dh
    return pl.pallas_call(
        _compress_kernel,
        grid=(b,),
        in_specs=[
            pl.BlockSpec((None, ncp, kdim), lambda i: (i, 0, 0)),
            _const_spec((1, kdim)),
            _const_spec((1, kdim)),
            _const_spec((kdim, g_ * hid)),
            _const_spec((kdim, g_ * hid)),
            _const_spec((g_ * hid, g_ * 2 * dh)),
        ],
        out_specs=pl.BlockSpec((None, ncp, g_ * 2 * dh), lambda i: (i, 0, 0)),
        out_shape=jax.ShapeDtypeStruct((b, ncp, g_ * 2 * dh), BF16),
        compiler_params=_params(1),
        name="compress",
    )(xr, pe_t[0], pe_t[1], w1bd[0], w1bd[1], w2bd)


def _split_heads(qt):
    row = lax.broadcasted_iota(jnp.int32, qt.shape, 0)
    zero = jnp.zeros_like(qt)
    return jnp.concatenate([jnp.where(row < HEAD_DIM, qt, zero), jnp.where(row >= HEAD_DIM, qt, zero)], axis=1)


def _flash_scratch(tk, m_cols, dv):
    row = pltpu.VMEM((1, m_cols), F32)
    sc = pltpu.VMEM((tk, m_cols), F32)
    return [sc, sc, row, row, pltpu.VMEM((2 * tk, m_cols), BF16), row, row, pltpu.VMEM((dv + ONES_ROWS, m_cols), F32)]


class _Stream(NamedTuple):
    qst: jax.Array
    load_k: Callable
    load_vt: Callable
    scratch: Sequence
    mask_fn: Callable
    inner_fn: Callable


def _flash_tiles(streams, tk, n_k, hi):
    def scores(st, t, slot, masked):
        s_ref, t_ref = st.scratch[slot], st.scratch[2 + slot]
        k = st.load_k(pl.multiple_of(jnp.clip(t, 0, n_k - 1) * tk, tk))
        s = (st.mask_fn if masked else st.inner_fn)(t, jnp.dot(k, st.qst, preferred_element_type=F32))
        t_ref[...] = jnp.max(s, axis=0, keepdims=True)
        s_ref[...] = s

    def probs(st):
        sa_ref, sb_ref, ta_ref, tb_ref, p_ref, al_ref, m_ref, _ = st.scratch
        m_old = m_ref[...]
        m_new = jnp.maximum(m_old, jnp.maximum(ta_ref[...], tb_ref[...]))
        p_ref[:tk, :] = jnp.exp2(sa_ref[...] - m_new).astype(BF16)
        p_ref[tk:, :] = jnp.exp2(sb_ref[...] - m_new).astype(BF16)
        al_ref[...] = jnp.exp2(m_old - m_new)
        m_ref[...] = m_new

    def values(st, pair):
        p_ref, al_ref, acc_ref = st.scratch[4], st.scratch[5], st.scratch[7]
        vt = st.load_vt(jnp.clip(pair, 0, n_k // 2 - 1))
        acc_ref[...] = al_ref[...] * acc_ref[...] + jnp.dot(vt, p_ref[...], preferred_element_type=F32)

    def each(fn, *args):
        for st in streams:
            fn(st, *args)

    def init(st):
        _, _, _, _, p_ref, al_ref, m_ref, acc_ref = st.scratch
        m_ref[...] = jnp.full(m_ref.shape, NEG_INF, F32)
        acc_ref[...] = jnp.zeros(acc_ref.shape, F32)
        p_ref[...] = jnp.zeros(p_ref.shape, BF16)
        al_ref[...] = jnp.ones(al_ref.shape, F32)

    n_pairs = hi // 2 + 1

    def pair_at(r):
        return jnp.where(r == 0, n_pairs - 1, r - 1)

    each(init)
    each(scores, 2 * pair_at(0), 0, True)
    each(scores, 2 * pair_at(0) + 1, 1, True)

    def body(r, carry):
        each(values, pair_at(r - 1))
        each(probs)
        each(scores, 2 * r, 0, False)
        each(scores, 2 * r + 1, 1, False)
        return carry

    lax.fori_loop(0, n_pairs - 1, body, 0)
    each(values, pair_at(n_pairs - 2))
    each(probs)
    each(values, pair_at(n_pairs - 1))
    out = []
    for st in streams:
        acc = st.scratch[7][...]
        dv = acc.shape[0] - ONES_ROWS
        out.append((acc[dv:dv + 1], acc[:dv]))
    return out


def _diff_kernel(lam_ref, g_ref, qt_ref, k_ref, vt_ref, o_ref, *scratch, tq, tk, heads, lam_init):
    qi = pl.program_id(2)
    m_cols = 2 * tq
    qpos = qi * tq + lax.broadcasted_iota(jnp.int32, (1, m_cols), 1) % tq
    krel = lax.broadcasted_iota(jnp.int32, (tk, m_cols), 0)
    hw = 2 * HEAD_DIM
    vrows = DIFF_VDIM + ONES_ROWS
    n_sc = len(scratch) // heads

    def causal(t, s):
        return jnp.where(krel <= qpos - t * tk, s, NEG_INF)

    streams = [
        _Stream(
            qst=_split_heads(qt_ref[h * hw:(h + 1) * hw, :]),
            load_k=lambda row, h=h: k_ref[pl.ds(row, tk), h * hw:(h + 1) * hw],
            load_vt=lambda pair, h=h: vt_ref[pair, h * vrows:(h + 1) * vrows, :],
            scratch=scratch[h * n_sc:(h + 1) * n_sc],
            mask_fn=causal,
            inner_fn=lambda t, s: s,
        )
        for h in range(heads)
    ]
    results = _flash_tiles(streams, tk, k_ref.shape[0] // tk, (qi * tq + tq - 1) // tk)
    lf = lam_ref[...]
    lam = (jnp.exp(jnp.sum(lf[0:1] * lf[1:2], axis=1, keepdims=True))
           - jnp.exp(jnp.sum(lf[2:3] * lf[3:4], axis=1, keepdims=True)) + lam_init)
    for h, (l, acc) in enumerate(results):
        o = acc * (1.0 / l)
        y = o[:, :tq] - lam * o[:, tq:]
        y = y * lax.rsqrt(jnp.mean(y * y, axis=0, keepdims=True) + NORM_EPS)
        y = y.T * g_ref[...] * (1.0 - lam_init)
        o_ref[:, h * DIFF_VDIM:(h + 1) * DIFF_VDIM] = y.astype(o_ref.dtype)


def _diff_attention(dqt, dk, dvt, diff_lambda, subln_g, b, s, layer):
    tk = KEY_TILE
    tq = 2 * tk
    heads = 2
    hw = heads * 2 * HEAD_DIM
    vrows = heads * (DIFF_VDIM + ONES_ROWS)
    nq = s // tq
    return pl.pallas_call(
        functools.partial(_diff_kernel, tq=tq, tk=tk, heads=heads, lam_init=_lambda_init(layer)),
        grid=(b, DIFF_HEADS // heads, nq),
        in_specs=[
            _const_spec(diff_lambda.shape),
            _const_spec((1, DIFF_VDIM)),
            pl.BlockSpec((hw, tq), lambda bi, h, i: (h, bi * nq + i)),
            pl.BlockSpec((None, s, hw), lambda bi, h, i: (bi, 0, h)),
            pl.BlockSpec((None, s // KEY_PAIR, vrows, KEY_PAIR), lambda bi, h, i: (bi, 0, h, 0)),
        ],
        out_specs=pl.BlockSpec((None, tq, heads * DIFF_VDIM), lambda bi, h, i: (bi, i, h)),
        out_shape=jax.ShapeDtypeStruct((b, s, DIFF_HEADS * DIFF_VDIM), BF16),
        scratch_shapes=_flash_scratch(tk, 2 * tq, DIFF_VDIM) * heads,
        compiler_params=_params(3),
        name="diff_attn",
    )(diff_lambda, subln_g[None, :], dqt, dk.reshape(b, s, -1), dvt.reshape(b, s // KEY_PAIR, -1, KEY_PAIR))


def _nsa_kernel(ovl_ref, qt_ref, gate_ref, kc_ref, vc_ref, ks_ref, vst_ref, kw_ref, vwt_ref, o_ref, imp_ref, selb_ref, cnt_ref, *scratch, tq, tk):
    qi = pl.program_id(1)
    q0 = qi * tq
    m_cols = NSA_HPG * tq
    qpos_t = q0 + lax.broadcasted_iota(jnp.int32, (1, tq), 1)
    qpos = jnp.concatenate([qpos_t] * NSA_HPG, axis=1)
    dh = HEAD_DIM
    groups = range(NSA_KV_GROUPS)
    gw = NSA_HPG * dh
    kwid = 2 * dh
    vrows = dh + ONES_ROWS
    n_sc = len(scratch) // NSA_KV_GROUPS
    ncp = kc_ref.shape[0]
    nsb = imp_ref.shape[1]
    topk = min(SEL_TOPK, nsb)

    qsts, o_cs = [], []
    last_cmp = lax.shift_right_arithmetic(qpos - (CMP_LEN - 1), int(math.log2(CMP_STRIDE)))
    cmask = lax.broadcasted_iota(jnp.int32, (ncp, m_cols), 0) <= last_cmp
    blk = lax.broadcasted_iota(jnp.int32, (nsb, tq), 0)
    qblk = qpos_t // SEL_LEN
    forced = (blk == 0) | (blk == qblk) | (blk == qblk - 1)
    for g in groups:
        qt = qt_ref[g * gw:(g + 1) * gw, :]
        qst = jnp.concatenate([_split_heads(qt[:LANES]), _split_heads(qt[LANES:])], axis=1)
        qsts.append(qst)
        sc = jnp.dot(kc_ref[:, g * kwid:(g + 1) * kwid], qst, preferred_element_type=F32)
        sc = jnp.where(cmask, sc, NEG_INF)
        pc = jnp.exp2(sc - jnp.max(sc, axis=0, keepdims=True))
        pc = pc * jnp.where(qpos >= CMP_LEN - 1, 1.0 / jnp.sum(pc, axis=0, keepdims=True), 0.0)
        o_cs.append(lax.dot_general(vc_ref[:, g * kwid:(g + 1) * kwid], pc.astype(BF16), (((0,), (0,)), ((), ())),
                                    preferred_element_type=F32)[:dh])
        pc_sum = pc[:, 0:tq]
        for h in range(1, NSA_HPG):
            pc_sum = pc_sum + pc[:, h * tq:(h + 1) * tq]
        imp = jnp.dot(ovl_ref[...], pc_sum, preferred_element_type=F32, precision=lax.Precision.HIGHEST)
        imp = jnp.where(forced, FORCE_SCORE, imp)
        imp_ref[g] = jnp.where(blk <= qblk, imp, NEG_INF)
        cnt_ref[g] = jnp.zeros((nsb, tq), jnp.int32)

    sub = 8
    last_blk = (q0 + tq - 1) // SEL_LEN
    for g0 in range(0, nsb, sub):
        @pl.when(g0 <= last_blk)
        def _(g0=g0):
            for g in groups:
                cnts = [cnt_ref[g, b0:b0 + sub, :] for b0 in range(0, nsb, sub)]
                for j in range(g0, g0 + sub):
                    row = jnp.broadcast_to(imp_ref[g, j:j + 1, :], (sub, tq))
                    for i, b0 in enumerate(range(0, nsb, sub)):
                        cur = imp_ref[g, b0:b0 + sub, :]
                        if b0 > j:
                            beats = row >= cur
                        elif b0 + sub - 1 < j:
                            beats = row > cur
                        else:
                            beats = (row > cur) | ((row == cur) & (lax.broadcasted_iota(jnp.int32, (sub, tq), 0) > j - b0))
                        cnts[i] = cnts[i] + jnp.where(beats, 1, 0)
                for i, b0 in enumerate(range(0, nsb, sub)):
                    cnt_ref[g, b0:b0 + sub, :] = cnts[i]
    for g in groups:
        selb_ref[g] = jnp.where(cnt_ref[g] < topk, 0.0, NEG_INF)

    krel = lax.broadcasted_iota(jnp.int32, (tk, m_cols), 0)
    bpt = tk // SEL_LEN
    last = (q0 + tq - 1) // tk

    def biased(g):
        def fn(t, s):
            first_blk = jnp.clip(t, 0, nsb // bpt - 1) * bpt
            rows = []
            for r in range(bpt):
                brow = selb_ref[g, pl.ds(first_blk + r, 1), :]
                brow = jnp.concatenate([brow] * NSA_HPG, axis=1)
                rows.append(jnp.broadcast_to(brow, (SEL_LEN, m_cols)))
            return s + jnp.concatenate(rows, axis=0)
        return fn

    def selected(g):
        return lambda t, s: jnp.where(krel <= qpos - t * tk, biased(g)(t, s), NEG_INF)

    streams = [
        _Stream(
            qst=qsts[g],
            load_k=lambda row, g=g: ks_ref[pl.ds(row, tk), g * kwid:(g + 1) * kwid],
            load_vt=lambda pair, g=g: vst_ref[pair, g * vrows:(g + 1) * vrows, :],
            scratch=scratch[g * n_sc:(g + 1) * n_sc],
            mask_fn=selected(g),
            inner_fn=biased(g),
        )
        for g in groups
    ]
    sel_out = _flash_tiles(streams, tk, ks_ref.shape[0] // tk, last)

    gates = gate_ref[...].T
    n_back = -(-(WINDOW - 1) // tk)
    win_tiles = [qi * (tq // tk) - n_back + r for r in range(n_back + tq // tk)]
    for g in groups:
        l_s, acc_s = sel_out[g]
        o_s = acc_s * (1.0 / l_s)

        s_w = []
        for r, t in enumerate(win_tiles):
            k = kw_ref[pl.ds(pl.multiple_of(jnp.maximum(t, 0) * tk, tk), tk), g * kwid:(g + 1) * kwid]
            s = jnp.dot(k, qsts[g], preferred_element_type=F32)
            d_min = (n_back - r) * tk - (tk - 1)
            d_max = (n_back - r) * tk + (tq - 1)
            if d_min < 0:
                s = jnp.where(krel <= qpos - t * tk, s, NEG_INF)
            if d_max >= WINDOW:
                s = jnp.where(krel > qpos - t * tk - WINDOW, s, NEG_INF)
            if r < n_back:
                s = s + jnp.where(t >= 0, 0.0, NEG_INF)
            s_w.append(s)
        m_w = functools.reduce(jnp.maximum, [jnp.max(s, axis=0, keepdims=True) for s in s_w])
        acc_w = jnp.zeros((vrows, m_cols), F32)
        for t, s in zip(win_tiles, s_w):
            p = jnp.exp2(s - m_w).astype(BF16)
            acc_w = acc_w + jnp.dot(vwt_ref[jnp.maximum(t, 0), g * vrows:(g + 1) * vrows, :], p, preferred_element_type=F32)
        o_w = acc_w[:dh] * (1.0 / acc_w[dh:dh + 1])

        outs = []
        for h in range(NSA_HPG):
            sl = slice(h * tq, (h + 1) * tq)
            gate = lambda branch, h=h: gates[branch * NSA_HEADS + g * NSA_HPG + h:branch * NSA_HEADS + g * NSA_HPG + h + 1, :]
            outs.append(gate(0) * o_cs[g][:, sl] + gate(1) * o_s[:, sl] + gate(2) * o_w[:, sl])
        y = jnp.concatenate(outs, axis=0)
        o_ref[:, g * gw:(g + 1) * gw] = y.T.astype(o_ref.dtype)


def _overlap_matrix(s):
    nc = s // CMP_STRIDE
    nsb = s // SEL_LEN
    ci = np.arange(nc)[None, :] * CMP_STRIDE
    sj = np.arange(nsb)[:, None] * SEL_LEN
    ovl = ((ci < sj + SEL_LEN) & (ci + CMP_LEN > sj)).astype(np.float32)
    ovl[:, (s - CMP_LEN) // CMP_STRIDE + 1:] = 0.0
    return jnp.asarray(ovl)


def _nsa_attention(nqt, ng, kcmp, vcmp, ksd, vst, kwd, vwt, b, s):
    tk = tq = KEY_TILE
    nq = s // tq
    g_ = NSA_KV_GROUPS
    qw = NSA_HEADS * HEAD_DIM
    kw_ = g_ * 2 * HEAD_DIM
    vrows = g_ * (HEAD_DIM + ONES_ROWS)
    ncp = s // CMP_STRIDE
    nsb = s // SEL_LEN
    seq3 = lambda a: a.reshape(b, s, a.shape[-1])
    vt4 = lambda a: a.reshape(b, -1, a.shape[-2], a.shape[-1])
    k_spec = pl.BlockSpec((None, s, kw_), lambda bi, i: (bi, 0, 0))
    vt_spec = lambda tile: pl.BlockSpec((None, s // tile, vrows, tile), lambda bi, i: (bi, 0, 0, 0))
    cmp_spec = pl.BlockSpec((None, ncp, kw_), lambda bi, i: (bi, 0, 0))
    per_group = lambda dt: pltpu.VMEM((g_, nsb, tq), dt)
    return pl.pallas_call(
        functools.partial(_nsa_kernel, tq=tq, tk=tk),
        grid=(b, nq),
        in_specs=[
            _const_spec((nsb, ncp)),
            pl.BlockSpec((qw, tq), lambda bi, i: (0, bi * nq + i)),
            pl.BlockSpec((None, tq, LANES), lambda bi, i: (bi, i, 0)),
            cmp_spec, cmp_spec, k_spec, vt_spec(KEY_PAIR), k_spec, vt_spec(KEY_TILE),
        ],
        out_specs=pl.BlockSpec((None, tq, qw), lambda bi, i: (bi, i, 0)),
        out_shape=jax.ShapeDtypeStruct((b, s, qw), BF16),
        scratch_shapes=[per_group(F32), per_group(F32), per_group(jnp.int32)] + _flash_scratch(tk, NSA_HPG * tq, HEAD_DIM) * g_,
        compiler_params=_params(2),
        name="nsa_attn",
    )(_overlap_matrix(s), nqt, seq3(ng), kcmp, vcmp, seq3(ksd), vt4(vst), seq3(kwd), vt4(vwt))


def _merge_kernel(x_ref, g_ref, ya_ref, yb_ref, wmg_ref, wa_ref, wb_ref, wo_ref, o_ref):
    x = x_ref[...]
    d = x.shape[1]
    ms = jnp.mean(x * x, axis=-1, keepdims=True)
    u = (x * lax.rsqrt(ms + NORM_EPS) * g_ref[...]).astype(BF16)
    ya = jnp.dot(ya_ref[...], wa_ref[...], preferred_element_type=F32)
    yb = jnp.dot(yb_ref[...], wb_ref[...], preferred_element_type=F32)
    mg_a = jax.nn.sigmoid(jnp.dot(u, wmg_ref[:, :d], preferred_element_type=F32))
    mg_b = jax.nn.sigmoid(jnp.dot(u, wmg_ref[:, d:], preferred_element_type=F32))
    merged = (mg_a * ya + mg_b * yb).astype(BF16)
    o_ref[...] = x + jnp.dot(merged, wo_ref[...], preferred_element_type=F32)


def _merge(x2, g, ya, yb, w_mg, w_a, w_b, w_o):
    t, d = x2.shape
    tm = min(t, 512)
    ws = [w.astype(BF16) for w in (w_mg, w_a, w_b, w_o)]
    return pl.pallas_call(
        _merge_kernel,
        grid=(t // tm,),
        in_specs=[
            pl.BlockSpec((tm, d), lambda i: (i, 0)),
            _const_spec((1, d)),
            pl.BlockSpec((tm, ya.shape[1]), lambda i: (i, 0)),
            pl.BlockSpec((tm, yb.shape[1]), lambda i: (i, 0)),
        ] + [_const_spec(w.shape) for w in ws],
        out_specs=pl.BlockSpec((tm, d), lambda i: (i, 0)),
        out_shape=jax.ShapeDtypeStruct((t, d), F32),
        compiler_params=_params(1),
        name="merge_out_proj",
    )(x2, g[None, :], ya, yb, *ws)


def _ffn_kernel(h_ref, g_ref, gf_ref, wg_ref, wu_ref, wd_ref, o_ref, *, chunk, final_norm):
    h = h_ref[...]
    ms = jnp.mean(h * h, axis=-1, keepdims=True)
    u = (h * lax.rsqrt(ms + NORM_EPS) * g_ref[...]).astype(BF16)
    d_ff = wg_ref.shape[1]
    acc = h
    for c0 in range(0, d_ff, chunk):
        a = jnp.dot(u, wg_ref[:, c0:c0 + chunk], preferred_element_type=F32)
        up = jnp.dot(u, wu_ref[:, c0:c0 + chunk], preferred_element_type=F32)
        act = (a * jax.nn.sigmoid(a) * up).astype(BF16)
        acc = acc + jnp.dot(act, wd_ref[c0:c0 + chunk, :], preferred_element_type=F32)
    if final_norm:
        ms = jnp.mean(acc * acc, axis=-1, keepdims=True)
        acc = acc * lax.rsqrt(ms + NORM_EPS) * gf_ref[...]
    o_ref[...] = acc


def _ffn(h2, g, g_final, w_gate, w_up, w_down, final_norm):
    t, d = h2.shape
    tm = min(t, 512)
    d_ff = w_gate.shape[1]
    chunk = 256
    assert d_ff % chunk == 0
    ws = [w.astype(BF16) for w in (w_gate, w_up, w_down)]
    return pl.pallas_call(
        functools.partial(_ffn_kernel, chunk=chunk, final_norm=final_norm),
        grid=(t // tm,),
        in_specs=[pl.BlockSpec((tm, d), lambda i: (i, 0)), _const_spec((1, d)), _const_spec((1, d))]
        + [_const_spec(w.shape) for w in ws],
        out_specs=pl.BlockSpec((tm, d), lambda i: (i, 0)),
        out_shape=jax.ShapeDtypeStruct((t, d), F32),
        compiler_params=_params(1),
        name="swiglu_ffn",
    )(h2, g[None, :], g_final[None, :], *ws)


def kernel(x, positions, attn_norm_g, w_in, diff_lambda, diff_subln_g, cmp_pe_k, cmp_pe_v, cmp_k_w1, cmp_k_w2, cmp_v_w1, cmp_v_w2, w_branch_a, w_branch_b, w_out, ffn_norm_g, w_gate, w_up, w_down, final_norm_g):
    b, s, d = x.shape
    depth = w_in.shape[0]
    cos, sin = _rope_tables(positions)
    h = x.reshape(b * s, d)
    gates_off = w_in.shape[2] - 2 * d
    for layer in range(depth):
        p = _project(h, attn_norm_g[layer], cos, sin, w_in[layer, :, :gates_off])
        kcmp = _compress(p["kc"], cmp_pe_k[layer], cmp_k_w1[layer], cmp_k_w2[layer], b, s)
        vcmp = _compress(p["vc"], cmp_pe_v[layer], cmp_v_w1[layer], cmp_v_w2[layer], b, s)
        ya = _diff_attention(p["dqT"], p["dk"], p["dvT"], diff_lambda[layer], diff_subln_g[layer], b, s, layer)
        yb = _nsa_attention(p["nqT"], p["ng"], kcmp, vcmp, p["ksd"], p["vsT"], p["kwd"], p["vwT"], b, s)
        h = _merge(h, attn_norm_g[layer], ya.reshape(b * s, -1), yb.reshape(b * s, -1),
                   w_in[layer, :, gates_off:], w_branch_a[layer], w_branch_b[layer], w_out[layer])
        h = _ffn(h, ffn_norm_g[layer], final_norm_g, w_gate[layer], w_up[layer], w_down[layer], layer == depth - 1)
    return h.reshape(b, s, d)
```

```python
import functools
import math
from typing import Callable, NamedTuple, Sequence

import jax
import jax.numpy as jnp
import numpy as np
from jax import lax
from jax.experimental import pallas as pl
from jax.experimental.pallas import tpu as pltpu

HEAD_DIM = 64
ROPE_THETA = 10000.0
NORM_EPS = 1e-6
NEG_INF = -1e30
FORCE_SCORE = 1e9

DIFF_HEADS = 4
DIFF_VDIM = 2 * HEAD_DIM
NSA_HEADS = 8
NSA_KV_GROUPS = 2
NSA_HPG = NSA_HEADS // NSA_KV_GROUPS
CMP_LEN = 32
CMP_STRIDE = 16
SEL_LEN = 64
SEL_TOPK = 16
WINDOW = 512

LANES = 128
VMEM_LIMIT = 56 * 1024 * 1024
KEY_TILE = 256

BF16 = jnp.bfloat16
F32 = jnp.float32


def _lambda_init(layer):
    return 0.8 - 0.6 * math.exp(-0.3 * layer)


def _params(n_axes):
    return pltpu.CompilerParams(dimension_semantics=("arbitrary",) * n_axes, vmem_limit_bytes=VMEM_LIMIT)


def _const_spec(shape):
    nd = len(shape)
    return pl.BlockSpec(shape, lambda *_: (0,) * nd)


def _rope_table_kernel(pos_ref, invf_ref, cos_ref, sin_ref):
    ang = pos_ref[...].astype(F32) * invf_ref[...]
    cos_ref[...] = jnp.cos(ang)
    sin_ref[...] = jnp.sin(ang)


def _rope_tables(positions):
    half = HEAD_DIM // 2
    per_row = LANES // half
    t = positions.size
    rows = t // per_row
    pos_e = jnp.repeat(positions.reshape(rows, per_row), half, axis=1)
    inv_freq = 1.0 / (ROPE_THETA ** (jnp.arange(0, HEAD_DIM, 2, dtype=F32) / HEAD_DIM))
    invf = jnp.tile(inv_freq, per_row)[None, :]
    tr = min(rows, 1024)
    cos, sin = pl.pallas_call(
        _rope_table_kernel,
        grid=(rows // tr,),
        in_specs=[pl.BlockSpec((tr, LANES), lambda i: (i, 0)), _const_spec((1, LANES))],
        out_specs=[pl.BlockSpec((tr, LANES), lambda i: (i, 0))] * 2,
        out_shape=[jax.ShapeDtypeStruct((rows, LANES), F32)] * 2,
        compiler_params=_params(1),
        name="rope_tables",
    )(pos_e, invf)
    return cos.reshape(t, half), sin.reshape(t, half)


_ROW_SEGS = (("dk", 512, BF16, True), ("ksd", 256, BF16, True), ("kwd", 256, BF16, True), ("kc", 128, F32, True),
             ("vc", 128, F32, False), ("ng", 128, F32, False))
Q_SCALE = HEAD_DIM ** -0.5 * math.log2(math.e)
KEY_PAIR = 2 * KEY_TILE
ONES_ROWS = 16
_COL_SEGS = (("dqT", 512, True, None, None), ("nqT", 512, True, None, None),
             ("dvT", 512, False, KEY_PAIR, DIFF_VDIM), ("vsT", 128, False, KEY_PAIR, HEAD_DIM), ("vwT", 128, False, KEY_TILE, HEAD_DIM))
_STRIDED_SEGS = ("kc", "vc")


def _proj_kernel(x_ref, g_ref, cos_ref, sin_ref, wr_ref, wt_ref, *out_refs):
    x = x_ref[...]
    ms = jnp.mean(x * x, axis=-1, keepdims=True)
    u = (x * lax.rsqrt(ms + NORM_EPS) * g_ref[...]).astype(BF16)
    half = HEAD_DIM // 2
    reps = LANES // half

    def rope_fn(c, s, axis):
        idx = lax.broadcasted_iota(jnp.int32, c.shape, axis)
        upper = (idx % HEAD_DIM) >= half
        s_up = jnp.where(upper, s, 0.0)
        s_lo = jnp.where(upper, 0.0, -s)
        return lambda y: y * c + pltpu.roll(y, half, axis) * s_up + pltpu.roll(y, LANES - half, axis) * s_lo

    c = jnp.concatenate([cos_ref[...]] * reps, axis=1)
    s = jnp.concatenate([sin_ref[...]] * reps, axis=1)
    rope_rows = rope_fn(c, s, 1)
    rope_cols = rope_fn(c.T, s.T, 0)

    *refs, stage_ref = out_refs
    tm = x.shape[0]
    off = 0
    for name, width, _, roped in _ROW_SEGS:
        o_ref = refs.pop(0)
        for c0 in range(0, width, 2 * LANES):
            cw = min(2 * LANES, width - c0)
            y = jnp.dot(u, wr_ref[:, off + c0:off + c0 + cw], preferred_element_type=F32)
            for l0 in range(0, cw, LANES):
                yl = y[:, l0:l0 + LANES]
                if roped:
                    yl = rope_rows(yl)
                if name == "ng":
                    yl = jax.nn.sigmoid(yl)
                if name in _STRIDED_SEGS:
                    stage_ref[...] = yl
                    for l in range(CMP_STRIDE):
                        o_ref[:, l * LANES:(l + 1) * LANES] = stage_ref[pl.ds(l, tm // CMP_STRIDE, stride=CMP_STRIDE), :]
                else:
                    o_ref[:, c0 + l0:c0 + l0 + LANES] = yl.astype(o_ref.dtype)
        off += width

    off = 0
    for name, rows, roped, tile, group in _COL_SEGS:
        o_ref = refs.pop(0)
        for r0 in range(0, rows, 2 * LANES):
            rw = min(2 * LANES, rows - r0)
            yt = lax.dot_general(wt_ref[off + r0:off + r0 + rw, :], u, (((1,), (1,)), ((), ())), preferred_element_type=F32)
            for s0 in range(0, rw, LANES):
                ys = yt[s0:s0 + LANES]
                if roped:
                    ys = rope_cols(ys) * Q_SCALE
                ys = ys.astype(o_ref.dtype)
                if not tile:
                    o_ref[r0 + s0:r0 + s0 + LANES, :] = ys
                    continue
                for g0 in range(0, LANES, group):
                    dst = (r0 + s0 + g0) // group * (group + ONES_ROWS)
                    for j in range(tm // tile):
                        o_ref[j, dst:dst + group, :] = ys[g0:g0 + group, j * tile:(j + 1) * tile]
                        o_ref[j, dst + group:dst + group + ONES_ROWS, :] = jnp.ones((ONES_ROWS, tile), o_ref.dtype)
        off += rows


def _dup_groups(w):
    d = w.shape[0]
    wg = w.reshape(d, NSA_KV_GROUPS, 1, HEAD_DIM)
    return jnp.broadcast_to(wg, (d, NSA_KV_GROUPS, 2, HEAD_DIM)).reshape(d, NSA_KV_GROUPS * 2 * HEAD_DIM)


def _project(x2, g, cos, sin, w_in):
    t, d = x2.shape
    qk = DIFF_HEADS * 2 * HEAD_DIM
    kv = NSA_KV_GROUPS * HEAD_DIM
    splits = (qk, qk, DIFF_HEADS * DIFF_VDIM, NSA_HEADS * HEAD_DIM, kv, kv, kv, kv, kv, kv, 3 * NSA_HEADS)
    offs = np.cumsum((0,) + splits)
    dq, dk, dv, nq, kc, vc, ks, vs, kw, vw, ng = (w_in[:, offs[i]:offs[i + 1]] for i in range(len(splits)))
    ng = jnp.pad(ng, ((0, 0), (0, LANES - ng.shape[1])))
    wr = jnp.concatenate([dk, _dup_groups(ks), _dup_groups(kw), kc, vc, ng], axis=1)
    wt = jnp.concatenate([dq, nq, dv, vs, vw], axis=1).T
    tm = min(t, 512)
    assert tm % KEY_TILE == 0
    half = HEAD_DIM // 2
    out_specs, out_shape = [], []
    for name, w, dt, _ in _ROW_SEGS:
        fold = CMP_STRIDE if name in _STRIDED_SEGS else 1
        out_specs.append(pl.BlockSpec((tm // fold, w * fold), lambda i: (i, 0)))
        out_shape.append(jax.ShapeDtypeStruct((t // fold, w * fold), dt))
    for _, rows, _, tile, group in _COL_SEGS:
        if tile:
            assert tm % tile == 0
            padded = rows // group * (group + ONES_ROWS)
            out_specs.append(pl.BlockSpec((tm // tile, padded, tile), lambda i: (i, 0, 0)))
            out_shape.append(jax.ShapeDtypeStruct((t // tile, padded, tile), BF16))
        else:
            out_specs.append(pl.BlockSpec((rows, tm), lambda i: (0, i)))
            out_shape.append(jax.ShapeDtypeStruct((rows, t), BF16))
    outs = pl.pallas_call(
        _proj_kernel,
        grid=(t // tm,),
        in_specs=[
            pl.BlockSpec((tm, d), lambda i: (i, 0)),
            _const_spec((1, d)),
            pl.BlockSpec((tm, half), lambda i: (i, 0)),
            pl.BlockSpec((tm, half), lambda i: (i, 0)),
            _const_spec(wr.shape),
            _const_spec(wt.shape),
        ],
        out_specs=out_specs,
        out_shape=out_shape,
        scratch_shapes=[pltpu.VMEM((tm, LANES), F32)],
        compiler_params=_params(1),
        name="in_proj",
    )(x2, g[None, :], cos, sin, wr, wt)
    return dict(zip([n for n, *_ in _ROW_SEGS] + [n for n, *_ in _COL_SEGS], outs))


def _compress_kernel(x_ref, pea_ref, peb_ref, w1a_ref, w1b_ref, w2_ref, o_ref):
    x = x_ref[...]
    a = jnp.dot((x + pea_ref[...]).astype(BF16), w1a_ref[...], preferred_element_type=F32)
    b = jnp.dot((x + peb_ref[...]).astype(BF16), w1b_ref[...], preferred_element_type=F32)
    ncp = x.shape[0]
    h = a + pltpu.roll(b, ncp - 1, 0)
    h = h * jax.nn.sigmoid(h)
    o_ref[...] = jnp.dot(h.astype(BF16), w2_ref[...], preferred_element_type=F32).astype(o_ref.dtype)


def _compress(xc, pe, w1, w2, b, s):
    g_, dh = NSA_KV_GROUPS, HEAD_DIM
    ncp = s // CMP_STRIDE
    hid = w1.shape[1]
    halves = CMP_LEN // CMP_STRIDE
    assert halves == 2
    xr = xc.reshape(b, ncp, CMP_STRIDE * g_ * dh)
    eye = jnp.eye(g_, dtype=F32)
    pe_r = pe.reshape(halves, CMP_STRIDE, 1, dh)
    pe_t = jnp.broadcast_to(pe_r, (halves, CMP_STRIDE, g_, dh)).reshape(halves, 1, CMP_STRIDE * g_ * dh)
    w1r = w1.reshape(halves, CMP_STRIDE, dh, hid)
    w1bd = jnp.einsum("hldj,pg->hlpdgj", w1r, eye).reshape(halves, CMP_STRIDE * g_ * dh, g_ * hid).astype(BF16)
    w2bd = jnp.einsum("jd,pg,r->pjgrd", w2, eye, jnp.ones((2,), F32)).reshape(g_ * hid, g_ * 2 * dh).astype(BF16)
    kdim = CMP_STRIDE * g_ * dh
    return pl.pallas_call(
        _compress_kernel,
        grid=(b,),
        in_specs=[
            pl.BlockSpec((None, ncp, kdim), lambda i: (i, 0, 0)),
            _const_spec((1, kdim)),
            _const_spec((1, kdim)),
            _const_spec((kdim, g_ * hid)),
            _const_spec((kdim, g_ * hid)),
            _const_spec((g_ * hid, g_ * 2 * dh)),
        ],
        out_specs=pl.BlockSpec((None, ncp, g_ * 2 * dh), lambda i: (i, 0, 0)),
        out_shape=jax.ShapeDtypeStruct((b, ncp, g_ * 2 * dh), BF16),
        compiler_params=_params(1),
        name="compress",
    )(xr, pe_t[0], pe_t[1], w1bd[0], w1bd[1], w2bd)


def _split_heads(qt):
    row = lax.broadcasted_iota(jnp.int32, qt.shape, 0)
    zero = jnp.zeros_like(qt)
    return jnp.concatenate([jnp.where(row < HEAD_DIM, qt, zero), jnp.where(row >= HEAD_DIM, qt, zero)], axis=1)


def _flash_scratch(tk, m_cols, dv):
    row = pltpu.VMEM((1, m_cols), F32)
    sc = pltpu.VMEM((tk, m_cols), F32)
    return [sc, sc, row, row, pltpu.VMEM((2 * tk, m_cols), BF16), row, row, pltpu.VMEM((dv + ONES_ROWS, m_cols), F32)]


class _Stream(NamedTuple):
    qst: jax.Array
    load_k: Callable
    load_vt: Callable
    scratch: Sequence
    mask_fn: Callable
    inner_fn: Callable


def _flash_tiles(streams, tk, n_k, hi):
    def scores(st, t, slot, masked):
        s_ref, t_ref = st.scratch[slot], st.scratch[2 + slot]
        k = st.load_k(pl.multiple_of(jnp.clip(t, 0, n_k - 1) * tk, tk))
        s = (st.mask_fn if masked else st.inner_fn)(t, jnp.dot(k, st.qst, preferred_element_type=F32))
        t_ref[...] = jnp.max(s, axis=0, keepdims=True)
        s_ref[...] = s

    def probs(st):
        sa_ref, sb_ref, ta_ref, tb_ref, p_ref, al_ref, m_ref, _ = st.scratch
        m_old = m_ref[...]
        m_new = jnp.maximum(m_old, jnp.maximum(ta_ref[...], tb_ref[...]))
        p_ref[:tk, :] = jnp.exp2(sa_ref[...] - m_new).astype(BF16)
        p_ref[tk:, :] = jnp.exp2(sb_ref[...] - m_new).astype(BF16)
        al_ref[...] = jnp.exp2(m_old - m_new)
        m_ref[...] = m_new

    def values(st, pair):
        p_ref, al_ref, acc_ref = st.scratch[4], st.scratch[5], st.scratch[7]
        vt = st.load_vt(jnp.clip(pair, 0, n_k // 2 - 1))
        acc_ref[...] = al_ref[...] * acc_ref[...] + jnp.dot(vt, p_ref[...], preferred_element_type=F32)

    def each(fn, *args):
        for st in streams:
            fn(st, *args)

    def init(st):
        _, _, _, _, p_ref, al_ref, m_ref, acc_ref = st.scratch
        m_ref[...] = jnp.full(m_ref.shape, NEG_INF, F32)
        acc_ref[...] = jnp.zeros(acc_ref.shape, F32)
        p_ref[...] = jnp.zeros(p_ref.shape, BF16)
        al_ref[...] = jnp.ones(al_ref.shape, F32)

    n_pairs = hi // 2 + 1

    def pair_at(r):
        return jnp.where(r == 0, n_pairs - 1, r - 1)

    each(init)
    each(scores, 2 * pair_at(0), 0, True)
    each(scores, 2 * pair_at(0) + 1, 1, True)

    def body(r, carry):
        each(values, pair_at(r - 1))
        each(probs)
        each(scores, 2 * r, 0, False)
        each(scores, 2 * r + 1, 1, False)
        return carry

    lax.fori_loop(0, n_pairs - 1, body, 0)
    each(values, pair_at(n_pairs - 2))
    each(probs)
    each(values, pair_at(n_pairs - 1))
    out = []
    for st in streams:
        acc = st.scratch[7][...]
        dv = acc.shape[0] - ONES_ROWS
        out.append((acc[dv:dv + 1], acc[:dv]))
    return out


def _diff_kernel(lam_ref, g_ref, qt_ref, k_ref, vt_ref, o_ref, *scratch, tq, tk, heads, lam_init):
    qi = pl.program_id(2)
    m_cols = 2 * tq
    qpos = qi * tq + lax.broadcasted_iota(jnp.int32, (1, m_cols), 1) % tq
    krel = lax.broadcasted_iota(jnp.int32, (tk, m_cols), 0)
    hw = 2 * HEAD_DIM
    vrows = DIFF_VDIM + ONES_ROWS
    n_sc = len(scratch) // heads

    def causal(t, s):
        return jnp.where(krel <= qpos - t * tk, s, NEG_INF)

    streams = [
        _Stream(
            qst=_split_heads(qt_ref[h * hw:(h + 1) * hw, :]),
            load_k=lambda row, h=h: k_ref[pl.ds(row, tk), h * hw:(h + 1) * hw],
            load_vt=lambda pair, h=h: vt_ref[pair, h * vrows:(h + 1) * vrows, :],
            scratch=scratch[h * n_sc:(h + 1) * n_sc],
            mask_fn=causal,
            inner_fn=lambda t, s: s,
        )
        for h in range(heads)
    ]
    results = _flash_tiles(streams, tk, k_ref.shape[0] // tk, (qi * tq + tq - 1) // tk)
    lf = lam_ref[...]
    lam = (jnp.exp(jnp.sum(lf[0:1] * lf[1:2], axis=1, keepdims=True))
           - jnp.exp(jnp.sum(lf[2:3] * lf[3:4], axis=1, keepdims=True)) + lam_init)
    for h, (l, acc) in enumerate(results):
        o = acc * (1.0 / l)
        y = o[:, :tq] - lam * o[:, tq:]
        y = y * lax.rsqrt(jnp.mean(y * y, axis=0, keepdims=True) + NORM_EPS)
        y = y.T * g_ref[...] * (1.0 - lam_init)
        o_ref[:, h * DIFF_VDIM:(h + 1) * DIFF_VDIM] = y.astype(o_ref.dtype)


def _diff_attention(dqt, dk, dvt, diff_lambda, subln_g, b, s, layer):
    tk = KEY_TILE
    tq = 2 * tk
    heads = DIFF_HEADS
    hw = heads * 2 * HEAD_DIM
    vrows = heads * (DIFF_VDIM + ONES_ROWS)
    nq = s // tq
    return pl.pallas_call(
        functools.partial(_diff_kernel, tq=tq, tk=tk, heads=heads, lam_init=_lambda_init(layer)),
        grid=(b, DIFF_HEADS // heads, nq),
        in_specs=[
            _const_spec(diff_lambda.shape),
            _const_spec((1, DIFF_VDIM)),
            pl.BlockSpec((hw, tq), lambda bi, h, i: (h, bi * nq + i)),
            pl.BlockSpec((None, s, hw), lambda bi, h, i: (bi, 0, h)),
            pl.BlockSpec((None, s // KEY_PAIR, vrows, KEY_PAIR), lambda bi, h, i: (bi, 0, h, 0)),
        ],
        out_specs=pl.BlockSpec((None, tq, heads * DIFF_VDIM), lambda bi, h, i: (bi, i, h)),
        out_shape=jax.ShapeDtypeStruct((b, s, DIFF_HEADS * DIFF_VDIM), BF16),
        scratch_shapes=_flash_scratch(tk, 2 * tq, DIFF_VDIM) * heads,
        compiler_params=_params(3),
        name="diff_attn",
    )(diff_lambda, subln_g[None, :], dqt, dk.reshape(b, s, -1), dvt.reshape(b, s // KEY_PAIR, -1, KEY_PAIR))


def _nsa_kernel(ovl_ref, qt_ref, gate_ref, kc_ref, vc_ref, ks_ref, vst_ref, kw_ref, vwt_ref, o_ref, imp_ref, selb_ref, cnt_ref, *scratch, tq, tk):
    qi = pl.program_id(1)
    q0 = qi * tq
    m_cols = NSA_HPG * tq
    qpos_t = q0 + lax.broadcasted_iota(jnp.int32, (1, tq), 1)
    qpos = jnp.concatenate([qpos_t] * NSA_HPG, axis=1)
    dh = HEAD_DIM
    groups = range(NSA_KV_GROUPS)
    gw = NSA_HPG * dh
    kwid = 2 * dh
    vrows = dh + ONES_ROWS
    n_sc = len(scratch) // NSA_KV_GROUPS
    ncp = kc_ref.shape[0]
    nsb = imp_ref.shape[1]
    topk = min(SEL_TOPK, nsb)

    qsts = []
    last_cmp = lax.shift_right_arithmetic(qpos - (CMP_LEN - 1), int(math.log2(CMP_STRIDE)))
    cmask = lax.broadcasted_iota(jnp.int32, (ncp, m_cols), 0) <= last_cmp
    blk = lax.broadcasted_iota(jnp.int32, (nsb, tq), 0)
    qblk = qpos_t // SEL_LEN
    forced = (blk == 0) | (blk == qblk) | (blk == qblk - 1)
    for g in groups:
        qt = qt_ref[g * gw:(g + 1) * gw, :]
        qsts.append(jnp.concatenate([_split_heads(qt[:LANES]), _split_heads(qt[LANES:])], axis=1))
    scs = [jnp.where(cmask, jnp.dot(kc_ref[:, g * kwid:(g + 1) * kwid], qsts[g], preferred_element_type=F32), NEG_INF)
           for g in groups]
    pcs = [jnp.exp2(sc - jnp.max(sc, axis=0, keepdims=True)) for sc in scs]
    pcs = [pc * jnp.where(qpos >= CMP_LEN - 1, 1.0 / jnp.sum(pc, axis=0, keepdims=True), 0.0) for pc in pcs]
    o_cs = [lax.dot_general(vc_ref[:, g * kwid:(g + 1) * kwid], pcs[g].astype(BF16), (((0,), (0,)), ((), ())),
                            preferred_element_type=F32)[:dh] for g in groups]
    for g in groups:
        pc_sum = functools.reduce(jnp.add, [pcs[g][:, h * tq:(h + 1) * tq] for h in range(NSA_HPG)])
        imp = jnp.dot(ovl_ref[...], pc_sum, preferred_element_type=F32, precision=lax.Precision.HIGHEST)
        imp = jnp.where(forced, FORCE_SCORE, imp)
        imp_ref[g] = jnp.where(blk <= qblk, imp, NEG_INF)
        cnt_ref[g] = jnp.zeros((nsb, tq), jnp.int32)

    sub = 8
    last_blk = (q0 + tq - 1) // SEL_LEN
    for g0 in range(0, nsb, sub):
        @pl.when(g0 <= last_blk)
        def _(g0=g0):
            for g in groups:
                cnts = [cnt_ref[g, b0:b0 + sub, :] for b0 in range(0, nsb, sub)]
                for j in range(g0, g0 + sub):
                    row = jnp.broadcast_to(imp_ref[g, j:j + 1, :], (sub, tq))
                    for i, b0 in enumerate(range(0, nsb, sub)):
                        cur = imp_ref[g, b0:b0 + sub, :]
                        if b0 > j:
                            beats = row >= cur
                        elif b0 + sub - 1 < j:
                            beats = row > cur
                        else:
                            beats = (row > cur) | ((row == cur) & (lax.broadcasted_iota(jnp.int32, (sub, tq), 0) > j - b0))
                        cnts[i] = cnts[i] + jnp.where(beats, 1, 0)
                for i, b0 in enumerate(range(0, nsb, sub)):
                    cnt_ref[g, b0:b0 + sub, :] = cnts[i]
    for g in groups:
        selb_ref[g] = jnp.where(cnt_ref[g] < topk, 0.0, NEG_INF)

    krel = lax.broadcasted_iota(jnp.int32, (tk, m_cols), 0)
    bpt = tk // SEL_LEN
    last = (q0 + tq - 1) // tk

    def biased(g):
        def fn(t, s):
            first_blk = jnp.clip(t, 0, nsb // bpt - 1) * bpt
            rows = []
            for r in range(bpt):
                brow = selb_ref[g, pl.ds(first_blk + r, 1), :]
                brow = jnp.concatenate([brow] * NSA_HPG, axis=1)
                rows.append(jnp.broadcast_to(brow, (SEL_LEN, m_cols)))
            return s + jnp.concatenate(rows, axis=0)
        return fn

    def selected(g):
        return lambda t, s: jnp.where(krel <= qpos - t * tk, biased(g)(t, s), NEG_INF)

    streams = [
        _Stream(
            qst=qsts[g],
            load_k=lambda row, g=g: ks_ref[pl.ds(row, tk), g * kwid:(g + 1) * kwid],
            load_vt=lambda pair, g=g: vst_ref[pair, g * vrows:(g + 1) * vrows, :],
            scratch=scratch[g * n_sc:(g + 1) * n_sc],
            mask_fn=selected(g),
            inner_fn=biased(g),
        )
        for g in groups
    ]
    sel_out = _flash_tiles(streams, tk, ks_ref.shape[0] // tk, last)

    gates = gate_ref[...].T
    n_back = -(-(WINDOW - 1) // tk)
    win_tiles = [qi * (tq // tk) - n_back + r for r in range(n_back + tq // tk)]
    o_ss = [acc_s * (1.0 / l_s) for l_s, acc_s in sel_out]

    s_ws = [[] for _ in groups]
    for r, t in enumerate(win_tiles):
        d_min = (n_back - r) * tk - (tk - 1)
        d_max = (n_back - r) * tk + (tq - 1)
        for g in groups:
            k = kw_ref[pl.ds(pl.multiple_of(jnp.maximum(t, 0) * tk, tk), tk), g * kwid:(g + 1) * kwid]
            s = jnp.dot(k, qsts[g], preferred_element_type=F32)
            if d_min < 0:
                s = jnp.where(krel <= qpos - t * tk, s, NEG_INF)
            if d_max >= WINDOW:
                s = jnp.where(krel > qpos - t * tk - WINDOW, s, NEG_INF)
            if r < n_back:
                s = s + jnp.where(t >= 0, 0.0, NEG_INF)
            s_ws[g].append(s)
    m_ws = [functools.reduce(jnp.maximum, [jnp.max(s, axis=0, keepdims=True) for s in s_w]) for s_w in s_ws]
    acc_ws = [jnp.zeros((vrows, m_cols), F32) for _ in groups]
    for r, t in enumerate(win_tiles):
        for g in groups:
            p = jnp.exp2(s_ws[g][r] - m_ws[g]).astype(BF16)
            acc_ws[g] = acc_ws[g] + jnp.dot(vwt_ref[jnp.maximum(t, 0), g * vrows:(g + 1) * vrows, :], p, preferred_element_type=F32)
    o_ws = [acc_w[:dh] * (1.0 / acc_w[dh:dh + 1]) for acc_w in acc_ws]

    for g in groups:
        outs = []
        for h in range(NSA_HPG):
            sl = slice(h * tq, (h + 1) * tq)
            gate = lambda branch, h=h: gates[branch * NSA_HEADS + g * NSA_HPG + h:branch * NSA_HEADS + g * NSA_HPG + h + 1, :]
            outs.append(gate(0) * o_cs[g][:, sl] + gate(1) * o_ss[g][:, sl] + gate(2) * o_ws[g][:, sl])
        y = jnp.concatenate(outs, axis=0)
        o_ref[:, g * gw:(g + 1) * gw] = y.T.astype(o_ref.dtype)


def _overlap_matrix(s):
    nc = s // CMP_STRIDE
    nsb = s // SEL_LEN
    ci = np.arange(nc)[None, :] * CMP_STRIDE
    sj = np.arange(nsb)[:, None] * SEL_LEN
    ovl = ((ci < sj + SEL_LEN) & (ci + CMP_LEN > sj)).astype(np.float32)
    ovl[:, (s - CMP_LEN) // CMP_STRIDE + 1:] = 0.0
    return jnp.asarray(ovl)


def _nsa_attention(nqt, ng, kcmp, vcmp, ksd, vst, kwd, vwt, b, s):
    tk = tq = KEY_TILE
    nq = s // tq
    g_ = NSA_KV_GROUPS
    qw = NSA_HEADS * HEAD_DIM
    kw_ = g_ * 2 * HEAD_DIM
    vrows = g_ * (HEAD_DIM + ONES_ROWS)
    ncp = s // CMP_STRIDE
    nsb = s // SEL_LEN
    seq3 = lambda a: a.reshape(b, s, a.shape[-1])
    vt4 = lambda a: a.reshape(b, -1, a.shape[-2], a.shape[-1])
    k_spec = pl.BlockSpec((None, s, kw_), lambda bi, i: (bi, 0, 0))
    vt_spec = lambda tile: pl.BlockSpec((None, s // tile, vrows, tile), lambda bi, i: (bi, 0, 0, 0))
    cmp_spec = pl.BlockSpec((None, ncp, kw_), lambda bi, i: (bi, 0, 0))
    per_group = lambda dt: pltpu.VMEM((g_, nsb, tq), dt)
    return pl.pallas_call(
        functools.partial(_nsa_kernel, tq=tq, tk=tk),
        grid=(b, nq),
        in_specs=[
            _const_spec((nsb, ncp)),
            pl.BlockSpec((qw, tq), lambda bi, i: (0, bi * nq + i)),
            pl.BlockSpec((None, tq, LANES), lambda bi, i: (bi, i, 0)),
            cmp_spec, cmp_spec, k_spec, vt_spec(KEY_PAIR), k_spec, vt_spec(KEY_TILE),
        ],
        out_specs=pl.BlockSpec((None, tq, qw), lambda bi, i: (bi, i, 0)),
        out_shape=jax.ShapeDtypeStruct((b, s, qw), BF16),
        scratch_shapes=[per_group(F32), per_group(F32), per_group(jnp.int32)] + _flash_scratch(tk, NSA_HPG * tq, HEAD_DIM) * g_,
        compiler_params=_params(2),
        name="nsa_attn",
    )(_overlap_matrix(s), nqt, seq3(ng), kcmp, vcmp, seq3(ksd), vt4(vst), seq3(kwd), vt4(vwt))


def _merge_kernel(x_ref, g_ref, ya_ref, yb_ref, wmg_ref, wa_ref, wb_ref, wo_ref, o_ref):
    x = x_ref[...]
    d = x.shape[1]
    ms = jnp.mean(x * x, axis=-1, keepdims=True)
    u = (x * lax.rsqrt(ms + NORM_EPS) * g_ref[...]).astype(BF16)
    ya = jnp.dot(ya_ref[...], wa_ref[...], preferred_element_type=F32)
    yb = jnp.dot(yb_ref[...], wb_ref[...], preferred_element_type=F32)
    mg_a = jax.nn.sigmoid(jnp.dot(u, wmg_ref[:, :d], preferred_element_type=F32))
    mg_b = jax.nn.sigmoid(jnp.dot(u, wmg_ref[:, d:], preferred_element_type=F32))
    merged = (mg_a * ya + mg_b * yb).astype(BF16)
    o_ref[...] = x + jnp.dot(merged, wo_ref[...], preferred_element_type=F32)


def _merge(x2, g, ya, yb, w_mg, w_a, w_b, w_o):
    t, d = x2.shape
    tm = min(t, 512)
    ws = [w.astype(BF16) for w in (w_mg, w_a, w_b, w_o)]
    return pl.pallas_call(
        _merge_kernel,
        grid=(t // tm,),
        in_specs=[
            pl.BlockSpec((tm, d), lambda i: (i, 0)),
            _const_spec((1, d)),
            pl.BlockSpec((tm, ya.shape[1]), lambda i: (i, 0)),
            pl.BlockSpec((tm, yb.shape[1]), lambda i: (i, 0)),
        ] + [_const_spec(w.shape) for w in ws],
        out_specs=pl.BlockSpec((tm, d), lambda i: (i, 0)),
        out_shape=jax.ShapeDtypeStruct((t, d), F32),
        compiler_params=_params(1),
        name="merge_out_proj",
    )(x2, g[None, :], ya, yb, *ws)


def _ffn_kernel(h_ref, g_ref, gf_ref, wg_ref, wu_ref, wd_ref, o_ref, *, chunk, final_norm):
    h = h_ref[...]
    ms = jnp.mean(h * h, axis=-1, keepdims=True)
    u = (h * lax.rsqrt(ms + NORM_EPS) * g_ref[...]).astype(BF16)
    d_ff = wg_ref.shape[1]
    acc = h
    for c0 in range(0, d_ff, chunk):
        a = jnp.dot(u, wg_ref[:, c0:c0 + chunk], preferred_element_type=F32)
        up = jnp.dot(u, wu_ref[:, c0:c0 + chunk], preferred_element_type=F32)
        act = (a * jax.nn.sigmoid(a) * up).astype(BF16)
        acc = acc + jnp.dot(act, wd_ref[c0:c0 + chunk, :], preferred_element_type=F32)
    if final_norm:
        ms = jnp.mean(acc * acc, axis=-1, keepdims=True)
        acc = acc * lax.rsqrt(ms + NORM_EPS) * gf_ref[...]
    o_ref[...] = acc


def _ffn(h2, g, g_final, w_gate, w_up, w_down, final_norm):
    t, d = h2.shape
    tm = min(t, 512)
    d_ff = w_gate.shape[1]
    chunk = 256
    assert d_ff % chunk == 0
    ws = [w.astype(BF16) for w in (w_gate, w_up, w_down)]
    return pl.pallas_call(
        functools.partial(_ffn_kernel, chunk=chunk, final_norm=final_norm),
        grid=(t // tm,),
        in_specs=[pl.BlockSpec((tm, d), lambda i: (i, 0)), _const_spec((1, d)), _const_spec((1, d))]
        + [_const_spec(w.shape) for w in ws],
        out_specs=pl.BlockSpec((tm, d), lambda i: (i, 0)),
        out_shape=jax.ShapeDtypeStruct((t, d), F32),
        compiler_params=_params(1),
        name="swiglu_ffn",
    )(h2, g[None, :], g_final[None, :], *ws)


def kernel(x, positions, attn_norm_g, w_in, diff_lambda, diff_subln_g, cmp_pe_k, cmp_pe_v, cmp_k_w1, cmp_k_w2, cmp_v_w1, cmp_v_w2, w_branch_a, w_branch_b, w_out, ffn_norm_g, w_gate, w_up, w_down, final_norm_g):
    b, s, d = x.shape
    depth = w_in.shape[0]
    cos, sin = _rope_tables(positions)
    h = x.reshape(b * s, d)
    gates_off = w_in.shape[2] - 2 * d
    for layer in range(depth):
        w_in_b = w_in[layer].astype(BF16)
        p = _project(h, attn_norm_g[layer], cos, sin, w_in_b[:, :gates_off])
        kcmp = _compress(p["kc"], cmp_pe_k[layer], cmp_k_w1[layer], cmp_k_w2[layer], b, s)
        vcmp = _compress(p["vc"], cmp_pe_v[layer], cmp_v_w1[layer], cmp_v_w2[layer], b, s)
        ya = _diff_attention(p["dqT"], p["dk"], p["dvT"], diff_lambda[layer], diff_subln_g[layer], b, s, layer)
        yb = _nsa_attention(p["nqT"], p["ng"], kcmp, vcmp, p["ksd"], p["vsT"], p["kwd"], p["vwT"], b, s)
        h = _merge(h, attn_norm_g[layer], ya.reshape(b * s, -1), yb.reshape(b * s, -1),
                   w_in_b[:, gates_off:], w_branch_a[layer], w_branch_b[layer], w_out[layer])
        h = _ffn(h, ffn_norm_g[layer], final_norm_g, w_gate[layer], w_up[layer], w_down[layer], layer == depth - 1)
    return h.reshape(b, s, d)
```

```python
import functools
import math
from typing import Callable, NamedTuple, Sequence

import jax
import jax.numpy as jnp
import numpy as np
from jax import lax
from jax.experimental import pallas as pl
from jax.experimental.pallas import tpu as pltpu

HEAD_DIM = 64
ROPE_THETA = 10000.0
NORM_EPS = 1e-6
NEG_INF = -1e30
FORCE_SCORE = 1e9

DIFF_HEADS = 4
DIFF_VDIM = 2 * HEAD_DIM
NSA_HEADS = 8
NSA_KV_GROUPS = 2
NSA_HPG = NSA_HEADS // NSA_KV_GROUPS
CMP_LEN = 32
CMP_STRIDE = 16
SEL_LEN = 64
SEL_TOPK = 16
WINDOW = 512

LANES = 128
VMEM_LIMIT = 56 * 1024 * 1024
KEY_TILE = 256

BF16 = jnp.bfloat16
F32 = jnp.float32


def _lambda_init(layer):
    return 0.8 - 0.6 * math.exp(-0.3 * layer)


def _params(n_axes):
    return pltpu.CompilerParams(dimension_semantics=("arbitrary",) * n_axes, vmem_limit_bytes=VMEM_LIMIT)


def _const_spec(shape):
    nd = len(shape)
    return pl.BlockSpec(shape, lambda *_: (0,) * nd)


def _rope_table_kernel(pos_ref, invf_ref, cos_ref, sin_ref):
    ang = pos_ref[...].astype(F32) * invf_ref[...]
    cos_ref[...] = jnp.cos(ang)
    sin_ref[...] = jnp.sin(ang)


def _rope_tables(positions):
    half = HEAD_DIM // 2
    per_row = LANES // half
    t = positions.size
    rows = t // per_row
    pos_e = jnp.repeat(positions.reshape(rows, per_row), half, axis=1)
    inv_freq = 1.0 / (ROPE_THETA ** (jnp.arange(0, HEAD_DIM, 2, dtype=F32) / HEAD_DIM))
    invf = jnp.tile(inv_freq, per_row)[None, :]
    tr = min(rows, 1024)
    cos, sin = pl.pallas_call(
        _rope_table_kernel,
        grid=(rows // tr,),
        in_specs=[pl.BlockSpec((tr, LANES), lambda i: (i, 0)), _const_spec((1, LANES))],
        out_specs=[pl.BlockSpec((tr, LANES), lambda i: (i, 0))] * 2,
        out_shape=[jax.ShapeDtypeStruct((rows, LANES), F32)] * 2,
        compiler_params=_params(1),
        name="rope_tables",
    )(pos_e, invf)
    return cos.reshape(t, half), sin.reshape(t, half)


_ROW_SEGS = (("dk", 512, BF16, True), ("ksd", 256, BF16, True), ("kwd", 256, BF16, True), ("kc", 128, F32, True),
             ("vc", 128, F32, False), ("ng", 128, F32, False))
Q_SCALE = HEAD_DIM ** -0.5 * math.log2(math.e)
KEY_PAIR = 2 * KEY_TILE
ONES_ROWS = 16
_COL_SEGS = (("dqT", 512, True, None, None), ("nqT", 512, True, None, None),
             ("dvT", 512, False, KEY_PAIR, DIFF_VDIM), ("vsT", 128, False, KEY_PAIR, HEAD_DIM), ("vwT", 128, False, KEY_TILE, HEAD_DIM))
_STRIDED_SEGS = ("kc", "vc")


def _proj_kernel(x_ref, g_ref, cos_ref, sin_ref, wr_ref, wt_ref, *out_refs):
    x = x_ref[...]
    ms = jnp.mean(x * x, axis=-1, keepdims=True)
    u = (x * lax.rsqrt(ms + NORM_EPS) * g_ref[...]).astype(BF16)
    half = HEAD_DIM // 2
    reps = LANES // half

    def rope_fn(c, s, axis):
        idx = lax.broadcasted_iota(jnp.int32, c.shape, axis)
        upper = (idx % HEAD_DIM) >= half
        s_up = jnp.where(upper, s, 0.0)
        s_lo = jnp.where(upper, 0.0, -s)
        return lambda y: y * c + pltpu.roll(y, half, axis) * s_up + pltpu.roll(y, LANES - half, axis) * s_lo

    c = jnp.concatenate([cos_ref[...]] * reps, axis=1)
    s = jnp.concatenate([sin_ref[...]] * reps, axis=1)
    rope_rows = rope_fn(c, s, 1)
    rope_cols = rope_fn(c.T, s.T, 0)

    *refs, stage_ref = out_refs
    tm = x.shape[0]
    off = 0
    for name, width, _, roped in _ROW_SEGS:
        o_ref = refs.pop(0)
        for c0 in range(0, width, 2 * LANES):
            cw = min(2 * LANES, width - c0)
            y = jnp.dot(u, wr_ref[:, off + c0:off + c0 + cw], preferred_element_type=F32)
            for l0 in range(0, cw, LANES):
                yl = y[:, l0:l0 + LANES]
                if roped:
                    yl = rope_rows(yl)
                if name == "ng":
                    yl = jax.nn.sigmoid(yl)
                if name in _STRIDED_SEGS:
                    stage_ref[...] = yl
                    for l in range(CMP_STRIDE):
                        o_ref[:, l * LANES:(l + 1) * LANES] = stage_ref[pl.ds(l, tm // CMP_STRIDE, stride=CMP_STRIDE), :]
                else:
                    o_ref[:, c0 + l0:c0 + l0 + LANES] = yl.astype(o_ref.dtype)
        off += width

    off = 0
    ut = u.T
    for name, rows, roped, tile, group in _COL_SEGS:
        o_ref = refs.pop(0)
        for r0 in range(0, rows, 2 * LANES):
            rw = min(2 * LANES, rows - r0)
            yt = jnp.dot(wt_ref[off + r0:off + r0 + rw, :], ut, preferred_element_type=F32)
            for s0 in range(0, rw, LANES):
                ys = yt[s0:s0 + LANES]
                if roped:
                    ys = rope_cols(ys) * Q_SCALE
                ys = ys.astype(o_ref.dtype)
                if not tile:
                    o_ref[r0 + s0:r0 + s0 + LANES, :] = ys
                    continue
                for g0 in range(0, LANES, group):
                    dst = (r0 + s0 + g0) // group * (group + ONES_ROWS)
                    for j in range(tm // tile):
                        o_ref[j, dst:dst + group, :] = ys[g0:g0 + group, j * tile:(j + 1) * tile]
                        o_ref[j, dst + group:dst + group + ONES_ROWS, :] = jnp.ones((ONES_ROWS, tile), o_ref.dtype)
        off += rows


def _dup_groups(w):
    d = w.shape[0]
    wg = w.reshape(d, NSA_KV_GROUPS, 1, HEAD_DIM)
    return jnp.broadcast_to(wg, (d, NSA_KV_GROUPS, 2, HEAD_DIM)).reshape(d, NSA_KV_GROUPS * 2 * HEAD_DIM)


def _project(x2, g, cos, sin, w_in):
    t, d = x2.shape
    qk = DIFF_HEADS * 2 * HEAD_DIM
    kv = NSA_KV_GROUPS * HEAD_DIM
    splits = (qk, qk, DIFF_HEADS * DIFF_VDIM, NSA_HEADS * HEAD_DIM, kv, kv, kv, kv, kv, kv, 3 * NSA_HEADS)
    offs = np.cumsum((0,) + splits)
    dq, dk, dv, nq, kc, vc, ks, vs, kw, vw, ng = (w_in[:, offs[i]:offs[i + 1]] for i in range(len(splits)))
    ng = jnp.pad(ng, ((0, 0), (0, LANES - ng.shape[1])))
    wr = jnp.concatenate([dk, _dup_groups(ks), _dup_groups(kw), kc, vc, ng], axis=1)
    wt = jnp.concatenate([dq, nq, dv, vs, vw], axis=1).T
    tm = min(t, 512)
    assert tm % KEY_TILE == 0
    half = HEAD_DIM // 2
    out_specs, out_shape = [], []
    for name, w, dt, _ in _ROW_SEGS:
        fold = CMP_STRIDE if name in _STRIDED_SEGS else 1
        out_specs.append(pl.BlockSpec((tm // fold, w * fold), lambda i: (i, 0)))
        out_shape.append(jax.ShapeDtypeStruct((t // fold, w * fold), dt))
    for _, rows, _, tile, group in _COL_SEGS:
        if tile:
            assert tm % tile == 0
            padded = rows // group * (group + ONES_ROWS)
            out_specs.append(pl.BlockSpec((tm // tile, padded, tile), lambda i: (i, 0, 0)))
            out_shape.append(jax.ShapeDtypeStruct((t // tile, padded, tile), BF16))
        else:
            out_specs.append(pl.BlockSpec((rows, tm), lambda i: (0, i)))
            out_shape.append(jax.ShapeDtypeStruct((rows, t), BF16))
    outs = pl.pallas_call(
        _proj_kernel,
        grid=(t // tm,),
        in_specs=[
            pl.BlockSpec((tm, d), lambda i: (i, 0)),
            _const_spec((1, d)),
            pl.BlockSpec((tm, half), lambda i: (i, 0)),
            pl.BlockSpec((tm, half), lambda i: (i, 0)),
            _const_spec(wr.shape),
            _const_spec(wt.shape),
        ],
        out_specs=out_specs,
        out_shape=out_shape,
        scratch_shapes=[pltpu.VMEM((tm, LANES), F32)],
        compiler_params=_params(1),
        name="in_proj",
    )(x2, g[None, :], cos, sin, wr, wt)
    return dict(zip([n for n, *_ in _ROW_SEGS] + [n for n, *_ in _COL_SEGS], outs))


def _compress_kernel(x_ref, pea_ref, peb_ref, w1a_ref, w1b_ref, w2_ref, o_ref):
    x = x_ref[...]
    a = jnp.dot((x + pea_ref[...]).astype(BF16), w1a_ref[...], preferred_element_type=F32)
    b = jnp.dot((x + peb_ref[...]).astype(BF16), w1b_ref[...], preferred_element_type=F32)
    ncp = x.shape[0]
    h = a + pltpu.roll(b, ncp - 1, 0)
    h = h * jax.nn.sigmoid(h)
    o_ref[...] = jnp.dot(h.astype(BF16), w2_ref[...], preferred_element_type=F32).astype(o_ref.dtype)


def _compress(xc, pe, w1, w2, b, s):
    g_, dh = NSA_KV_GROUPS, HEAD_DIM
    ncp = s // CMP_STRIDE
    hid = w1.shape[1]
    halves = CMP_LEN // CMP_STRIDE
    assert halves == 2
    xr = xc.reshape(b, ncp, CMP_STRIDE * g_ * dh)
    eye = jnp.eye(g_, dtype=F32)
    pe_r = pe.reshape(halves, CMP_STRIDE, 1, dh)
    pe_t = jnp.broadcast_to(pe_r, (halves, CMP_STRIDE, g_, dh)).reshape(halves, 1, CMP_STRIDE * g_ * dh)
    w1r = w1.reshape(halves, CMP_STRIDE, dh, hid)
    w1bd = jnp.einsum("hldj,pg->hlpdgj", w1r, eye).reshape(halves, CMP_STRIDE * g_ * dh, g_ * hid).astype(BF16)
    w2bd = jnp.einsum("jd,pg,r->pjgrd", w2, eye, jnp.ones((2,), F32)).reshape(g_ * hid, g_ * 2 * dh).astype(BF16)
    kdim = CMP_STRIDE * g_ * dh
    return pl.pallas_call(
        _compress_kernel,
        grid=(b,),
        in_specs=[
            pl.BlockSpec((None, ncp, kdim), lambda i: (i, 0, 0)),
            _const_spec((1, kdim)),
            _const_spec((1, kdim)),
            _const_spec((kdim, g_ * hid)),
            _const_spec((kdim, g_ * hid)),
            _const_spec((g_ * hid, g_ * 2 * dh)),
        ],
        out_specs=pl.BlockSpec((None, ncp, g_ * 2 * dh), lambda i: (i, 0, 0)),
        out_shape=jax.ShapeDtypeStruct((b, ncp, g_ * 2 * dh), BF16),
        compiler_params=_params(1),
        name="compress",
    )(xr, pe_t[0], pe_t[1], w1bd[0], w1bd[1], w2bd)


def _split_heads(qt):
    row = lax.broadcasted_iota(jnp.int32, qt.shape, 0)
    zero = jnp.zeros_like(qt)
    return jnp.concatenate([jnp.where(row < HEAD_DIM, qt, zero), jnp.where(row >= HEAD_DIM, qt, zero)], axis=1)


def _flash_scratch(tk, m_cols, dv):
    row = pltpu.VMEM((1, m_cols), F32)
    sc = pltpu.VMEM((tk, m_cols), F32)
    return [sc, sc, row, row, pltpu.VMEM((2 * tk, m_cols), BF16), row, row, pltpu.VMEM((dv + ONES_ROWS, m_cols), F32)]


class _Stream(NamedTuple):
    qst: jax.Array
    load_k: Callable
    load_vt: Callable
    scratch: Sequence
    mask_fn: Callable
    inner_fn: Callable


def _flash_tiles(streams, tk, n_k, hi, first_step):
    def scores(st, t, slot, masked):
        s_ref, t_ref = st.scratch[slot], st.scratch[2 + slot]
        k = st.load_k(pl.multiple_of(jnp.clip(t, 0, n_k - 1) * tk, tk))
        s = (st.mask_fn if masked else st.inner_fn)(t, jnp.dot(k, st.qst, preferred_element_type=F32))
        t_ref[...] = jnp.max(s, axis=0, keepdims=True)
        s_ref[...] = s

    def probs(st):
        sa_ref, sb_ref, ta_ref, tb_ref, p_ref, al_ref, m_ref, _ = st.scratch
        m_old = m_ref[...]
        m_new = jnp.maximum(m_old, jnp.maximum(ta_ref[...], tb_ref[...]))
        p_ref[:tk, :] = jnp.exp2(sa_ref[...] - m_new).astype(BF16)
        p_ref[tk:, :] = jnp.exp2(sb_ref[...] - m_new).astype(BF16)
        al_ref[...] = jnp.exp2(m_old - m_new)
        m_ref[...] = m_new

    def values(st, pair):
        p_ref, al_ref, acc_ref = st.scratch[4], st.scratch[5], st.scratch[7]
        vt = st.load_vt(jnp.clip(pair, 0, n_k // 2 - 1))
        acc_ref[...] = al_ref[...] * acc_ref[...] + jnp.dot(vt, p_ref[...], preferred_element_type=F32)

    def each(fn, *args):
        for st in streams:
            fn(st, *args)

    def init(st):
        m_ref, al_ref = st.scratch[6], st.scratch[5]
        m_ref[...] = jnp.full(m_ref.shape, NEG_INF, F32)
        al_ref[...] = jnp.ones(al_ref.shape, F32)

    @pl.when(first_step)
    def _():
        for st in streams:
            st.scratch[4][...] = jnp.zeros(st.scratch[4].shape, BF16)
            st.scratch[7][...] = jnp.zeros(st.scratch[7].shape, F32)

    n_pairs = hi // 2 + 1

    def pair_at(r):
        return jnp.where(r == 0, n_pairs - 1, r - 1)

    each(init)
    each(scores, 2 * pair_at(0), 0, True)
    each(scores, 2 * pair_at(0) + 1, 1, True)

    def body(r, carry):
        each(values, pair_at(r - 1))
        each(probs)
        each(scores, 2 * r, 0, False)
        each(scores, 2 * r + 1, 1, False)
        return carry

    lax.fori_loop(0, n_pairs - 1, body, 0)
    each(values, pair_at(n_pairs - 2))
    each(probs)
    each(values, pair_at(n_pairs - 1))
    out = []
    for st in streams:
        acc = st.scratch[7][...]
        dv = acc.shape[0] - ONES_ROWS
        out.append((acc[dv:dv + 1], acc[:dv]))
    return out


def _diff_kernel(lam_ref, g_ref, qt_ref, k_ref, vt_ref, o_ref, *scratch, tq, tk, heads, lam_init):
    qi = pl.program_id(2)
    m_cols = 2 * tq
    qpos = qi * tq + lax.broadcasted_iota(jnp.int32, (1, m_cols), 1) % tq
    krel = lax.broadcasted_iota(jnp.int32, (tk, m_cols), 0)
    hw = 2 * HEAD_DIM
    vrows = DIFF_VDIM + ONES_ROWS
    n_sc = len(scratch) // heads

    def causal(t, s):
        return jnp.where(krel <= qpos - t * tk, s, NEG_INF)

    streams = [
        _Stream(
            qst=_split_heads(qt_ref[h * hw:(h + 1) * hw, :]),
            load_k=lambda row, h=h: k_ref[pl.ds(row, tk), h * hw:(h + 1) * hw],
            load_vt=lambda pair, h=h: vt_ref[pair, h * vrows:(h + 1) * vrows, :],
            scratch=scratch[h * n_sc:(h + 1) * n_sc],
            mask_fn=causal,
            inner_fn=lambda t, s: s,
        )
        for h in range(heads)
    ]
    first_step = (pl.program_id(0) == 0) & (pl.program_id(1) == 0) & (qi == 0)
    results = _flash_tiles(streams, tk, k_ref.shape[0] // tk, (qi * tq + tq - 1) // tk, first_step)
    lf = lam_ref[...]
    lam = (jnp.exp(jnp.sum(lf[0:1] * lf[1:2], axis=1, keepdims=True))
           - jnp.exp(jnp.sum(lf[2:3] * lf[3:4], axis=1, keepdims=True)) + lam_init)
    for h, (l, acc) in enumerate(results):
        o = acc * (1.0 / l)
        y = o[:, :tq] - lam * o[:, tq:]
        y = y * lax.rsqrt(jnp.mean(y * y, axis=0, keepdims=True) + NORM_EPS)
        y = y.T * g_ref[...] * (1.0 - lam_init)
        o_ref[:, h * DIFF_VDIM:(h + 1) * DIFF_VDIM] = y.astype(o_ref.dtype)


def _diff_attention(dqt, dk, dvt, diff_lambda, subln_g, b, s, layer):
    tk = KEY_TILE
    tq = 2 * tk
    heads = DIFF_HEADS
    hw = heads * 2 * HEAD_DIM
    vrows = heads * (DIFF_VDIM + ONES_ROWS)
    nq = s // tq
    return pl.pallas_call(
        functools.partial(_diff_kernel, tq=tq, tk=tk, heads=heads, lam_init=_lambda_init(layer)),
        grid=(b, DIFF_HEADS // heads, nq),
        in_specs=[
            _const_spec(diff_lambda.shape),
            _const_spec((1, DIFF_VDIM)),
            pl.BlockSpec((hw, tq), lambda bi, h, i: (h, bi * nq + i)),
            pl.BlockSpec((None, s, hw), lambda bi, h, i: (bi, 0, h)),
            pl.BlockSpec((None, s // KEY_PAIR, vrows, KEY_PAIR), lambda bi, h, i: (bi, 0, h, 0)),
        ],
        out_specs=pl.BlockSpec((None, tq, heads * DIFF_VDIM), lambda bi, h, i: (bi, i, h)),
        out_shape=jax.ShapeDtypeStruct((b, s, DIFF_HEADS * DIFF_VDIM), BF16),
        scratch_shapes=_flash_scratch(tk, 2 * tq, DIFF_VDIM) * heads,
        compiler_params=_params(3),
        name="diff_attn",
    )(diff_lambda, subln_g[None, :], dqt, dk.reshape(b, s, -1), dvt.reshape(b, s // KEY_PAIR, -1, KEY_PAIR))


def _nsa_kernel(ovl_ref, qt_ref, gate_ref, kc_ref, vc_ref, ks_ref, vst_ref, kw_ref, vwt_ref, o_ref, imp_ref, selb_ref, cnt_ref, *scratch, tq, tk):
    qi = pl.program_id(1)
    q0 = qi * tq
    m_cols = NSA_HPG * tq
    qpos_t = q0 + lax.broadcasted_iota(jnp.int32, (1, tq), 1)
    qpos = jnp.concatenate([qpos_t] * NSA_HPG, axis=1)
    dh = HEAD_DIM
    groups = range(NSA_KV_GROUPS)
    gw = NSA_HPG * dh
    kwid = 2 * dh
    vrows = dh + ONES_ROWS
    n_sc = len(scratch) // NSA_KV_GROUPS
    ncp = kc_ref.shape[0]
    nsb = imp_ref.shape[1]
    topk = min(SEL_TOPK, nsb)

    qsts = []
    last_cmp = lax.shift_right_arithmetic(qpos - (CMP_LEN - 1), int(math.log2(CMP_STRIDE)))
    cmask = lax.broadcasted_iota(jnp.int32, (ncp, m_cols), 0) <= last_cmp
    blk = lax.broadcasted_iota(jnp.int32, (nsb, tq), 0)
    qblk = qpos_t // SEL_LEN
    forced = (blk == 0) | (blk == qblk) | (blk == qblk - 1)
    for g in groups:
        qt = qt_ref[g * gw:(g + 1) * gw, :]
        qsts.append(jnp.concatenate([_split_heads(qt[:LANES]), _split_heads(qt[LANES:])], axis=1))
    scs = [jnp.where(cmask, jnp.dot(kc_ref[:, g * kwid:(g + 1) * kwid], qsts[g], preferred_element_type=F32), NEG_INF)
           for g in groups]
    pcs = [jnp.exp2(sc - jnp.max(sc, axis=0, keepdims=True)) for sc in scs]
    pcs = [pc * jnp.where(qpos >= CMP_LEN - 1, 1.0 / jnp.sum(pc, axis=0, keepdims=True), 0.0) for pc in pcs]
    o_cs = [lax.dot_general(vc_ref[:, g * kwid:(g + 1) * kwid], pcs[g].astype(BF16), (((0,), (0,)), ((), ())),
                            preferred_element_type=F32)[:dh] for g in groups]
    for g in groups:
        pc_sum = functools.reduce(jnp.add, [pcs[g][:, h * tq:(h + 1) * tq] for h in range(NSA_HPG)])
        imp = jnp.dot(ovl_ref[...], pc_sum, preferred_element_type=F32, precision=lax.Precision.HIGHEST)
        imp = jnp.where(forced, FORCE_SCORE, imp)
        imp_ref[g] = jnp.where(blk <= qblk, imp, NEG_INF)
        cnt_ref[g] = jnp.zeros((nsb, tq), jnp.int32)

    sub = 8
    last_blk = (q0 + tq - 1) // SEL_LEN
    for g0 in range(0, nsb, sub):
        @pl.when(g0 <= last_blk)
        def _(g0=g0):
            for g in groups:
                cnts = [cnt_ref[g, b0:b0 + sub, :] for b0 in range(0, nsb, sub)]
                for j in range(g0, g0 + sub):
                    row = jnp.broadcast_to(imp_ref[g, j:j + 1, :], (sub, tq))
                    for i, b0 in enumerate(range(0, nsb, sub)):
                        cur = imp_ref[g, b0:b0 + sub, :]
                        if b0 > j:
                            beats = row >= cur
                        elif b0 + sub - 1 < j:
                            beats = row > cur
                        else:
                            beats = (row > cur) | ((row == cur) & (lax.broadcasted_iota(jnp.int32, (sub, tq), 0) > j - b0))
                        cnts[i] = cnts[i] + jnp.where(beats, 1, 0)
                for i, b0 in enumerate(range(0, nsb, sub)):
                    cnt_ref[g, b0:b0 + sub, :] = cnts[i]
    for g in groups:
        selb_ref[g] = jnp.where(cnt_ref[g] < topk, 0.0, NEG_INF)

    krel = lax.broadcasted_iota(jnp.int32, (tk, m_cols), 0)
    bpt = tk // SEL_LEN
    last = (q0 + tq - 1) // tk

    def biased(g):
        def fn(t, s):
            first_blk = jnp.clip(t, 0, nsb // bpt - 1) * bpt
            rows = []
            for r in range(bpt):
                brow = selb_ref[g, pl.ds(first_blk + r, 1), :]
                brow = jnp.concatenate([brow] * NSA_HPG, axis=1)
                rows.append(jnp.broadcast_to(brow, (SEL_LEN, m_cols)))
            return s + jnp.concatenate(rows, axis=0)
        return fn

    def selected(g):
        return lambda t, s: jnp.where(krel <= qpos - t * tk, biased(g)(t, s), NEG_INF)

    streams = [
        _Stream(
            qst=qsts[g],
            load_k=lambda row, g=g: ks_ref[pl.ds(row, tk), g * kwid:(g + 1) * kwid],
            load_vt=lambda pair, g=g: vst_ref[pair, g * vrows:(g + 1) * vrows, :],
            scratch=scratch[g * n_sc:(g + 1) * n_sc],
            mask_fn=selected(g),
            inner_fn=biased(g),
        )
        for g in groups
    ]
    sel_out = _flash_tiles(streams, tk, ks_ref.shape[0] // tk, last, (pl.program_id(0) == 0) & (qi == 0))

    gates = gate_ref[...].T
    n_back = -(-(WINDOW - 1) // tk)
    win_tiles = [qi * (tq // tk) - n_back + r for r in range(n_back + tq // tk)]
    o_ss = [acc_s * (1.0 / l_s) for l_s, acc_s in sel_out]

    s_ws = [[] for _ in groups]
    for r, t in enumerate(win_tiles):
        d_min = (n_back - r) * tk - (tk - 1)
        d_max = (n_back - r) * tk + (tq - 1)
        for g in groups:
            k = kw_ref[pl.ds(pl.multiple_of(jnp.maximum(t, 0) * tk, tk), tk), g * kwid:(g + 1) * kwid]
            s = jnp.dot(k, qsts[g], preferred_element_type=F32)
            if d_min < 0:
                s = jnp.where(krel <= qpos - t * tk, s, NEG_INF)
            if d_max >= WINDOW:
                s = jnp.where(krel > qpos - t * tk - WINDOW, s, NEG_INF)
            if r < n_back:
                s = s + jnp.where(t >= 0, 0.0, NEG_INF)
            s_ws[g].append(s)
    m_ws = [functools.reduce(jnp.maximum, [jnp.max(s, axis=0, keepdims=True) for s in s_w]) for s_w in s_ws]
    acc_ws = [jnp.zeros((vrows, m_cols), F32) for _ in groups]
    for r, t in enumerate(win_tiles):
        for g in groups:
            p = jnp.exp2(s_ws[g][r] - m_ws[g]).astype(BF16)
            acc_ws[g] = acc_ws[g] + jnp.dot(vwt_ref[jnp.maximum(t, 0), g * vrows:(g + 1) * vrows, :], p, preferred_element_type=F32)
    o_ws = [acc_w[:dh] * (1.0 / acc_w[dh:dh + 1]) for acc_w in acc_ws]

    for g in groups:
        outs = []
        for h in range(NSA_HPG):
            sl = slice(h * tq, (h + 1) * tq)
            gate = lambda branch, h=h: gates[branch * NSA_HEADS + g * NSA_HPG + h:branch * NSA_HEADS + g * NSA_HPG + h + 1, :]
            outs.append(gate(0) * o_cs[g][:, sl] + gate(1) * o_ss[g][:, sl] + gate(2) * o_ws[g][:, sl])
        y = jnp.concatenate(outs, axis=0)
        o_ref[:, g * gw:(g + 1) * gw] = y.T.astype(o_ref.dtype)


def _overlap_matrix(s):
    nc = s // CMP_STRIDE
    nsb = s // SEL_LEN
    ci = np.arange(nc)[None, :] * CMP_STRIDE
    sj = np.arange(nsb)[:, None] * SEL_LEN
    ovl = ((ci < sj + SEL_LEN) & (ci + CMP_LEN > sj)).astype(np.float32)
    ovl[:, (s - CMP_LEN) // CMP_STRIDE + 1:] = 0.0
    return jnp.asarray(ovl)


def _nsa_attention(nqt, ng, kcmp, vcmp, ksd, vst, kwd, vwt, b, s):
    tk = tq = KEY_TILE
    nq = s // tq
    g_ = NSA_KV_GROUPS
    qw = NSA_HEADS * HEAD_DIM
    kw_ = g_ * 2 * HEAD_DIM
    vrows = g_ * (HEAD_DIM + ONES_ROWS)
    ncp = s // CMP_STRIDE
    nsb = s // SEL_LEN
    seq3 = lambda a: a.reshape(b, s, a.shape[-1])
    vt4 = lambda a: a.reshape(b, -1, a.shape[-2], a.shape[-1])
    k_spec = pl.BlockSpec((None, s, kw_), lambda bi, i: (bi, 0, 0))
    vt_spec = lambda tile: pl.BlockSpec((None, s // tile, vrows, tile), lambda bi, i: (bi, 0, 0, 0))
    cmp_spec = pl.BlockSpec((None, ncp, kw_), lambda bi, i: (bi, 0, 0))
    per_group = lambda dt: pltpu.VMEM((g_, nsb, tq), dt)
    return pl.pallas_call(
        functools.partial(_nsa_kernel, tq=tq, tk=tk),
        grid=(b, nq),
        in_specs=[
            _const_spec((nsb, ncp)),
            pl.BlockSpec((qw, tq), lambda bi, i: (0, bi * nq + i)),
            pl.BlockSpec((None, tq, LANES), lambda bi, i: (bi, i, 0)),
            cmp_spec, cmp_spec, k_spec, vt_spec(KEY_PAIR), k_spec, vt_spec(KEY_TILE),
        ],
        out_specs=pl.BlockSpec((None, tq, qw), lambda bi, i: (bi, i, 0)),
        out_shape=jax.ShapeDtypeStruct((b, s, qw), BF16),
        scratch_shapes=[per_group(F32), per_group(F32), per_group(jnp.int32)] + _flash_scratch(tk, NSA_HPG * tq, HEAD_DIM) * g_,
        compiler_params=_params(2),
        name="nsa_attn",
    )(_overlap_matrix(s), nqt, seq3(ng), kcmp, vcmp, seq3(ksd), vt4(vst), seq3(kwd), vt4(vwt))


def _merge_kernel(x_ref, g_ref, ya_ref, yb_ref, wmg_ref, wa_ref, wb_ref, wo_ref, o_ref):
    x = x_ref[...]
    d = x.shape[1]
    ms = jnp.mean(x * x, axis=-1, keepdims=True)
    u = (x * lax.rsqrt(ms + NORM_EPS) * g_ref[...]).astype(BF16)
    ya = jnp.dot(ya_ref[...], wa_ref[...], preferred_element_type=F32)
    yb = jnp.dot(yb_ref[...], wb_ref[...], preferred_element_type=F32)
    mg_a = jax.nn.sigmoid(jnp.dot(u, wmg_ref[:, :d], preferred_element_type=F32))
    mg_b = jax.nn.sigmoid(jnp.dot(u, wmg_ref[:, d:], preferred_element_type=F32))
    merged = (mg_a * ya + mg_b * yb).astype(BF16)
    o_ref[...] = x + jnp.dot(merged, wo_ref[...], preferred_element_type=F32)


def _merge(x2, g, ya, yb, w_mg, w_a, w_b, w_o):
    t, d = x2.shape
    tm = min(t, 512)
    ws = [w.astype(BF16) for w in (w_mg, w_a, w_b, w_o)]
    return pl.pallas_call(
        _merge_kernel,
        grid=(t // tm,),
        in_specs=[
            pl.BlockSpec((tm, d), lambda i: (i, 0)),
            _const_spec((1, d)),
            pl.BlockSpec((tm, ya.shape[1]), lambda i: (i, 0)),
            pl.BlockSpec((tm, yb.shape[1]), lambda i: (i, 0)),
        ] + [_const_spec(w.shape) for w in ws],
        out_specs=pl.BlockSpec((tm, d), lambda i: (i, 0)),
        out_shape=jax.ShapeDtypeStruct((t, d), F32),
        compiler_params=_params(1),
        name="merge_out_proj",
    )(x2, g[None, :], ya, yb, *ws)


def _ffn_kernel(h_ref, g_ref, gf_ref, wg_ref, wu_ref, wd_ref, o_ref, *, chunk, final_norm):
    h = h_ref[...]
    ms = jnp.mean(h * h, axis=-1, keepdims=True)
    u = (h * lax.rsqrt(ms + NORM_EPS) * g_ref[...]).astype(BF16)
    d_ff = wg_ref.shape[1]
    acc = h
    for c0 in range(0, d_ff, chunk):
        a = jnp.dot(u, wg_ref[:, c0:c0 + chunk], preferred_element_type=F32)
        up = jnp.dot(u, wu_ref[:, c0:c0 + chunk], preferred_element_type=F32)
        act = (a * jax.nn.sigmoid(a) * up).astype(BF16)
        acc = acc + jnp.dot(act, wd_ref[c0:c0 + chunk, :], preferred_element_type=F32)
    if final_norm:
        ms = jnp.mean(acc * acc, axis=-1, keepdims=True)
        acc = acc * lax.rsqrt(ms + NORM_EPS) * gf_ref[...]
    o_ref[...] = acc


def _ffn(h2, g, g_final, w_gate, w_up, w_down, final_norm):
    t, d = h2.shape
    tm = min(t, 512)
    d_ff = w_gate.shape[1]
    chunk = 256
    assert d_ff % chunk == 0
    ws = [w.astype(BF16) for w in (w_gate, w_up, w_down)]
    return pl.pallas_call(
        functools.partial(_ffn_kernel, chunk=chunk, final_norm=final_norm),
        grid=(t // tm,),
        in_specs=[pl.BlockSpec((tm, d), lambda i: (i, 0)), _const_spec((1, d)), _const_spec((1, d))]
        + [_const_spec(w.shape) for w in ws],
        out_specs=pl.BlockSpec((tm, d), lambda i: (i, 0)),
        out_shape=jax.ShapeDtypeStruct((t, d), F32),
        compiler_params=_params(1),
        name="swiglu_ffn",
    )(h2, g[None, :], g_final[None, :], *ws)


def kernel(x, positions, attn_norm_g, w_in, diff_lambda, diff_subln_g, cmp_pe_k, cmp_pe_v, cmp_k_w1, cmp_k_w2, cmp_v_w1, cmp_v_w2, w_branch_a, w_branch_b, w_out, ffn_norm_g, w_gate, w_up, w_down, final_norm_g):
    b, s, d = x.shape
    depth = w_in.shape[0]
    cos, sin = _rope_tables(positions)
    h = x.reshape(b * s, d)
    gates_off = w_in.shape[2] - 2 * d
    for layer in range(depth):
        w_in_b = w_in[layer].astype(BF16)
        p = _project(h, attn_norm_g[layer], cos, sin, w_in_b[:, :gates_off])
        kcmp = _compress(p["kc"], cmp_pe_k[layer], cmp_k_w1[layer], cmp_k_w2[layer], b, s)
        vcmp = _compress(p["vc"], cmp_pe_v[layer], cmp_v_w1[layer], cmp_v_w2[layer], b, s)
        ya = _diff_attention(p["dqT"], p["dk"], p["dvT"], diff_lambda[layer], diff_subln_g[layer], b, s, layer)
        yb = _nsa_attention(p["nqT"], p["ng"], kcmp, vcmp, p["ksd"], p["vsT"], p["kwd"], p["vwT"], b, s)
        h = _merge(h, attn_norm_g[layer], ya.reshape(b * s, -1), yb.reshape(b * s, -1),
                   w_in_b[:, gates_off:], w_branch_a[layer], w_branch_b[layer], w_out[layer])
        h = _ffn(h, ffn_norm_g[layer], final_norm_g, w_gate[layer], w_up[layer], w_down[layer], layer == depth - 1)
    return h.reshape(b, s, d)
```

```python
import functools
import math
from typing import Callable, NamedTuple, Sequence

import jax
import jax.numpy as jnp
import numpy as np
from jax import lax
from jax.experimental import pallas as pl
from jax.experimental.pallas import tpu as pltpu

HEAD_DIM = 64
ROPE_THETA = 10000.0
NORM_EPS = 1e-6
NEG_INF = -1e30
FORCE_SCORE = 1e9

DIFF_HEADS = 4
DIFF_VDIM = 2 * HEAD_DIM
NSA_HEADS = 8
NSA_KV_GROUPS = 2
NSA_HPG = NSA_HEADS // NSA_KV_GROUPS
CMP_LEN = 32
CMP_STRIDE = 16
SEL_LEN = 64
SEL_TOPK = 16
WINDOW = 512

LANES = 128
VMEM_LIMIT = 56 * 1024 * 1024
KEY_TILE = 256

BF16 = jnp.bfloat16
F32 = jnp.float32


def _lambda_init(layer):
    return 0.8 - 0.6 * math.exp(-0.3 * layer)


def _params(n_axes):
    return pltpu.CompilerParams(dimension_semantics=("arbitrary",) * n_axes, vmem_limit_bytes=VMEM_LIMIT)


def _const_spec(shape):
    nd = len(shape)
    return pl.BlockSpec(shape, lambda *_: (0,) * nd)


def _rope_table_kernel(pos_ref, invf_ref, cos_ref, sin_ref):
    ang = pos_ref[...].astype(F32) * invf_ref[...]
    cos_ref[...] = jnp.cos(ang)
    sin_ref[...] = jnp.sin(ang)


def _rope_tables(positions):
    half = HEAD_DIM // 2
    per_row = LANES // half
    t = positions.size
    rows = t // per_row
    pos_e = jnp.repeat(positions.reshape(rows, per_row), half, axis=1)
    inv_freq = 1.0 / (ROPE_THETA ** (jnp.arange(0, HEAD_DIM, 2, dtype=F32) / HEAD_DIM))
    invf = jnp.tile(inv_freq, per_row)[None, :]
    tr = min(rows, 1024)
    cos, sin = pl.pallas_call(
        _rope_table_kernel,
        grid=(rows // tr,),
        in_specs=[pl.BlockSpec((tr, LANES), lambda i: (i, 0)), _const_spec((1, LANES))],
        out_specs=[pl.BlockSpec((tr, LANES), lambda i: (i, 0))] * 2,
        out_shape=[jax.ShapeDtypeStruct((rows, LANES), F32)] * 2,
        compiler_params=_params(1),
        name="rope_tables",
    )(pos_e, invf)
    return cos.reshape(t, half), sin.reshape(t, half)


_ROW_SEGS = (("dk", 512, BF16, True), ("ksd", 128, BF16, True), ("kwd", 128, BF16, True), ("kc", 128, F32, True),
             ("vc", 128, F32, False), ("ng", 128, F32, False))
_DUP_SEGS = ("ksd", "kwd")
Q_SCALE = HEAD_DIM ** -0.5 * math.log2(math.e)
KEY_PAIR = 2 * KEY_TILE
ONES_ROWS = 16
_COL_SEGS = (("dqT", 512, True, None, None), ("nqT", 512, True, None, None),
             ("dvT", 512, False, KEY_PAIR, DIFF_VDIM), ("vsT", 128, False, KEY_PAIR, HEAD_DIM), ("vwT", 128, False, KEY_TILE, HEAD_DIM))
_STRIDED_SEGS = ("kc", "vc")


def _proj_kernel(x_ref, g_ref, cos_ref, sin_ref, wr_ref, wt_ref, *out_refs):
    x = x_ref[...]
    ms = jnp.mean(x * x, axis=-1, keepdims=True)
    u = (x * lax.rsqrt(ms + NORM_EPS) * g_ref[...]).astype(BF16)
    half = HEAD_DIM // 2
    reps = LANES // half

    def rope_fn(c, s, axis):
        idx = lax.broadcasted_iota(jnp.int32, c.shape, axis)
        upper = (idx % HEAD_DIM) >= half
        s_up = jnp.where(upper, s, 0.0)
        s_lo = jnp.where(upper, 0.0, -s)
        return lambda y: y * c + pltpu.roll(y, half, axis) * s_up + pltpu.roll(y, LANES - half, axis) * s_lo

    c = jnp.concatenate([cos_ref[...]] * reps, axis=1)
    s = jnp.concatenate([sin_ref[...]] * reps, axis=1)
    rope_rows = rope_fn(c, s, 1)
    rope_cols = rope_fn(c.T, s.T, 0)

    *refs, stage_ref = out_refs
    tm = x.shape[0]
    pieces = []
    for name, width, _, roped in _ROW_SEGS:
        o_ref = refs.pop(0)
        pieces += [(name, o_ref, c, roped) for c in range(0, width, LANES)]
    for p0 in range(0, len(pieces), 2):
        pair = pieces[p0:p0 + 2]
        y = jnp.dot(u, wr_ref[:, p0 * LANES:(p0 + len(pair)) * LANES], preferred_element_type=F32)
        for i, (name, o_ref, c, roped) in enumerate(pair):
            yl = y[:, i * LANES:(i + 1) * LANES]
            if roped:
                yl = rope_rows(yl)
            if name == "ng":
                yl = jax.nn.sigmoid(yl)
            if name in _DUP_SEGS:
                other = pltpu.roll(yl, HEAD_DIM, 1)
                first = lax.broadcasted_iota(jnp.int32, yl.shape, 1) < HEAD_DIM
                o_ref[:, :LANES] = jnp.where(first, yl, other).astype(o_ref.dtype)
                o_ref[:, LANES:] = jnp.where(first, other, yl).astype(o_ref.dtype)
            elif name in _STRIDED_SEGS:
                stage_ref[...] = yl
                for l in range(CMP_STRIDE):
                    o_ref[:, l * LANES:(l + 1) * LANES] = stage_ref[pl.ds(l, tm // CMP_STRIDE, stride=CMP_STRIDE), :]
            else:
                o_ref[:, c:c + LANES] = yl.astype(o_ref.dtype)

    off = 0
    ut = u.T
    for name, rows, roped, tile, group in _COL_SEGS:
        o_ref = refs.pop(0)
        for r0 in range(0, rows, 2 * LANES):
            rw = min(2 * LANES, rows - r0)
            yt = jnp.dot(wt_ref[off + r0:off + r0 + rw, :], ut, preferred_element_type=F32)
            for s0 in range(0, rw, LANES):
                ys = yt[s0:s0 + LANES]
                if roped:
                    ys = rope_cols(ys) * Q_SCALE
                ys = ys.astype(o_ref.dtype)
                if not tile:
                    o_ref[r0 + s0:r0 + s0 + LANES, :] = ys
                    continue
                for g0 in range(0, LANES, group):
                    dst = (r0 + s0 + g0) // group * (group + ONES_ROWS)
                    for j in range(tm // tile):
                        o_ref[j, dst:dst + group, :] = ys[g0:g0 + group, j * tile:(j + 1) * tile]
                        o_ref[j, dst + group:dst + group + ONES_ROWS, :] = jnp.ones((ONES_ROWS, tile), o_ref.dtype)
        off += rows


def _project(x2, g, cos, sin, w_in):
    t, d = x2.shape
    qk = DIFF_HEADS * 2 * HEAD_DIM
    kv = NSA_KV_GROUPS * HEAD_DIM
    splits = (qk, qk, DIFF_HEADS * DIFF_VDIM, NSA_HEADS * HEAD_DIM, kv, kv, kv, kv, kv, kv, 3 * NSA_HEADS)
    offs = np.cumsum((0,) + splits)
    dq, dk, dv, nq, kc, vc, ks, vs, kw, vw, ng = (w_in[:, offs[i]:offs[i + 1]] for i in range(len(splits)))
    ng = jnp.pad(ng, ((0, 0), (0, LANES - ng.shape[1])))
    wr = jnp.concatenate([dk, ks, kw, kc, vc, ng], axis=1)
    wt = jnp.concatenate([dq, nq, dv, vs, vw], axis=1).T
    tm = min(t, 512)
    assert tm % KEY_TILE == 0
    half = HEAD_DIM // 2
    out_specs, out_shape = [], []
    for name, w, dt, _ in _ROW_SEGS:
        fold = CMP_STRIDE if name in _STRIDED_SEGS else 1
        w_out = w * fold * (2 if name in _DUP_SEGS else 1)
        out_specs.append(pl.BlockSpec((tm // fold, w_out), lambda i: (i, 0)))
        out_shape.append(jax.ShapeDtypeStruct((t // fold, w_out), dt))
    for _, rows, _, tile, group in _COL_SEGS:
        if tile:
            assert tm % tile == 0
            padded = rows // group * (group + ONES_ROWS)
            out_specs.append(pl.BlockSpec((tm // tile, padded, tile), lambda i: (i, 0, 0)))
            out_shape.append(jax.ShapeDtypeStruct((t // tile, padded, tile), BF16))
        else:
            out_specs.append(pl.BlockSpec((rows, tm), lambda i: (0, i)))
            out_shape.append(jax.ShapeDtypeStruct((rows, t), BF16))
    outs = pl.pallas_call(
        _proj_kernel,
        grid=(t // tm,),
        in_specs=[
            pl.BlockSpec((tm, d), lambda i: (i, 0)),
            _const_spec((1, d)),
            pl.BlockSpec((tm, half), lambda i: (i, 0)),
            pl.BlockSpec((tm, half), lambda i: (i, 0)),
            _const_spec(wr.shape),
            _const_spec(wt.shape),
        ],
        out_specs=out_specs,
        out_shape=out_shape,
        scratch_shapes=[pltpu.VMEM((tm, LANES), F32)],
        compiler_params=_params(1),
        name="in_proj",
    )(x2, g[None, :], cos, sin, wr, wt)
    return dict(zip([n for n, *_ in _ROW_SEGS] + [n for n, *_ in _COL_SEGS], outs))


def _compress_kernel(x_ref, pea_ref, peb_ref, w1a_ref, w1b_ref, w2_ref, o_ref):
    x = x_ref[...]
    a = jnp.dot((x + pea_ref[...]).astype(BF16), w1a_ref[...], preferred_element_type=F32)
    b = jnp.dot((x + peb_ref[...]).astype(BF16), w1b_ref[...], preferred_element_type=F32)
    ncp = x.shape[0]
    h = a + pltpu.roll(b, ncp - 1, 0)
    h = h * jax.nn.sigmoid(h)
    o_ref[...] = jnp.dot(h.astype(BF16), w2_ref[...], preferred_element_type=F32).astype(o_ref.dtype)


def _compress(xc, pe, w1, w2, b, s):
    g_, dh = NSA_KV_GROUPS, HEAD_DIM
    ncp = s // CMP_STRIDE
    hid = w1.shape[1]
    halves = CMP_LEN // CMP_STRIDE
    assert halves == 2
    xr = xc.reshape(b, ncp, CMP_STRIDE * g_ * dh)
    eye = jnp.eye(g_, dtype=F32)
    pe_r = pe.reshape(halves, CMP_STRIDE, 1, dh)
    pe_t = jnp.broadcast_to(pe_r, (halves, CMP_STRIDE, g_, dh)).reshape(halves, 1, CMP_STRIDE * g_ * dh)
    w1r = w1.reshape(halves, CMP_STRIDE, dh, hid)
    w1bd = jnp.einsum("hldj,pg->hlpdgj", w1r, eye).reshape(halves, CMP_STRIDE * g_ * dh, g_ * hid).astype(BF16)
    w2bd = jnp.einsum("jd,pg,r->pjgrd", w2, eye, jnp.ones((2,), F32)).reshape(g_ * hid, g_ * 2 * dh).astype(BF16)
    kdim = CMP_STRIDE * g_ * dh
    return pl.pallas_call(
        _compress_kernel,
        grid=(b,),
        in_specs=[
            pl.BlockSpec((None, ncp, kdim), lambda i: (i, 0, 0)),
            _const_spec((1, kdim)),
            _const_spec((1, kdim)),
            _const_spec((kdim, g_ * hid)),
            _const_spec((kdim, g_ * hid)),
            _const_spec((g_ * hid, g_ * 2 * dh)),
        ],
        out_specs=pl.BlockSpec((None, ncp, g_ * 2 * dh), lambda i: (i, 0, 0)),
        out_shape=jax.ShapeDtypeStruct((b, ncp, g_ * 2 * dh), BF16),
        compiler_params=_params(1),
        name="compress",
    )(xr, pe_t[0], pe_t[1], w1bd[0], w1bd[1], w2bd)


def _split_heads(qt):
    row = lax.broadcasted_iota(jnp.int32, qt.shape, 0)
    zero = jnp.zeros_like(qt)
    return jnp.concatenate([jnp.where(row < HEAD_DIM, qt, zero), jnp.where(row >= HEAD_DIM, qt, zero)], axis=1)


def _flash_scratch(tk, m_cols, dv):
    row = pltpu.VMEM((1, m_cols), F32)
    sc = pltpu.VMEM((tk, m_cols), F32)
    return [sc, sc, row, row, pltpu.VMEM((2 * tk, m_cols), BF16), row, row, pltpu.VMEM((dv + ONES_ROWS, m_cols), F32)]


class _Stream(NamedTuple):
    qst: jax.Array
    load_k: Callable
    load_vt: Callable
    scratch: Sequence
    mask_fn: Callable
    inner_fn: Callable


def _flash_tiles(streams, tk, n_k, hi, first_step):
    def scores(st, t, slot, masked):
        s_ref, t_ref = st.scratch[slot], st.scratch[2 + slot]
        k = st.load_k(pl.multiple_of(jnp.clip(t, 0, n_k - 1) * tk, tk))
        s = (st.mask_fn if masked else st.inner_fn)(t, jnp.dot(k, st.qst, preferred_element_type=F32))
        t_ref[...] = jnp.max(s, axis=0, keepdims=True)
        s_ref[...] = s

    def probs(st):
        sa_ref, sb_ref, ta_ref, tb_ref, p_ref, al_ref, m_ref, _ = st.scratch
        m_old = m_ref[...]
        m_new = jnp.maximum(m_old, jnp.maximum(ta_ref[...], tb_ref[...]))
        p_ref[:tk, :] = jnp.exp2(sa_ref[...] - m_new).astype(BF16)
        p_ref[tk:, :] = jnp.exp2(sb_ref[...] - m_new).astype(BF16)
        al_ref[...] = jnp.exp2(m_old - m_new)
        m_ref[...] = m_new

    def values(st, pair):
        p_ref, al_ref, acc_ref = st.scratch[4], st.scratch[5], st.scratch[7]
        vt = st.load_vt(jnp.clip(pair, 0, n_k // 2 - 1))
        acc_ref[...] = al_ref[...] * acc_ref[...] + jnp.dot(vt, p_ref[...], preferred_element_type=F32)

    def each(fn, *args):
        for st in streams:
            fn(st, *args)

    def init(st):
        m_ref, al_ref = st.scratch[6], st.scratch[5]
        m_ref[...] = jnp.full(m_ref.shape, NEG_INF, F32)
        al_ref[...] = jnp.ones(al_ref.shape, F32)

    @pl.when(first_step)
    def _():
        for st in streams:
            st.scratch[4][...] = jnp.zeros(st.scratch[4].shape, BF16)
            st.scratch[7][...] = jnp.zeros(st.scratch[7].shape, F32)

    n_pairs = hi // 2 + 1

    def pair_at(r):
        return jnp.where(r == 0, n_pairs - 1, r - 1)

    each(init)
    each(scores, 2 * pair_at(0), 0, True)
    each(scores, 2 * pair_at(0) + 1, 1, True)

    def body(r, carry):
        each(values, pair_at(r - 1))
        each(probs)
        each(scores, 2 * r, 0, False)
        each(scores, 2 * r + 1, 1, False)
        return carry

    lax.fori_loop(0, n_pairs - 1, body, 0)
    each(values, pair_at(n_pairs - 2))
    each(probs)
    each(values, pair_at(n_pairs - 1))
    out = []
    for st in streams:
        acc = st.scratch[7][...]
        dv = acc.shape[0] - ONES_ROWS
        out.append((acc[dv:dv + 1], acc[:dv]))
    return out


def _diff_kernel(lam_ref, g_ref, qt_ref, k_ref, vt_ref, o_ref, *scratch, tq, tk, heads, lam_init):
    qi = pl.program_id(2)
    m_cols = 2 * tq
    qpos = qi * tq + lax.broadcasted_iota(jnp.int32, (1, m_cols), 1) % tq
    krel = lax.broadcasted_iota(jnp.int32, (tk, m_cols), 0)
    hw = 2 * HEAD_DIM
    vrows = DIFF_VDIM + ONES_ROWS
    n_sc = len(scratch) // heads

    def causal(t, s):
        return jnp.where(krel <= qpos - t * tk, s, NEG_INF)

    streams = [
        _Stream(
            qst=_split_heads(qt_ref[h * hw:(h + 1) * hw, :]),
            load_k=lambda row, h=h: k_ref[pl.ds(row, tk), h * hw:(h + 1) * hw],
            load_vt=lambda pair, h=h: vt_ref[pair, h * vrows:(h + 1) * vrows, :],
            scratch=scratch[h * n_sc:(h + 1) * n_sc],
            mask_fn=causal,
            inner_fn=lambda t, s: s,
        )
        for h in range(heads)
    ]
    first_step = (pl.program_id(0) == 0) & (pl.program_id(1) == 0) & (qi == 0)
    results = _flash_tiles(streams, tk, k_ref.shape[0] // tk, (qi * tq + tq - 1) // tk, first_step)
    lf = lam_ref[...]
    lam = (jnp.exp(jnp.sum(lf[0:1] * lf[1:2], axis=1, keepdims=True))
           - jnp.exp(jnp.sum(lf[2:3] * lf[3:4], axis=1, keepdims=True)) + lam_init)
    for h, (l, acc) in enumerate(results):
        o = acc * (1.0 / l)
        y = o[:, :tq] - lam * o[:, tq:]
        y = y * lax.rsqrt(jnp.mean(y * y, axis=0, keepdims=True) + NORM_EPS)
        y = y.T * g_ref[...] * (1.0 - lam_init)
        o_ref[:, h * DIFF_VDIM:(h + 1) * DIFF_VDIM] = y.astype(o_ref.dtype)


def _diff_attention(dqt, dk, dvt, diff_lambda, subln_g, b, s, layer):
    tk = KEY_TILE
    tq = 2 * tk
    heads = DIFF_HEADS
    hw = heads * 2 * HEAD_DIM
    vrows = heads * (DIFF_VDIM + ONES_ROWS)
    nq = s // tq
    return pl.pallas_call(
        functools.partial(_diff_kernel, tq=tq, tk=tk, heads=heads, lam_init=_lambda_init(layer)),
        grid=(b, DIFF_HEADS // heads, nq),
        in_specs=[
            _const_spec(diff_lambda.shape),
            _const_spec((1, DIFF_VDIM)),
            pl.BlockSpec((hw, tq), lambda bi, h, i: (h, bi * nq + i)),
            pl.BlockSpec((None, s, hw), lambda bi, h, i: (bi, 0, h)),
            pl.BlockSpec((None, s // KEY_PAIR, vrows, KEY_PAIR), lambda bi, h, i: (bi, 0, h, 0)),
        ],
        out_specs=pl.BlockSpec((None, tq, heads * DIFF_VDIM), lambda bi, h, i: (bi, i, h)),
        out_shape=jax.ShapeDtypeStruct((b, s, DIFF_HEADS * DIFF_VDIM), BF16),
        scratch_shapes=_flash_scratch(tk, 2 * tq, DIFF_VDIM) * heads,
        compiler_params=_params(3),
        name="diff_attn",
    )(diff_lambda, subln_g[None, :], dqt, dk.reshape(b, s, -1), dvt.reshape(b, s // KEY_PAIR, -1, KEY_PAIR))


def _nsa_kernel(ovl_ref, qt_ref, gate_ref, kc_ref, vc_ref, ks_ref, vst_ref, kw_ref, vwt_ref, o_ref, imp_ref, selb_ref, cnt_ref, *scratch, tq, tk, nsub):
    step = pl.program_id(1)
    m_cols = NSA_HPG * tq
    dh = HEAD_DIM
    gw = NSA_HPG * dh
    kwid = 2 * dh
    vrows = dh + ONES_ROWS
    units = [(j, g) for j in range(nsub) for g in range(NSA_KV_GROUPS)]
    n_sc = len(scratch) // len(units)
    ncp = kc_ref.shape[0]
    nsb = imp_ref.shape[1]
    topk = min(SEL_TOPK, nsb)
    bpt = tk // SEL_LEN
    n_back = -(-(WINDOW - 1) // tk)
    krel = lax.broadcasted_iota(jnp.int32, (tk, m_cols), 0)
    blk = lax.broadcasted_iota(jnp.int32, (nsb, tq), 0)
    crow = lax.broadcasted_iota(jnp.int32, (ncp, m_cols), 0)

    q0s = [(step * nsub + j) * tq for j in range(nsub)]
    qpos_ts = [q0 + lax.broadcasted_iota(jnp.int32, (1, tq), 1) for q0 in q0s]
    qposs = [jnp.concatenate([qp] * NSA_HPG, axis=1) for qp in qpos_ts]
    cmasks = [crow <= lax.shift_right_arithmetic(qp - (CMP_LEN - 1), int(math.log2(CMP_STRIDE))) for qp in qposs]
    qblks = [qp // SEL_LEN for qp in qpos_ts]
    forceds = [(blk == 0) | (blk == qb) | (blk == qb - 1) for qb in qblks]

    qsts = []
    for j, g in units:
        qt = qt_ref[g * gw:(g + 1) * gw, j * tq:(j + 1) * tq]
        qsts.append(jnp.concatenate([_split_heads(qt[:LANES]), _split_heads(qt[LANES:])], axis=1))
    scs = [jnp.where(cmasks[j], jnp.dot(kc_ref[:, g * kwid:(g + 1) * kwid], qsts[u], preferred_element_type=F32), NEG_INF)
           for u, (j, g) in enumerate(units)]
    pcs = [jnp.exp2(sc - jnp.max(sc, axis=0, keepdims=True)) for sc in scs]
    pcs = [pc * jnp.where(qposs[j] >= CMP_LEN - 1, 1.0 / jnp.sum(pc, axis=0, keepdims=True), 0.0)
           for pc, (j, g) in zip(pcs, units)]
    o_cs = [lax.dot_general(vc_ref[:, g * kwid:(g + 1) * kwid], pcs[u].astype(BF16), (((0,), (0,)), ((), ())),
                            preferred_element_type=F32)[:dh] for u, (j, g) in enumerate(units)]
    for u, (j, g) in enumerate(units):
        pc_sum = functools.reduce(jnp.add, [pcs[u][:, h * tq:(h + 1) * tq] for h in range(NSA_HPG)])
        imp = jnp.dot(ovl_ref[...], pc_sum, preferred_element_type=F32, precision=lax.Precision.HIGHEST)
        imp = jnp.where(forceds[j], FORCE_SCORE, imp)
        imp_ref[u] = jnp.where(blk <= qblks[j], imp, NEG_INF)
        cnt_ref[u] = jnp.zeros((nsb, tq), jnp.int32)

    sub = 8
    for g0 in range(0, nsb, sub):
        for j in range(nsub):
            @pl.when(g0 <= (q0s[j] + tq - 1) // SEL_LEN)
            def _(g0=g0, j=j):
                for u in [u for u, (ju, _) in enumerate(units) if ju == j]:
                    cnts = [cnt_ref[u, b0:b0 + sub, :] for b0 in range(0, nsb, sub)]
                    for i in range(g0, g0 + sub):
                        row = jnp.broadcast_to(imp_ref[u, i:i + 1, :], (sub, tq))
                        for n, b0 in enumerate(range(0, nsb, sub)):
                            cur = imp_ref[u, b0:b0 + sub, :]
                            if b0 > i:
                                beats = row >= cur
                            elif b0 + sub - 1 < i:
                                beats = row > cur
                            else:
                                beats = (row > cur) | ((row == cur) & (lax.broadcasted_iota(jnp.int32, (sub, tq), 0) > i - b0))
                            cnts[n] = cnts[n] + jnp.where(beats, 1, 0)
                    for n, b0 in enumerate(range(0, nsb, sub)):
                        cnt_ref[u, b0:b0 + sub, :] = cnts[n]
    for u in range(len(units)):
        selb_ref[u] = jnp.where(cnt_ref[u] < topk, 0.0, NEG_INF)

    def biased(u):
        def fn(t, s):
            first_blk = jnp.clip(t, 0, nsb // bpt - 1) * bpt
            rows = []
            for r in range(bpt):
                brow = selb_ref[u, pl.ds(first_blk + r, 1), :]
                brow = jnp.concatenate([brow] * NSA_HPG, axis=1)
                rows.append(jnp.broadcast_to(brow, (SEL_LEN, m_cols)))
            return s + jnp.concatenate(rows, axis=0)
        return fn

    def selected(u, j):
        return lambda t, s: jnp.where(krel <= qposs[j] - t * tk, biased(u)(t, s), NEG_INF)

    streams = [
        _Stream(
            qst=qsts[u],
            load_k=lambda row, g=g: ks_ref[pl.ds(row, tk), g * kwid:(g + 1) * kwid],
            load_vt=lambda pair, g=g: vst_ref[pair, g * vrows:(g + 1) * vrows, :],
            scratch=scratch[u * n_sc:(u + 1) * n_sc],
            mask_fn=selected(u, j),
            inner_fn=biased(u),
        )
        for u, (j, g) in enumerate(units)
    ]
    last = (q0s[-1] + tq - 1) // tk
    sel_out = _flash_tiles(streams, tk, ks_ref.shape[0] // tk, last, (pl.program_id(0) == 0) & (step == 0))
    o_ss = [acc_s * (1.0 / l_s) for l_s, acc_s in sel_out]

    n_win = n_back + tq // tk
    win_tile = lambda j, r: (step * nsub + j) * (tq // tk) - n_back + r
    s_ws = [[] for _ in units]
    for r in range(n_win):
        d_min = (n_back - r) * tk - (tk - 1)
        d_max = (n_back - r) * tk + (tq - 1)
        for u, (j, g) in enumerate(units):
            t = win_tile(j, r)
            k = kw_ref[pl.ds(pl.multiple_of(jnp.maximum(t, 0) * tk, tk), tk), g * kwid:(g + 1) * kwid]
            s = jnp.dot(k, qsts[u], preferred_element_type=F32)
            if d_min < 0:
                s = jnp.where(krel <= qposs[j] - t * tk, s, NEG_INF)
            if d_max >= WINDOW:
                s = jnp.where(krel > qposs[j] - t * tk - WINDOW, s, NEG_INF)
            if r < n_back:
                s = s + jnp.where(t >= 0, 0.0, NEG_INF)
            s_ws[u].append(s)
    m_ws = [functools.reduce(jnp.maximum, [jnp.max(s, axis=0, keepdims=True) for s in s_w]) for s_w in s_ws]
    acc_ws = [jnp.zeros((vrows, m_cols), F32) for _ in units]
    for r in range(n_win):
        for u, (j, g) in enumerate(units):
            p = jnp.exp2(s_ws[u][r] - m_ws[u]).astype(BF16)
            vt = vwt_ref[jnp.maximum(win_tile(j, r), 0), g * vrows:(g + 1) * vrows, :]
            acc_ws[u] = acc_ws[u] + jnp.dot(vt, p, preferred_element_type=F32)
    o_ws = [acc_w[:dh] * (1.0 / acc_w[dh:dh + 1]) for acc_w in acc_ws]

    for u, (j, g) in enumerate(units):
        gates = gate_ref[j * tq:(j + 1) * tq, :].T
        outs = []
        for h in range(NSA_HPG):
            sl = slice(h * tq, (h + 1) * tq)
            gate = lambda branch, h=h: gates[branch * NSA_HEADS + g * NSA_HPG + h:branch * NSA_HEADS + g * NSA_HPG + h + 1, :]
            outs.append(gate(0) * o_cs[u][:, sl] + gate(1) * o_ss[u][:, sl] + gate(2) * o_ws[u][:, sl])
        y = jnp.concatenate(outs, axis=0)
        o_ref[j * tq:(j + 1) * tq, g * gw:(g + 1) * gw] = y.T.astype(o_ref.dtype)


def _overlap_matrix(s):
    nc = s // CMP_STRIDE
    nsb = s // SEL_LEN
    ci = np.arange(nc)[None, :] * CMP_STRIDE
    sj = np.arange(nsb)[:, None] * SEL_LEN
    ovl = ((ci < sj + SEL_LEN) & (ci + CMP_LEN > sj)).astype(np.float32)
    ovl[:, (s - CMP_LEN) // CMP_STRIDE + 1:] = 0.0
    return jnp.asarray(ovl)


def _nsa_attention(nqt, ng, kcmp, vcmp, ksd, vst, kwd, vwt, b, s):
    tk = tq = KEY_TILE
    nsub = KEY_PAIR // tq
    tstep = nsub * tq
    nq = s // tstep
    g_ = NSA_KV_GROUPS
    qw = NSA_HEADS * HEAD_DIM
    kw_ = g_ * 2 * HEAD_DIM
    vrows = g_ * (HEAD_DIM + ONES_ROWS)
    ncp = s // CMP_STRIDE
    nsb = s // SEL_LEN
    seq3 = lambda a: a.reshape(b, s, a.shape[-1])
    vt4 = lambda a: a.reshape(b, -1, a.shape[-2], a.shape[-1])
    k_spec = pl.BlockSpec((None, s, kw_), lambda bi, i: (bi, 0, 0))
    vt_spec = lambda tile: pl.BlockSpec((None, s // tile, vrows, tile), lambda bi, i: (bi, 0, 0, 0))
    cmp_spec = pl.BlockSpec((None, ncp, kw_), lambda bi, i: (bi, 0, 0))
    n_units = g_ * nsub
    per_unit = lambda dt: pltpu.VMEM((n_units, nsb, tq), dt)
    return pl.pallas_call(
        functools.partial(_nsa_kernel, tq=tq, tk=tk, nsub=nsub),
        grid=(b, nq),
        in_specs=[
            _const_spec((nsb, ncp)),
            pl.BlockSpec((qw, tstep), lambda bi, i: (0, bi * nq + i)),
            pl.BlockSpec((None, tstep, LANES), lambda bi, i: (bi, i, 0)),
            cmp_spec, cmp_spec, k_spec, vt_spec(KEY_PAIR), k_spec, vt_spec(KEY_TILE),
        ],
        out_specs=pl.BlockSpec((None, tstep, qw), lambda bi, i: (bi, i, 0)),
        out_shape=jax.ShapeDtypeStruct((b, s, qw), BF16),
        scratch_shapes=[per_unit(F32), per_unit(F32), per_unit(jnp.int32)] + _flash_scratch(tk, NSA_HPG * tq, HEAD_DIM) * n_units,
        compiler_params=_params(2),
        name="nsa_attn",
    )(_overlap_matrix(s), nqt, seq3(ng), kcmp, vcmp, seq3(ksd), vt4(vst), seq3(kwd), vt4(vwt))


def _merge_kernel(x_ref, g_ref, ya_ref, yb_ref, wmg_ref, wa_ref, wb_ref, wo_ref, o_ref):
    x = x_ref[...]
    d = x.shape[1]
    ms = jnp.mean(x * x, axis=-1, keepdims=True)
    u = (x * lax.rsqrt(ms + NORM_EPS) * g_ref[...]).astype(BF16)
    ya = jnp.dot(ya_ref[...], wa_ref[...], preferred_element_type=F32)
    yb = jnp.dot(yb_ref[...], wb_ref[...], preferred_element_type=F32)
    mg_a = jax.nn.sigmoid(jnp.dot(u, wmg_ref[:, :d], preferred_element_type=F32))
    mg_b = jax.nn.sigmoid(jnp.dot(u, wmg_ref[:, d:], preferred_element_type=F32))
    merged = (mg_a * ya + mg_b * yb).astype(BF16)
    o_ref[...] = x + jnp.dot(merged, wo_ref[...], preferred_element_type=F32)


def _merge(x2, g, ya, yb, w_mg, w_a, w_b, w_o):
    t, d = x2.shape
    tm = min(t, 512)
    ws = [w.astype(BF16) for w in (w_mg, w_a, w_b, w_o)]
    return pl.pallas_call(
        _merge_kernel,
        grid=(t // tm,),
        in_specs=[
            pl.BlockSpec((tm, d), lambda i: (i, 0)),
            _const_spec((1, d)),
            pl.BlockSpec((tm, ya.shape[1]), lambda i: (i, 0)),
            pl.BlockSpec((tm, yb.shape[1]), lambda i: (i, 0)),
        ] + [_const_spec(w.shape) for w in ws],
        out_specs=pl.BlockSpec((tm, d), lambda i: (i, 0)),
        out_shape=jax.ShapeDtypeStruct((t, d), F32),
        compiler_params=_params(1),
        name="merge_out_proj",
    )(x2, g[None, :], ya, yb, *ws)


def _ffn_kernel(h_ref, g_ref, gf_ref, wg_ref, wu_ref, wd_ref, o_ref, *, chunk, final_norm):
    h = h_ref[...]
    ms = jnp.mean(h * h, axis=-1, keepdims=True)
    u = (h * lax.rsqrt(ms + NORM_EPS) * g_ref[...]).astype(BF16)
    d_ff = wg_ref.shape[1]
    acc = h
    for c0 in range(0, d_ff, chunk):
        c1 = min(c0 + chunk, d_ff)
        a = jnp.dot(u, wg_ref[:, c0:c1], preferred_element_type=F32)
        up = jnp.dot(u, wu_ref[:, c0:c1], preferred_element_type=F32)
        act = (a * jax.nn.sigmoid(a) * up).astype(BF16)
        acc = acc + jnp.dot(act, wd_ref[c0:c1, :], preferred_element_type=F32)
    if final_norm:
        ms = jnp.mean(acc * acc, axis=-1, keepdims=True)
        acc = acc * lax.rsqrt(ms + NORM_EPS) * gf_ref[...]
    o_ref[...] = acc


def _ffn(h2, g, g_final, w_gate, w_up, w_down, final_norm):
    t, d = h2.shape
    tm = min(t, 512)
    d_ff = w_gate.shape[1]
    chunk = 256
    ws = [w.astype(BF16) for w in (w_gate, w_up, w_down)]
    return pl.pallas_call(
        functools.partial(_ffn_kernel, chunk=chunk, final_norm=final_norm),
        grid=(t // tm,),
        in_specs=[pl.BlockSpec((tm, d), lambda i: (i, 0)), _const_spec((1, d)), _const_spec((1, d))]
        + [_const_spec(w.shape) for w in ws],
        out_specs=pl.BlockSpec((tm, d), lambda i: (i, 0)),
        out_shape=jax.ShapeDtypeStruct((t, d), F32),
        compiler_params=_params(1),
        name="swiglu_ffn",
    )(h2, g[None, :], g_final[None, :], *ws)


def kernel(x, positions, attn_norm_g, w_in, diff_lambda, diff_subln_g, cmp_pe_k, cmp_pe_v, cmp_k_w1, cmp_k_w2, cmp_v_w1, cmp_v_w2, w_branch_a, w_branch_b, w_out, ffn_norm_g, w_gate, w_up, w_down, final_norm_g):
    b, s, d = x.shape
    depth = w_in.shape[0]
    cos, sin = _rope_tables(positions)
    h = x.reshape(b * s, d)
    gates_off = w_in.shape[2] - 2 * d
    for layer in range(depth):
        w_in_b = w_in[layer].astype(BF16)
        p = _project(h, attn_norm_g[layer], cos, sin, w_in_b[:, :gates_off])
        kcmp = _compress(p["kc"], cmp_pe_k[layer], cmp_k_w1[layer], cmp_k_w2[layer], b, s)
        vcmp = _compress(p["vc"], cmp_pe_v[layer], cmp_v_w1[layer], cmp_v_w2[layer], b, s)
        ya = _diff_attention(p["dqT"], p["dk"], p["dvT"], diff_lambda[layer], diff_subln_g[layer], b, s, layer)
        yb = _nsa_attention(p["nqT"], p["ng"], kcmp, vcmp, p["ksd"], p["vsT"], p["kwd"], p["vwT"], b, s)
        h = _merge(h, attn_norm_g[layer], ya.reshape(b * s, -1), yb.reshape(b * s, -1),
                   w_in_b[:, gates_off:], w_branch_a[layer], w_branch_b[layer], w_out[layer])
        h = _ffn(h, ffn_norm_g[layer], final_norm_g, w_gate[layer], w_up[layer], w_down[layer], layer == depth - 1)
    return h.reshape(b, s, d)
```

```python
import functools
import math
from typing import Callable, NamedTuple, Sequence

import jax
import jax.numpy as jnp
import numpy as np
from jax import lax
from jax.experimental import pallas as pl
from jax.experimental.pallas import tpu as pltpu

HEAD_DIM = 64
ROPE_THETA = 10000.0
NORM_EPS = 1e-6
NEG_INF = -1e30
FORCE_SCORE = 1e9

DIFF_HEADS = 4
DIFF_VDIM = 2 * HEAD_DIM
NSA_HEADS = 8
NSA_KV_GROUPS = 2
NSA_HPG = NSA_HEADS // NSA_KV_GROUPS
CMP_LEN = 32
CMP_STRIDE = 16
SEL_LEN = 64
SEL_TOPK = 16
WINDOW = 512

LANES = 128
VMEM_LIMIT = 56 * 1024 * 1024
KEY_TILE = 256

BF16 = jnp.bfloat16
F32 = jnp.float32


def _lambda_init(layer):
    return 0.8 - 0.6 * math.exp(-0.3 * layer)


def _params(n_axes):
    return pltpu.CompilerParams(dimension_semantics=("arbitrary",) * n_axes, vmem_limit_bytes=VMEM_LIMIT)


def _const_spec(shape):
    nd = len(shape)
    return pl.BlockSpec(shape, lambda *_: (0,) * nd)


def _rope_table_kernel(pos_ref, invf_ref, cos_ref, sin_ref):
    ang = pos_ref[...].astype(F32) * invf_ref[...]
    cos_ref[...] = jnp.cos(ang)
    sin_ref[...] = jnp.sin(ang)


def _rope_tables(positions):
    half = HEAD_DIM // 2
    per_row = LANES // half
    t = positions.size
    rows = t // per_row
    pos_e = jnp.repeat(positions.reshape(rows, per_row), half, axis=1)
    inv_freq = 1.0 / (ROPE_THETA ** (jnp.arange(0, HEAD_DIM, 2, dtype=F32) / HEAD_DIM))
    invf = jnp.tile(inv_freq, per_row)[None, :]
    tr = min(rows, 1024)
    cos, sin = pl.pallas_call(
        _rope_table_kernel,
        grid=(rows // tr,),
        in_specs=[pl.BlockSpec((tr, LANES), lambda i: (i, 0)), _const_spec((1, LANES))],
        out_specs=[pl.BlockSpec((tr, LANES), lambda i: (i, 0))] * 2,
        out_shape=[jax.ShapeDtypeStruct((rows, LANES), F32)] * 2,
        compiler_params=_params(1),
        name="rope_tables",
    )(pos_e, invf)
    return cos, sin


_ROW_SEGS = (("dk", 512, BF16, True), ("ksd", 128, BF16, True), ("kwd", 128, BF16, True), ("kc", 128, F32, True),
             ("vc", 128, F32, False), ("ng", 128, F32, False))
_DUP_SEGS = ("ksd", "kwd")
Q_SCALE = HEAD_DIM ** -0.5 * math.log2(math.e)
KEY_PAIR = 2 * KEY_TILE
ONES_ROWS = 16
_COL_SEGS = (("dqT", 512, True, None, None), ("nqT", 512, True, None, None),
             ("dvT", 512, False, KEY_PAIR, DIFF_VDIM), ("vsT", 128, False, KEY_PAIR, HEAD_DIM), ("vwT", 128, False, KEY_TILE, HEAD_DIM))
_STRIDED_SEGS = ("kc", "vc")


def _proj_kernel(x_ref, g_ref, cos_ref, sin_ref, wr_ref, wt_ref, *out_refs):
    x = x_ref[...]
    ms = jnp.mean(x * x, axis=-1, keepdims=True)
    u = (x * lax.rsqrt(ms + NORM_EPS) * g_ref[...]).astype(BF16)
    half = HEAD_DIM // 2
    reps = LANES // half

    def rope_fn(c, s, axis):
        idx = lax.broadcasted_iota(jnp.int32, c.shape, axis)
        upper = (idx % HEAD_DIM) >= half
        s_up = jnp.where(upper, s, 0.0)
        s_lo = jnp.where(upper, 0.0, -s)
        return lambda y: y * c + pltpu.roll(y, half, axis) * s_up + pltpu.roll(y, LANES - half, axis) * s_lo

    *refs, stage_ref = out_refs

    def expand(t_ref):
        t4 = t_ref[...]
        rolled = [t4] + [pltpu.roll(t4, half * m, 1) for m in range(1, reps)]
        slot = lax.broadcasted_iota(jnp.int32, t4.shape, 1) // half
        for j in range(reps):
            ej = rolled[-j % reps]
            for k in range(1, reps):
                ej = jnp.where(slot == k, rolled[(k - j) % reps], ej)
            stage_ref[pl.ds(j, t4.shape[0], stride=reps), :] = ej
        return stage_ref[...]

    c = expand(cos_ref)
    s = expand(sin_ref)
    rope_rows = rope_fn(c, s, 1)
    rope_cols = rope_fn(c.T, s.T, 0)

    tm = x.shape[0]
    pieces = []
    for name, width, _, roped in _ROW_SEGS:
        o_ref = refs.pop(0)
        pieces += [(name, o_ref, c, roped) for c in range(0, width, LANES)]
    for p0 in range(0, len(pieces), 2):
        pair = pieces[p0:p0 + 2]
        y = jnp.dot(u, wr_ref[:, p0 * LANES:(p0 + len(pair)) * LANES], preferred_element_type=F32)
        for i, (name, o_ref, c, roped) in enumerate(pair):
            yl = y[:, i * LANES:(i + 1) * LANES]
            if roped:
                yl = rope_rows(yl)
            if name == "ng":
                yl = jax.nn.sigmoid(yl)
            if name in _DUP_SEGS:
                other = pltpu.roll(yl, HEAD_DIM, 1)
                first = lax.broadcasted_iota(jnp.int32, yl.shape, 1) < HEAD_DIM
                o_ref[:, :LANES] = jnp.where(first, yl, other).astype(o_ref.dtype)
                o_ref[:, LANES:] = jnp.where(first, other, yl).astype(o_ref.dtype)
            elif name in _STRIDED_SEGS:
                stage_ref[...] = yl
                for l in range(CMP_STRIDE):
                    o_ref[:, l * LANES:(l + 1) * LANES] = stage_ref[pl.ds(l, tm // CMP_STRIDE, stride=CMP_STRIDE), :]
            else:
                o_ref[:, c:c + LANES] = yl.astype(o_ref.dtype)

    off = 0
    ut = u.T
    for name, rows, roped, tile, group in _COL_SEGS:
        o_ref = refs.pop(0)
        for r0 in range(0, rows, 2 * LANES):
            rw = min(2 * LANES, rows - r0)
            yt = jnp.dot(wt_ref[off + r0:off + r0 + rw, :], ut, preferred_element_type=F32)
            for s0 in range(0, rw, LANES):
                ys = yt[s0:s0 + LANES]
                if roped:
                    ys = rope_cols(ys) * Q_SCALE
                ys = ys.astype(o_ref.dtype)
                if not tile:
                    o_ref[r0 + s0:r0 + s0 + LANES, :] = ys
                    continue
                for g0 in range(0, LANES, group):
                    dst = (r0 + s0 + g0) // group * (group + ONES_ROWS)
                    for j in range(tm // tile):
                        o_ref[j, dst:dst + group, :] = ys[g0:g0 + group, j * tile:(j + 1) * tile]
                        o_ref[j, dst + group:dst + group + ONES_ROWS, :] = jnp.ones((ONES_ROWS, tile), o_ref.dtype)
        off += rows


def _project(x2, g, cos, sin, w_in):
    t, d = x2.shape
    qk = DIFF_HEADS * 2 * HEAD_DIM
    kv = NSA_KV_GROUPS * HEAD_DIM
    splits = (qk, qk, DIFF_HEADS * DIFF_VDIM, NSA_HEADS * HEAD_DIM, kv, kv, kv, kv, kv, kv, 3 * NSA_HEADS)
    offs = np.cumsum((0,) + splits)
    dq, dk, dv, nq, kc, vc, ks, vs, kw, vw, ng = (w_in[:, offs[i]:offs[i + 1]] for i in range(len(splits)))
    ng = jnp.pad(ng, ((0, 0), (0, LANES - ng.shape[1])))
    wr = jnp.concatenate([dk, ks, kw, kc, vc, ng], axis=1)
    wt = jnp.concatenate([dq, nq, dv, vs, vw], axis=1).T
    tm = min(t, 512)
    assert tm % KEY_TILE == 0
    half = HEAD_DIM // 2
    out_specs, out_shape = [], []
    for name, w, dt, _ in _ROW_SEGS:
        fold = CMP_STRIDE if name in _STRIDED_SEGS else 1
        w_out = w * fold * (2 if name in _DUP_SEGS else 1)
        out_specs.append(pl.BlockSpec((tm // fold, w_out), lambda i: (i, 0)))
        out_shape.append(jax.ShapeDtypeStruct((t // fold, w_out), dt))
    for _, rows, _, tile, group in _COL_SEGS:
        if tile:
            assert tm % tile == 0
            padded = rows // group * (group + ONES_ROWS)
            out_specs.append(pl.BlockSpec((tm // tile, padded, tile), lambda i: (i, 0, 0)))
            out_shape.append(jax.ShapeDtypeStruct((t // tile, padded, tile), BF16))
        else:
            out_specs.append(pl.BlockSpec((rows, tm), lambda i: (0, i)))
            out_shape.append(jax.ShapeDtypeStruct((rows, t), BF16))
    outs = pl.pallas_call(
        _proj_kernel,
        grid=(t // tm,),
        in_specs=[
            pl.BlockSpec((tm, d), lambda i: (i, 0)),
            _const_spec((1, d)),
            pl.BlockSpec((tm * half // LANES, LANES), lambda i: (i, 0)),
            pl.BlockSpec((tm * half // LANES, LANES), lambda i: (i, 0)),
            _const_spec(wr.shape),
            _const_spec(wt.shape),
        ],
        out_specs=out_specs,
        out_shape=out_shape,
        scratch_shapes=[pltpu.VMEM((tm, LANES), F32)],
        compiler_params=_params(1),
        name="in_proj",
    )(x2, g[None, :], cos, sin, wr, wt)
    return dict(zip([n for n, *_ in _ROW_SEGS] + [n for n, *_ in _COL_SEGS], outs))


def _compress_kernel(x_ref, pea_ref, peb_ref, w1a_ref, w1b_ref, w2_ref, o_ref):
    x = x_ref[...]
    a = jnp.dot((x + pea_ref[...]).astype(BF16), w1a_ref[...], preferred_element_type=F32)
    b = jnp.dot((x + peb_ref[...]).astype(BF16), w1b_ref[...], preferred_element_type=F32)
    ncp = x.shape[0]
    h = a + pltpu.roll(b, ncp - 1, 0)
    h = h * jax.nn.sigmoid(h)
    o_ref[...] = jnp.dot(h.astype(BF16), w2_ref[...], preferred_element_type=F32).astype(o_ref.dtype)


def _compress(xc, pe, w1, w2, b, s):
    g_, dh = NSA_KV_GROUPS, HEAD_DIM
    ncp = s // CMP_STRIDE
    hid = w1.shape[1]
    halves = CMP_LEN // CMP_STRIDE
    assert halves == 2
    xr = xc.reshape(b, ncp, CMP_STRIDE * g_ * dh)
    eye = jnp.eye(g_, dtype=F32)
    pe_r = pe.reshape(halves, CMP_STRIDE, 1, dh)
    pe_t = jnp.broadcast_to(pe_r, (halves, CMP_STRIDE, g_, dh)).reshape(halves, 1, CMP_STRIDE * g_ * dh)
    w1r = w1.reshape(halves, CMP_STRIDE, dh, hid)
    w1bd = jnp.einsum("hldj,pg->hlpdgj", w1r, eye).reshape(halves, CMP_STRIDE * g_ * dh, g_ * hid).astype(BF16)
    w2bd = jnp.einsum("jd,pg,r->pjgrd", w2, eye, jnp.ones((2,), F32)).reshape(g_ * hid, g_ * 2 * dh).astype(BF16)
    kdim = CMP_STRIDE * g_ * dh
    return pl.pallas_call(
        _compress_kernel,
        grid=(b,),
        in_specs=[
            pl.BlockSpec((None, ncp, kdim), lambda i: (i, 0, 0)),
            _const_spec((1, kdim)),
            _const_spec((1, kdim)),
            _const_spec((kdim, g_ * hid)),
            _const_spec((kdim, g_ * hid)),
            _const_spec((g_ * hid, g_ * 2 * dh)),
        ],
        out_specs=pl.BlockSpec((None, ncp, g_ * 2 * dh), lambda i: (i, 0, 0)),
        out_shape=jax.ShapeDtypeStruct((b, ncp, g_ * 2 * dh), BF16),
        compiler_params=_params(1),
        name="compress",
    )(xr, pe_t[0], pe_t[1], w1bd[0], w1bd[1], w2bd)


def _split_heads(qt):
    row = lax.broadcasted_iota(jnp.int32, qt.shape, 0)
    zero = jnp.zeros_like(qt)
    return jnp.concatenate([jnp.where(row < HEAD_DIM, qt, zero), jnp.where(row >= HEAD_DIM, qt, zero)], axis=1)


def _flash_scratch(tk, m_cols, dv):
    row = pltpu.VMEM((1, m_cols), F32)
    sc = pltpu.VMEM((tk, m_cols), F32)
    return [sc, sc, row, row, pltpu.VMEM((2 * tk, m_cols), BF16), row, row, pltpu.VMEM((dv + ONES_ROWS, m_cols), F32)]


class _Stream(NamedTuple):
    qst: jax.Array
    load_k: Callable
    load_vt: Callable
    scratch: Sequence
    mask_fn: Callable
    inner_fn: Callable


def _flash_tiles(streams, tk, n_k, hi, first_step):
    def scores(st, t, slot, masked):
        s_ref, t_ref = st.scratch[slot], st.scratch[2 + slot]
        k = st.load_k(pl.multiple_of(jnp.clip(t, 0, n_k - 1) * tk, tk))
        s = (st.mask_fn if masked else st.inner_fn)(t, jnp.dot(k, st.qst, preferred_element_type=F32))
        t_ref[...] = jnp.max(s, axis=0, keepdims=True)
        s_ref[...] = s

    def probs(st):
        sa_ref, sb_ref, ta_ref, tb_ref, p_ref, al_ref, m_ref, _ = st.scratch
        m_old = m_ref[...]
        m_new = jnp.maximum(m_old, jnp.maximum(ta_ref[...], tb_ref[...]))
        p_ref[:tk, :] = jnp.exp2(sa_ref[...] - m_new).astype(BF16)
        p_ref[tk:, :] = jnp.exp2(sb_ref[...] - m_new).astype(BF16)
        al_ref[...] = jnp.exp2(m_old - m_new)
        m_ref[...] = m_new

    def values(st, pair):
        p_ref, al_ref, acc_ref = st.scratch[4], st.scratch[5], st.scratch[7]
        vt = st.load_vt(jnp.clip(pair, 0, n_k // 2 - 1))
        acc_ref[...] = al_ref[...] * acc_ref[...] + jnp.dot(vt, p_ref[...], preferred_element_type=F32)

    def each(fn, *args):
        for st in streams:
            fn(st, *args)

    def init(st):
        m_ref, al_ref = st.scratch[6], st.scratch[5]
        m_ref[...] = jnp.full(m_ref.shape, NEG_INF, F32)
        al_ref[...] = jnp.ones(al_ref.shape, F32)

    @pl.when(first_step)
    def _():
        for st in streams:
            st.scratch[4][...] = jnp.zeros(st.scratch[4].shape, BF16)
            st.scratch[7][...] = jnp.zeros(st.scratch[7].shape, F32)

    n_pairs = hi // 2 + 1

    def pair_at(r):
        return jnp.where(r == 0, n_pairs - 1, r - 1)

    each(init)
    each(scores, 2 * pair_at(0), 0, True)
    each(scores, 2 * pair_at(0) + 1, 1, True)

    def body(r, carry):
        each(values, pair_at(r - 1))
        each(probs)
        each(scores, 2 * r, 0, False)
        each(scores, 2 * r + 1, 1, False)
        return carry

    lax.fori_loop(0, n_pairs - 1, body, 0)
    each(values, pair_at(n_pairs - 2))
    each(probs)
    each(values, pair_at(n_pairs - 1))
    out = []
    for st in streams:
        acc = st.scratch[7][...]
        dv = acc.shape[0] - ONES_ROWS
        out.append((acc[dv:dv + 1], acc[:dv]))
    return out


def _diff_kernel(lam_ref, g_ref, qt_ref, k_ref, vt_ref, o_ref, *scratch, tq, tk, heads, lam_init):
    qi = pl.program_id(2)
    m_cols = 2 * tq
    qpos = qi * tq + lax.broadcasted_iota(jnp.int32, (1, m_cols), 1) % tq
    krel = lax.broadcasted_iota(jnp.int32, (tk, m_cols), 0)
    hw = 2 * HEAD_DIM
    vrows = DIFF_VDIM + ONES_ROWS
    n_sc = len(scratch) // heads

    def causal(t, s):
        return jnp.where(krel <= qpos - t * tk, s, NEG_INF)

    streams = [
        _Stream(
            qst=_split_heads(qt_ref[h * hw:(h + 1) * hw, :]),
            load_k=lambda row, h=h: k_ref[pl.ds(row, tk), h * hw:(h + 1) * hw],
            load_vt=lambda pair, h=h: vt_ref[pair, h * vrows:(h + 1) * vrows, :],
            scratch=scratch[h * n_sc:(h + 1) * n_sc],
            mask_fn=causal,
            inner_fn=lambda t, s: s,
        )
        for h in range(heads)
    ]
    first_step = (pl.program_id(0) == 0) & (pl.program_id(1) == 0) & (qi == 0)
    results = _flash_tiles(streams, tk, k_ref.shape[0] // tk, (qi * tq + tq - 1) // tk, first_step)
    lf = lam_ref[...]
    lam = (jnp.exp(jnp.sum(lf[0:1] * lf[1:2], axis=1, keepdims=True))
           - jnp.exp(jnp.sum(lf[2:3] * lf[3:4], axis=1, keepdims=True)) + lam_init)
    for h, (l, acc) in enumerate(results):
        o = acc * (1.0 / l)
        y = o[:, :tq] - lam * o[:, tq:]
        y = y * lax.rsqrt(jnp.mean(y * y, axis=0, keepdims=True) + NORM_EPS)
        y = y.T * g_ref[...] * (1.0 - lam_init)
        o_ref[:, h * DIFF_VDIM:(h + 1) * DIFF_VDIM] = y.astype(o_ref.dtype)


def _diff_attention(dqt, dk, dvt, diff_lambda, subln_g, b, s, layer):
    tk = KEY_TILE
    tq = 2 * tk
    heads = DIFF_HEADS
    hw = heads * 2 * HEAD_DIM
    vrows = heads * (DIFF_VDIM + ONES_ROWS)
    nq = s // tq
    return pl.pallas_call(
        functools.partial(_diff_kernel, tq=tq, tk=tk, heads=heads, lam_init=_lambda_init(layer)),
        grid=(b, DIFF_HEADS // heads, nq),
        in_specs=[
            _const_spec(diff_lambda.shape),
            _const_spec((1, DIFF_VDIM)),
            pl.BlockSpec((hw, tq), lambda bi, h, i: (h, bi * nq + i)),
            pl.BlockSpec((None, s, hw), lambda bi, h, i: (bi, 0, h)),
            pl.BlockSpec((None, s // KEY_PAIR, vrows, KEY_PAIR), lambda bi, h, i: (bi, 0, h, 0)),
        ],
        out_specs=pl.BlockSpec((None, tq, heads * DIFF_VDIM), lambda bi, h, i: (bi, i, h)),
        out_shape=jax.ShapeDtypeStruct((b, s, DIFF_HEADS * DIFF_VDIM), BF16),
        scratch_shapes=_flash_scratch(tk, 2 * tq, DIFF_VDIM) * heads,
        compiler_params=_params(3),
        name="diff_attn",
    )(diff_lambda, subln_g[None, :], dqt, dk.reshape(b, s, -1), dvt.reshape(b, s // KEY_PAIR, -1, KEY_PAIR))


def _nsa_kernel(ovl_ref, qt_ref, gate_ref, kc_ref, vc_ref, ks_ref, vst_ref, kw_ref, vwt_ref, o_ref, imp_ref, selb_ref, cnt_ref, *scratch, tq, tk, nsub):
    step = pl.program_id(1)
    m_cols = NSA_HPG * tq
    dh = HEAD_DIM
    gw = NSA_HPG * dh
    kwid = 2 * dh
    vrows = dh + ONES_ROWS
    units = [(j, g) for j in range(nsub) for g in range(NSA_KV_GROUPS)]
    n_sc = len(scratch) // len(units)
    ncp = kc_ref.shape[0]
    nsb = imp_ref.shape[1]
    topk = min(SEL_TOPK, nsb)
    bpt = tk // SEL_LEN
    n_back = -(-(WINDOW - 1) // tk)
    krel = lax.broadcasted_iota(jnp.int32, (tk, m_cols), 0)
    blk = lax.broadcasted_iota(jnp.int32, (nsb, tq), 0)
    crow = lax.broadcasted_iota(jnp.int32, (ncp, m_cols), 0)

    q0s = [(step * nsub + j) * tq for j in range(nsub)]
    qpos_ts = [q0 + lax.broadcasted_iota(jnp.int32, (1, tq), 1) for q0 in q0s]
    qposs = [jnp.concatenate([qp] * NSA_HPG, axis=1) for qp in qpos_ts]
    cmasks = [crow <= lax.shift_right_arithmetic(qp - (CMP_LEN - 1), int(math.log2(CMP_STRIDE))) for qp in qposs]
    qblks = [qp // SEL_LEN for qp in qpos_ts]
    forceds = [(blk == 0) | (blk == qb) | (blk == qb - 1) for qb in qblks]

    qsts = []
    for j, g in units:
        qt = qt_ref[g * gw:(g + 1) * gw, j * tq:(j + 1) * tq]
        qsts.append(jnp.concatenate([_split_heads(qt[:LANES]), _split_heads(qt[LANES:])], axis=1))
    scs = [jnp.where(cmasks[j], jnp.dot(kc_ref[:, g * kwid:(g + 1) * kwid], qsts[u], preferred_element_type=F32), NEG_INF)
           for u, (j, g) in enumerate(units)]
    pcs = [jnp.exp2(sc - jnp.max(sc, axis=0, keepdims=True)) for sc in scs]
    pcs = [pc * jnp.where(qposs[j] >= CMP_LEN - 1, 1.0 / jnp.sum(pc, axis=0, keepdims=True), 0.0)
           for pc, (j, g) in zip(pcs, units)]
    o_cs = [lax.dot_general(vc_ref[:, g * kwid:(g + 1) * kwid], pcs[u].astype(BF16), (((0,), (0,)), ((), ())),
                            preferred_element_type=F32)[:dh] for u, (j, g) in enumerate(units)]
    for u, (j, g) in enumerate(units):
        pc_sum = functools.reduce(jnp.add, [pcs[u][:, h * tq:(h + 1) * tq] for h in range(NSA_HPG)])
        imp = jnp.dot(ovl_ref[...], pc_sum, preferred_element_type=F32, precision=lax.Precision.HIGHEST)
        imp = jnp.where(forceds[j], FORCE_SCORE, imp)
        imp_ref[u] = jnp.where(blk <= qblks[j], imp, NEG_INF)
        cnt_ref[u] = jnp.zeros((nsb, tq), jnp.int32)

    sub = 8
    for g0 in range(0, nsb, sub):
        for j in range(nsub):
            @pl.when(g0 <= (q0s[j] + tq - 1) // SEL_LEN)
            def _(g0=g0, j=j):
                for u in [u for u, (ju, _) in enumerate(units) if ju == j]:
                    cnts = [cnt_ref[u, b0:b0 + sub, :] for b0 in range(0, nsb, sub)]
                    for i in range(g0, g0 + sub):
                        row = jnp.broadcast_to(imp_ref[u, i:i + 1, :], (sub, tq))
                        for n, b0 in enumerate(range(0, nsb, sub)):
                            cur = imp_ref[u, b0:b0 + sub, :]
                            if b0 > i:
                                beats = row >= cur
                            elif b0 + sub - 1 < i:
                                beats = row > cur
                            else:
                                beats = (row > cur) | ((row == cur) & (lax.broadcasted_iota(jnp.int32, (sub, tq), 0) > i - b0))
                            cnts[n] = cnts[n] + jnp.where(beats, 1, 0)
                    for n, b0 in enumerate(range(0, nsb, sub)):
                        cnt_ref[u, b0:b0 + sub, :] = cnts[n]
    for u in range(len(units)):
        selb_ref[u] = jnp.where(cnt_ref[u] < topk, 0.0, NEG_INF)

    def biased(u):
        def fn(t, s):
            first_blk = jnp.clip(t, 0, nsb // bpt - 1) * bpt
            rows = []
            for r in range(bpt):
                brow = selb_ref[u, pl.ds(first_blk + r, 1), :]
                brow = jnp.concatenate([brow] * NSA_HPG, axis=1)
                rows.append(jnp.broadcast_to(brow, (SEL_LEN, m_cols)))
            return s + jnp.concatenate(rows, axis=0)
        return fn

    def selected(u, j):
        return lambda t, s: jnp.where(krel <= qposs[j] - t * tk, biased(u)(t, s), NEG_INF)

    streams = [
        _Stream(
            qst=qsts[u],
            load_k=lambda row, g=g: ks_ref[pl.ds(row, tk), g * kwid:(g + 1) * kwid],
            load_vt=lambda pair, g=g: vst_ref[pair, g * vrows:(g + 1) * vrows, :],
            scratch=scratch[u * n_sc:(u + 1) * n_sc],
            mask_fn=selected(u, j),
            inner_fn=biased(u),
        )
        for u, (j, g) in enumerate(units)
    ]
    last = (q0s[-1] + tq - 1) // tk
    sel_out = _flash_tiles(streams, tk, ks_ref.shape[0] // tk, last, (pl.program_id(0) == 0) & (step == 0))
    o_ss = [acc_s * (1.0 / l_s) for l_s, acc_s in sel_out]

    n_win = n_back + tq // tk
    win_tile = lambda j, r: (step * nsub + j) * (tq // tk) - n_back + r
    s_ws = [[] for _ in units]
    for r in range(n_win):
        d_min = (n_back - r) * tk - (tk - 1)
        d_max = (n_back - r) * tk + (tq - 1)
        for u, (j, g) in enumerate(units):
            t = win_tile(j, r)
            k = kw_ref[pl.ds(pl.multiple_of(jnp.maximum(t, 0) * tk, tk), tk), g * kwid:(g + 1) * kwid]
            s = jnp.dot(k, qsts[u], preferred_element_type=F32)
            if d_min < 0:
                s = jnp.where(krel <= qposs[j] - t * tk, s, NEG_INF)
            if d_max >= WINDOW:
                s = jnp.where(krel > qposs[j] - t * tk - WINDOW, s, NEG_INF)
            if r < n_back:
                s = s + jnp.where(t >= 0, 0.0, NEG_INF)
            s_ws[u].append(s)
    m_ws = [functools.reduce(jnp.maximum, [jnp.max(s, axis=0, keepdims=True) for s in s_w]) for s_w in s_ws]
    acc_ws = [jnp.zeros((vrows, m_cols), F32) for _ in units]
    for r in range(n_win):
        for u, (j, g) in enumerate(units):
            p = jnp.exp2(s_ws[u][r] - m_ws[u]).astype(BF16)
            vt = vwt_ref[jnp.maximum(win_tile(j, r), 0), g * vrows:(g + 1) * vrows, :]
            acc_ws[u] = acc_ws[u] + jnp.dot(vt, p, preferred_element_type=F32)
    o_ws = [acc_w[:dh] * (1.0 / acc_w[dh:dh + 1]) for acc_w in acc_ws]

    for u, (j, g) in enumerate(units):
        gates = gate_ref[j * tq:(j + 1) * tq, :].T
        outs = []
        for h in range(NSA_HPG):
            sl = slice(h * tq, (h + 1) * tq)
            gate = lambda branch, h=h: gates[branch * NSA_HEADS + g * NSA_HPG + h:branch * NSA_HEADS + g * NSA_HPG + h + 1, :]
            outs.append(gate(0) * o_cs[u][:, sl] + gate(1) * o_ss[u][:, sl] + gate(2) * o_ws[u][:, sl])
        y = jnp.concatenate(outs, axis=0)
        o_ref[j * tq:(j + 1) * tq, g * gw:(g + 1) * gw] = y.T.astype(o_ref.dtype)


def _overlap_matrix(s):
    nc = s // CMP_STRIDE
    nsb = s // SEL_LEN
    ci = np.arange(nc)[None, :] * CMP_STRIDE
    sj = np.arange(nsb)[:, None] * SEL_LEN
    ovl = ((ci < sj + SEL_LEN) & (ci + CMP_LEN > sj)).astype(np.float32)
    ovl[:, (s - CMP_LEN) // CMP_STRIDE + 1:] = 0.0
    return jnp.asarray(ovl)


def _nsa_attention(nqt, ng, kcmp, vcmp, ksd, vst, kwd, vwt, b, s):
    tk = tq = KEY_TILE
    nsub = KEY_PAIR // tq
    tstep = nsub * tq
    nq = s // tstep
    g_ = NSA_KV_GROUPS
    qw = NSA_HEADS * HEAD_DIM
    kw_ = g_ * 2 * HEAD_DIM
    vrows = g_ * (HEAD_DIM + ONES_ROWS)
    ncp = s // CMP_STRIDE
    nsb = s // SEL_LEN
    seq3 = lambda a: a.reshape(b, s, a.shape[-1])
    vt4 = lambda a: a.reshape(b, -1, a.shape[-2], a.shape[-1])
    k_spec = pl.BlockSpec((None, s, kw_), lambda bi, i: (bi, 0, 0))
    vt_spec = lambda tile: pl.BlockSpec((None, s // tile, vrows, tile), lambda bi, i: (bi, 0, 0, 0))
    cmp_spec = pl.BlockSpec((None, ncp, kw_), lambda bi, i: (bi, 0, 0))
    n_units = g_ * nsub
    per_unit = lambda dt: pltpu.VMEM((n_units, nsb, tq), dt)
    return pl.pallas_call(
        functools.partial(_nsa_kernel, tq=tq, tk=tk, nsub=nsub),
        grid=(b, nq),
        in_specs=[
            _const_spec((nsb, ncp)),
            pl.BlockSpec((qw, tstep), lambda bi, i: (0, bi * nq + i)),
            pl.BlockSpec((None, tstep, LANES), lambda bi, i: (bi, i, 0)),
            cmp_spec, cmp_spec, k_spec, vt_spec(KEY_PAIR), k_spec, vt_spec(KEY_TILE),
        ],
        out_specs=pl.BlockSpec((None, tstep, qw), lambda bi, i: (bi, i, 0)),
        out_shape=jax.ShapeDtypeStruct((b, s, qw), BF16),
        scratch_shapes=[per_unit(F32), per_unit(F32), per_unit(jnp.int32)] + _flash_scratch(tk, NSA_HPG * tq, HEAD_DIM) * n_units,
        compiler_params=_params(2),
        name="nsa_attn",
    )(_overlap_matrix(s), nqt, seq3(ng), kcmp, vcmp, seq3(ksd), vt4(vst), seq3(kwd), vt4(vwt))


def _merge_kernel(x_ref, g_ref, ya_ref, yb_ref, wmg_ref, wa_ref, wb_ref, wo_ref, o_ref):
    x = x_ref[...]
    d = x.shape[1]
    ms = jnp.mean(x * x, axis=-1, keepdims=True)
    u = (x * lax.rsqrt(ms + NORM_EPS) * g_ref[...]).astype(BF16)
    ya = jnp.dot(ya_ref[...], wa_ref[...], preferred_element_type=F32)
    yb = jnp.dot(yb_ref[...], wb_ref[...], preferred_element_type=F32)
    mg_a = jax.nn.sigmoid(jnp.dot(u, wmg_ref[:, :d], preferred_element_type=F32))
    mg_b = jax.nn.sigmoid(jnp.dot(u, wmg_ref[:, d:], preferred_element_type=F32))
    merged = (mg_a * ya + mg_b * yb).astype(BF16)
    o_ref[...] = x + jnp.dot(merged, wo_ref[...], preferred_element_type=F32)


def _merge(x2, g, ya, yb, w_mg, w_a, w_b, w_o):
    t, d = x2.shape
    tm = min(t, 512)
    ws = [w.astype(BF16) for w in (w_mg, w_a, w_b, w_o)]
    return pl.pallas_call(
        _merge_kernel,
        grid=(t // tm,),
        in_specs=[
            pl.BlockSpec((tm, d), lambda i: (i, 0)),
            _const_spec((1, d)),
            pl.BlockSpec((tm, ya.shape[1]), lambda i: (i, 0)),
            pl.BlockSpec((tm, yb.shape[1]), lambda i: (i, 0)),
        ] + [_const_spec(w.shape) for w in ws],
        out_specs=pl.BlockSpec((tm, d), lambda i: (i, 0)),
        out_shape=jax.ShapeDtypeStruct((t, d), F32),
        compiler_params=_params(1),
        name="merge_out_proj",
    )(x2, g[None, :], ya, yb, *ws)


def _ffn_kernel(h_ref, g_ref, gf_ref, wg_ref, wu_ref, wd_ref, o_ref, *, chunk, final_norm):
    h = h_ref[...]
    ms = jnp.mean(h * h, axis=-1, keepdims=True)
    u = (h * lax.rsqrt(ms + NORM_EPS) * g_ref[...]).astype(BF16)
    d_ff = wg_ref.shape[1]
    acc = h
    for c0 in range(0, d_ff, chunk):
        c1 = min(c0 + chunk, d_ff)
        a = jnp.dot(u, wg_ref[:, c0:c1], preferred_element_type=F32)
        up = jnp.dot(u, wu_ref[:, c0:c1], preferred_element_type=F32)
        act = (a * jax.nn.sigmoid(a) * up).astype(BF16)
        acc = acc + jnp.dot(act, wd_ref[c0:c1, :], preferred_element_type=F32)
    if final_norm:
        ms = jnp.mean(acc * acc, axis=-1, keepdims=True)
        acc = acc * lax.rsqrt(ms + NORM_EPS) * gf_ref[...]
    o_ref[...] = acc


def _ffn(h2, g, g_final, w_gate, w_up, w_down, final_norm):
    t, d = h2.shape
    tm = min(t, 512)
    d_ff = w_gate.shape[1]
    chunk = 256
    ws = [w.astype(BF16) for w in (w_gate, w_up, w_down)]
    return pl.pallas_call(
        functools.partial(_ffn_kernel, chunk=chunk, final_norm=final_norm),
        grid=(t // tm,),
        in_specs=[pl.BlockSpec((tm, d), lambda i: (i, 0)), _const_spec((1, d)), _const_spec((1, d))]
        + [_const_spec(w.shape) for w in ws],
        out_specs=pl.BlockSpec((tm, d), lambda i: (i, 0)),
        out_shape=jax.ShapeDtypeStruct((t, d), F32),
        compiler_params=_params(1),
        name="swiglu_ffn",
    )(h2, g[None, :], g_final[None, :], *ws)


def kernel(x, positions, attn_norm_g, w_in, diff_lambda, diff_subln_g, cmp_pe_k, cmp_pe_v, cmp_k_w1, cmp_k_w2, cmp_v_w1, cmp_v_w2, w_branch_a, w_branch_b, w_out, ffn_norm_g, w_gate, w_up, w_down, final_norm_g):
    b, s, d = x.shape
    depth = w_in.shape[0]
    cos, sin = _rope_tables(positions)
    h = x.reshape(b * s, d)
    gates_off = w_in.shape[2] - 2 * d
    for layer in range(depth):
        w_in_b = w_in[layer].astype(BF16)
        p = _project(h, attn_norm_g[layer], cos, sin, w_in_b[:, :gates_off])
        kcmp = _compress(p["kc"], cmp_pe_k[layer], cmp_k_w1[layer], cmp_k_w2[layer], b, s)
        vcmp = _compress(p["vc"], cmp_pe_v[layer], cmp_v_w1[layer], cmp_v_w2[layer], b, s)
        ya = _diff_attention(p["dqT"], p["dk"], p["dvT"], diff_lambda[layer], diff_subln_g[layer], b, s, layer)
        yb = _nsa_attention(p["nqT"], p["ng"], kcmp, vcmp, p["ksd"], p["vsT"], p["kwd"], p["vwT"], b, s)
        h = _merge(h, attn_norm_g[layer], ya.reshape(b * s, -1), yb.reshape(b * s, -1),
                   w_in_b[:, gates_off:], w_branch_a[layer], w_branch_b[layer], w_out[layer])
        h = _ffn(h, ffn_norm_g[layer], final_norm_g, w_gate[layer], w_up[layer], w_down[layer], layer == depth - 1)
    return h.reshape(b, s, d)
```

```python
import functools
import math
from typing import Callable, NamedTuple, Sequence

import jax
import jax.numpy as jnp
import numpy as np
from jax import lax
from jax.experimental import pallas as pl
from jax.experimental.pallas import tpu as pltpu

HEAD_DIM = 64
ROPE_THETA = 10000.0
NORM_EPS = 1e-6
NEG_INF = -1e30
FORCE_SCORE = 1e9

DIFF_HEADS = 4
DIFF_VDIM = 2 * HEAD_DIM
NSA_HEADS = 8
NSA_KV_GROUPS = 2
NSA_HPG = NSA_HEADS // NSA_KV_GROUPS
CMP_LEN = 32
CMP_STRIDE = 16
SEL_LEN = 64
SEL_TOPK = 16
WINDOW = 512

LANES = 128
VMEM_LIMIT = 56 * 1024 * 1024
KEY_TILE = 256
ROW_TILE = 512

BF16 = jnp.bfloat16
F32 = jnp.float32


def _lambda_init(layer):
    return 0.8 - 0.6 * math.exp(-0.3 * layer)


def _params(n_axes):
    return pltpu.CompilerParams(dimension_semantics=("arbitrary",) * n_axes, vmem_limit_bytes=VMEM_LIMIT)


def _const_spec(shape):
    nd = len(shape)
    return pl.BlockSpec(shape, lambda *_: (0,) * nd)


def _rope_table_kernel(pos_ref, invf_ref, cos_ref, sin_ref):
    ang = pos_ref[...].astype(F32) * invf_ref[...]
    cos_ref[...] = jnp.cos(ang)
    sin_ref[...] = jnp.sin(ang)


def _rope_tables(positions):
    half = HEAD_DIM // 2
    per_row = LANES // half
    t = positions.size
    rows = t // per_row
    pos_e = jnp.repeat(positions.reshape(rows, per_row), half, axis=1)
    inv_freq = 1.0 / (ROPE_THETA ** (jnp.arange(0, HEAD_DIM, 2, dtype=F32) / HEAD_DIM))
    invf = jnp.tile(inv_freq, per_row)[None, :]
    tr = min(rows, 1024)
    cos, sin = pl.pallas_call(
        _rope_table_kernel,
        grid=(rows // tr,),
        in_specs=[pl.BlockSpec((tr, LANES), lambda i: (i, 0)), _const_spec((1, LANES))],
        out_specs=[pl.BlockSpec((tr, LANES), lambda i: (i, 0))] * 2,
        out_shape=[jax.ShapeDtypeStruct((rows, LANES), F32)] * 2,
        compiler_params=_params(1),
        name="rope_tables",
    )(pos_e, invf)
    return cos, sin


_ROW_SEGS = (("dk", 512, BF16, True), ("ksd", 128, BF16, True), ("kwd", 128, BF16, True), ("kc", 128, F32, True),
             ("vc", 128, F32, False), ("ng", 128, F32, False))
_DUP_SEGS = ("ksd", "kwd")
Q_SCALE = HEAD_DIM ** -0.5 * math.log2(math.e)
KEY_PAIR = 2 * KEY_TILE
ONES_ROWS = 16
_COL_SEGS = (("dqT", 512, True, None, None), ("nqT", 512, True, None, None),
             ("dvT", 512, False, KEY_PAIR, DIFF_VDIM), ("vsT", 128, False, KEY_PAIR, HEAD_DIM), ("vwT", 128, False, KEY_TILE, HEAD_DIM))
_STRIDED_SEGS = ("kc", "vc")


def _proj_kernel(x_ref, g_ref, cos_ref, sin_ref, wr_ref, wt_ref, *out_refs):
    x = x_ref[...]
    ms = jnp.mean(x * x, axis=-1, keepdims=True)
    u = (x * lax.rsqrt(ms + NORM_EPS) * g_ref[...]).astype(BF16)
    half = HEAD_DIM // 2
    reps = LANES // half

    def rope_fn(c, s, axis):
        idx = lax.broadcasted_iota(jnp.int32, c.shape, axis)
        upper = (idx % HEAD_DIM) >= half
        s_up = jnp.where(upper, s, 0.0)
        s_lo = jnp.where(upper, 0.0, -s)
        return lambda y: y * c + pltpu.roll(y, half, axis) * s_up + pltpu.roll(y, LANES - half, axis) * s_lo

    *refs, stage_ref = out_refs

    def expand(t_ref):
        t4 = t_ref[...]
        rolled = [t4] + [pltpu.roll(t4, half * m, 1) for m in range(1, reps)]
        slot = lax.broadcasted_iota(jnp.int32, t4.shape, 1) // half
        for j in range(reps):
            ej = rolled[-j % reps]
            for k in range(1, reps):
                ej = jnp.where(slot == k, rolled[(k - j) % reps], ej)
            stage_ref[pl.ds(j, t4.shape[0], stride=reps), :] = ej
        return stage_ref[...]

    c = expand(cos_ref)
    s = expand(sin_ref)
    rope_rows = rope_fn(c, s, 1)
    rope_cols = rope_fn(c.T, s.T, 0)

    tm = x.shape[0]
    pieces = []
    for name, width, _, roped in _ROW_SEGS:
        o_ref = refs.pop(0)
        pieces += [(name, o_ref, c, roped) for c in range(0, width, LANES)]
    for p0 in range(0, len(pieces), 2):
        pair = pieces[p0:p0 + 2]
        y = jnp.dot(u, wr_ref[:, p0 * LANES:(p0 + len(pair)) * LANES], preferred_element_type=F32)
        for i, (name, o_ref, c, roped) in enumerate(pair):
            yl = y[:, i * LANES:(i + 1) * LANES]
            if roped:
                yl = rope_rows(yl)
            if name == "ng":
                yl = jax.nn.sigmoid(yl)
            if name in _DUP_SEGS:
                other = pltpu.roll(yl, HEAD_DIM, 1)
                first = lax.broadcasted_iota(jnp.int32, yl.shape, 1) < HEAD_DIM
                o_ref[:, :LANES] = jnp.where(first, yl, other).astype(o_ref.dtype)
                o_ref[:, LANES:] = jnp.where(first, other, yl).astype(o_ref.dtype)
            elif name in _STRIDED_SEGS:
                stage_ref[...] = yl
                for l in range(CMP_STRIDE):
                    o_ref[:, l * LANES:(l + 1) * LANES] = stage_ref[pl.ds(l, tm // CMP_STRIDE, stride=CMP_STRIDE), :]
            else:
                o_ref[:, c:c + LANES] = yl.astype(o_ref.dtype)

    ut = u.T
    pieces = []
    for name, rows, roped, tile, group in _COL_SEGS:
        o_ref = refs.pop(0)
        pieces += [(o_ref, r, roped, tile, group) for r in range(0, rows, LANES)]
    for p0 in range(0, len(pieces), 2):
        pair = pieces[p0:p0 + 2]
        yt = jnp.dot(wt_ref[p0 * LANES:(p0 + len(pair)) * LANES, :], ut, preferred_element_type=F32)
        for i, (o_ref, r, roped, tile, group) in enumerate(pair):
            ys = yt[i * LANES:(i + 1) * LANES]
            if roped:
                ys = rope_cols(ys) * Q_SCALE
            ys = ys.astype(o_ref.dtype)
            if not tile:
                o_ref[r:r + LANES, :] = ys
                continue
            for g0 in range(0, LANES, group):
                dst = (r + g0) // group * (group + ONES_ROWS)
                for j in range(tm // tile):
                    o_ref[j, dst:dst + group, :] = ys[g0:g0 + group, j * tile:(j + 1) * tile]
                    o_ref[j, dst + group:dst + group + ONES_ROWS, :] = jnp.ones((ONES_ROWS, tile), o_ref.dtype)


def _project(x2, g, cos, sin, w_in):
    t, d = x2.shape
    qk = DIFF_HEADS * 2 * HEAD_DIM
    kv = NSA_KV_GROUPS * HEAD_DIM
    splits = (qk, qk, DIFF_HEADS * DIFF_VDIM, NSA_HEADS * HEAD_DIM, kv, kv, kv, kv, kv, kv, 3 * NSA_HEADS)
    offs = np.cumsum((0,) + splits)
    dq, dk, dv, nq, kc, vc, ks, vs, kw, vw, ng = (w_in[:, offs[i]:offs[i + 1]] for i in range(len(splits)))
    ng = jnp.pad(ng, ((0, 0), (0, LANES - ng.shape[1])))
    wr = jnp.concatenate([dk, ks, kw, kc, vc, ng], axis=1)
    wt = jnp.concatenate([dq, nq, dv, vs, vw], axis=1).T
    tm = min(t, ROW_TILE)
    assert tm % KEY_TILE == 0
    half = HEAD_DIM // 2
    out_specs, out_shape = [], []
    for name, w, dt, _ in _ROW_SEGS:
        fold = CMP_STRIDE if name in _STRIDED_SEGS else 1
        w_out = w * fold * (2 if name in _DUP_SEGS else 1)
        out_specs.append(pl.BlockSpec((tm // fold, w_out), lambda i: (i, 0)))
        out_shape.append(jax.ShapeDtypeStruct((t // fold, w_out), dt))
    for _, rows, _, tile, group in _COL_SEGS:
        if tile:
            assert tm % tile == 0
            padded = rows // group * (group + ONES_ROWS)
            out_specs.append(pl.BlockSpec((tm // tile, padded, tile), lambda i: (i, 0, 0)))
            out_shape.append(jax.ShapeDtypeStruct((t // tile, padded, tile), BF16))
        else:
            out_specs.append(pl.BlockSpec((rows, tm), lambda i: (0, i)))
            out_shape.append(jax.ShapeDtypeStruct((rows, t), BF16))
    outs = pl.pallas_call(
        _proj_kernel,
        grid=(t // tm,),
        in_specs=[
            pl.BlockSpec((tm, d), lambda i: (i, 0)),
            _const_spec((1, d)),
            pl.BlockSpec((tm * half // LANES, LANES), lambda i: (i, 0)),
            pl.BlockSpec((tm * half // LANES, LANES), lambda i: (i, 0)),
            _const_spec(wr.shape),
            _const_spec(wt.shape),
        ],
        out_specs=out_specs,
        out_shape=out_shape,
        scratch_shapes=[pltpu.VMEM((tm, LANES), F32)],
        compiler_params=_params(1),
        name="in_proj",
    )(x2, g[None, :], cos, sin, wr, wt)
    return dict(zip([n for n, *_ in _ROW_SEGS] + [n for n, *_ in _COL_SEGS], outs))


def _compress_kernel(x_ref, pea_ref, peb_ref, w1a_ref, w1b_ref, w2_ref, o_ref):
    x = x_ref[...]
    a = jnp.dot((x + pea_ref[...]).astype(BF16), w1a_ref[...], preferred_element_type=F32)
    b = jnp.dot((x + peb_ref[...]).astype(BF16), w1b_ref[...], preferred_element_type=F32)
    ncp = x.shape[0]
    h = a + pltpu.roll(b, ncp - 1, 0)
    h = h * jax.nn.sigmoid(h)
    o_ref[...] = jnp.dot(h.astype(BF16), w2_ref[...], preferred_element_type=F32).astype(o_ref.dtype)


def _compress(xc, pe, w1, w2, b, s):
    g_, dh = NSA_KV_GROUPS, HEAD_DIM
    ncp = s // CMP_STRIDE
    hid = w1.shape[1]
    halves = CMP_LEN // CMP_STRIDE
    assert halves == 2
    xr = xc.reshape(b, ncp, CMP_STRIDE * g_ * dh)
    eye = jnp.eye(g_, dtype=F32)
    pe_r = pe.reshape(halves, CMP_STRIDE, 1, dh)
    pe_t = jnp.broadcast_to(pe_r, (halves, CMP_STRIDE, g_, dh)).reshape(halves, 1, CMP_STRIDE * g_ * dh)
    w1r = w1.reshape(halves, CMP_STRIDE, dh, hid)
    w1bd = jnp.einsum("hldj,pg->hlpdgj", w1r, eye).reshape(halves, CMP_STRIDE * g_ * dh, g_ * hid).astype(BF16)
    w2bd = jnp.einsum("jd,pg,r->pjgrd", w2, eye, jnp.ones((2,), F32)).reshape(g_ * hid, g_ * 2 * dh).astype(BF16)
    kdim = CMP_STRIDE * g_ * dh
    return pl.pallas_call(
        _compress_kernel,
        grid=(b,),
        in_specs=[
            pl.BlockSpec((None, ncp, kdim), lambda i: (i, 0, 0)),
            _const_spec((1, kdim)),
            _const_spec((1, kdim)),
            _const_spec((kdim, g_ * hid)),
            _const_spec((kdim, g_ * hid)),
            _const_spec((g_ * hid, g_ * 2 * dh)),
        ],
        out_specs=pl.BlockSpec((None, ncp, g_ * 2 * dh), lambda i: (i, 0, 0)),
        out_shape=jax.ShapeDtypeStruct((b, ncp, g_ * 2 * dh), BF16),
        compiler_params=_params(1),
        name="compress",
    )(xr, pe_t[0], pe_t[1], w1bd[0], w1bd[1], w2bd)


def _split_heads(qt):
    row = lax.broadcasted_iota(jnp.int32, qt.shape, 0)
    zero = jnp.zeros_like(qt)
    return jnp.concatenate([jnp.where(row < HEAD_DIM, qt, zero), jnp.where(row >= HEAD_DIM, qt, zero)], axis=1)


def _flash_scratch(tk, m_cols, dv):
    row = pltpu.VMEM((1, m_cols), F32)
    sc = pltpu.VMEM((tk, m_cols), F32)
    return [sc, sc, row, row, pltpu.VMEM((2 * tk, m_cols), BF16), row, row, pltpu.VMEM((dv + ONES_ROWS, m_cols), F32)]


class _Stream(NamedTuple):
    qst: jax.Array
    load_k: Callable
    load_vt: Callable
    scratch: Sequence
    mask_fn: Callable
    inner_fn: Callable


def _flash_tiles(streams, tk, n_k, hi, first_step):
    def scores(st, t, slot, masked):
        s_ref, t_ref = st.scratch[slot], st.scratch[2 + slot]
        k = st.load_k(pl.multiple_of(jnp.clip(t, 0, n_k - 1) * tk, tk))
        s = (st.mask_fn if masked else st.inner_fn)(t, jnp.dot(k, st.qst, preferred_element_type=F32))
        t_ref[...] = jnp.max(s, axis=0, keepdims=True)
        s_ref[...] = s

    def probs(st):
        sa_ref, sb_ref, ta_ref, tb_ref, p_ref, al_ref, m_ref, _ = st.scratch
        m_old = m_ref[...]
        m_new = jnp.maximum(m_old, jnp.maximum(ta_ref[...], tb_ref[...]))
        p_ref[:tk, :] = jnp.exp2(sa_ref[...] - m_new).astype(BF16)
        p_ref[tk:, :] = jnp.exp2(sb_ref[...] - m_new).astype(BF16)
        al_ref[...] = jnp.exp2(m_old - m_new)
        m_ref[...] = m_new

    def values(st, pair):
        p_ref, al_ref, acc_ref = st.scratch[4], st.scratch[5], st.scratch[7]
        vt = st.load_vt(jnp.clip(pair, 0, n_k // 2 - 1))
        acc_ref[...] = al_ref[...] * acc_ref[...] + jnp.dot(vt, p_ref[...], preferred_element_type=F32)

    def each(fn, *args):
        for st in streams:
            fn(st, *args)

    def init(st):
        m_ref, al_ref = st.scratch[6], st.scratch[5]
        m_ref[...] = jnp.full(m_ref.shape, NEG_INF, F32)
        al_ref[...] = jnp.ones(al_ref.shape, F32)

    @pl.when(first_step)
    def _():
        for st in streams:
            st.scratch[4][...] = jnp.zeros(st.scratch[4].shape, BF16)
            st.scratch[7][...] = jnp.zeros(st.scratch[7].shape, F32)

    n_pairs = hi // 2 + 1

    def pair_at(r):
        return jnp.where(r == 0, n_pairs - 1, r - 1)

    each(init)
    each(scores, 2 * pair_at(0), 0, True)
    each(scores, 2 * pair_at(0) + 1, 1, True)

    def body(r, carry):
        each(values, pair_at(r - 1))
        each(probs)
        each(scores, 2 * r, 0, False)
        each(scores, 2 * r + 1, 1, False)
        return carry

    lax.fori_loop(0, n_pairs - 1, body, 0)
    each(values, pair_at(n_pairs - 2))
    each(probs)
    each(values, pair_at(n_pairs - 1))
    out = []
    for st in streams:
        acc = st.scratch[7][...]
        dv = acc.shape[0] - ONES_ROWS
        out.append((acc[dv:dv + 1], acc[:dv]))
    return out


def _diff_kernel(lam_ref, g_ref, qt_ref, k_ref, vt_ref, o_ref, *scratch, tq, tk, heads, lam_init):
    qi = pl.program_id(2)
    m_cols = 2 * tq
    qpos = qi * tq + lax.broadcasted_iota(jnp.int32, (1, m_cols), 1) % tq
    krel = lax.broadcasted_iota(jnp.int32, (tk, m_cols), 0)
    hw = 2 * HEAD_DIM
    vrows = DIFF_VDIM + ONES_ROWS
    n_sc = len(scratch) // heads

    def causal(t, s):
        return jnp.where(krel <= qpos - t * tk, s, NEG_INF)

    streams = [
        _Stream(
            qst=_split_heads(qt_ref[h * hw:(h + 1) * hw, :]),
            load_k=lambda row, h=h: k_ref[pl.ds(row, tk), h * hw:(h + 1) * hw],
            load_vt=lambda pair, h=h: vt_ref[pair, h * vrows:(h + 1) * vrows, :],
            scratch=scratch[h * n_sc:(h + 1) * n_sc],
            mask_fn=causal,
            inner_fn=lambda t, s: s,
        )
        for h in range(heads)
    ]
    first_step = (pl.program_id(0) == 0) & (pl.program_id(1) == 0) & (qi == 0)
    results = _flash_tiles(streams, tk, k_ref.shape[0] // tk, (qi * tq + tq - 1) // tk, first_step)
    lf = lam_ref[...]
    lam = (jnp.exp(jnp.sum(lf[0:1] * lf[1:2], axis=1, keepdims=True))
           - jnp.exp(jnp.sum(lf[2:3] * lf[3:4], axis=1, keepdims=True)) + lam_init)
    for h, (l, acc) in enumerate(results):
        o = acc * (1.0 / l)
        y = o[:, :tq] - lam * o[:, tq:]
        y = y * lax.rsqrt(jnp.mean(y * y, axis=0, keepdims=True) + NORM_EPS)
        y = y.T * g_ref[...] * (1.0 - lam_init)
        o_ref[:, h * DIFF_VDIM:(h + 1) * DIFF_VDIM] = y.astype(o_ref.dtype)


def _diff_attention(dqt, dk, dvt, diff_lambda, subln_g, b, s, layer):
    tk = KEY_TILE
    tq = 2 * tk
    heads = DIFF_HEADS
    hw = heads * 2 * HEAD_DIM
    vrows = heads * (DIFF_VDIM + ONES_ROWS)
    nq = s // tq
    return pl.pallas_call(
        functools.partial(_diff_kernel, tq=tq, tk=tk, heads=heads, lam_init=_lambda_init(layer)),
        grid=(b, DIFF_HEADS // heads, nq),
        in_specs=[
            _const_spec(diff_lambda.shape),
            _const_spec((1, DIFF_VDIM)),
            pl.BlockSpec((hw, tq), lambda bi, h, i: (h, bi * nq + i)),
            pl.BlockSpec((None, s, hw), lambda bi, h, i: (bi, 0, h)),
            pl.BlockSpec((None, s // KEY_PAIR, vrows, KEY_PAIR), lambda bi, h, i: (bi, 0, h, 0)),
        ],
        out_specs=pl.BlockSpec((None, tq, heads * DIFF_VDIM), lambda bi, h, i: (bi, i, h)),
        out_shape=jax.ShapeDtypeStruct((b, s, DIFF_HEADS * DIFF_VDIM), BF16),
        scratch_shapes=_flash_scratch(tk, 2 * tq, DIFF_VDIM) * heads,
        compiler_params=_params(3),
        name="diff_attn",
    )(diff_lambda, subln_g[None, :], dqt, dk.reshape(b, s, -1), dvt.reshape(b, s // KEY_PAIR, -1, KEY_PAIR))


def _nsa_kernel(ovl_ref, qt_ref, gate_ref, kc_ref, vc_ref, ks_ref, vst_ref, kw_ref, vwt_ref, o_ref, imp_ref, selb_ref, cnt_ref, *scratch, tq, tk, nsub):
    step = pl.program_id(1)
    m_cols = NSA_HPG * tq
    dh = HEAD_DIM
    gw = NSA_HPG * dh
    kwid = 2 * dh
    vrows = dh + ONES_ROWS
    units = [(j, g) for j in range(nsub) for g in range(NSA_KV_GROUPS)]
    n_sc = len(scratch) // len(units)
    ncp = kc_ref.shape[0]
    nsb = imp_ref.shape[1]
    topk = min(SEL_TOPK, nsb)
    bpt = tk // SEL_LEN
    n_back = -(-(WINDOW - 1) // tk)
    krel = lax.broadcasted_iota(jnp.int32, (tk, m_cols), 0)
    blk = lax.broadcasted_iota(jnp.int32, (nsb, tq), 0)
    crow = lax.broadcasted_iota(jnp.int32, (ncp, m_cols), 0)

    q0s = [(step * nsub + j) * tq for j in range(nsub)]
    qpos_ts = [q0 + lax.broadcasted_iota(jnp.int32, (1, tq), 1) for q0 in q0s]
    qposs = [jnp.concatenate([qp] * NSA_HPG, axis=1) for qp in qpos_ts]
    cmasks = [crow <= lax.shift_right_arithmetic(qp - (CMP_LEN - 1), int(math.log2(CMP_STRIDE))) for qp in qposs]
    qblks = [qp // SEL_LEN for qp in qpos_ts]
    forceds = [(blk == 0) | (blk == qb) | (blk == qb - 1) for qb in qblks]

    qsts = []
    for j, g in units:
        qt = qt_ref[g * gw:(g + 1) * gw, j * tq:(j + 1) * tq]
        qsts.append(jnp.concatenate([_split_heads(qt[:LANES]), _split_heads(qt[LANES:])], axis=1))
    scs = [jnp.where(cmasks[j], jnp.dot(kc_ref[:, g * kwid:(g + 1) * kwid], qsts[u], preferred_element_type=F32), NEG_INF)
           for u, (j, g) in enumerate(units)]
    pcs = [jnp.exp2(sc - jnp.max(sc, axis=0, keepdims=True)) for sc in scs]
    pcs = [pc * jnp.where(qposs[j] >= CMP_LEN - 1, 1.0 / jnp.sum(pc, axis=0, keepdims=True), 0.0)
           for pc, (j, g) in zip(pcs, units)]
    o_cs = [lax.dot_general(vc_ref[:, g * kwid:(g + 1) * kwid], pcs[u].astype(BF16), (((0,), (0,)), ((), ())),
                            preferred_element_type=F32)[:dh] for u, (j, g) in enumerate(units)]
    for u, (j, g) in enumerate(units):
        pc_sum = functools.reduce(jnp.add, [pcs[u][:, h * tq:(h + 1) * tq] for h in range(NSA_HPG)])
        imp = jnp.dot(ovl_ref[...], pc_sum, preferred_element_type=F32, precision=lax.Precision.HIGHEST)
        imp = jnp.where(forceds[j], FORCE_SCORE, imp)
        imp_ref[u] = jnp.where(blk <= qblks[j], imp, NEG_INF)
        cnt_ref[u] = jnp.zeros((nsb, tq), jnp.int32)

    sub = 8
    for g0 in range(0, nsb, sub):
        for j in range(nsub):
            @pl.when(g0 <= (q0s[j] + tq - 1) // SEL_LEN)
            def _(g0=g0, j=j):
                for u in [u for u, (ju, _) in enumerate(units) if ju == j]:
                    cnts = [cnt_ref[u, b0:b0 + sub, :] for b0 in range(0, nsb, sub)]
                    for i in range(g0, g0 + sub):
                        row = jnp.broadcast_to(imp_ref[u, i:i + 1, :], (sub, tq))
                        for n, b0 in enumerate(range(0, nsb, sub)):
                            cur = imp_ref[u, b0:b0 + sub, :]
                            if b0 > i:
                                beats = row >= cur
                            elif b0 + sub - 1 < i:
                                beats = row > cur
                            else:
                                beats = (row > cur) | ((row == cur) & (lax.broadcasted_iota(jnp.int32, (sub, tq), 0) > i - b0))
                            cnts[n] = cnts[n] + jnp.where(beats, 1, 0)
                    for n, b0 in enumerate(range(0, nsb, sub)):
                        cnt_ref[u, b0:b0 + sub, :] = cnts[n]
    for u in range(len(units)):
        selb_ref[u] = jnp.where(cnt_ref[u] < topk, 0.0, NEG_INF)

    def biased(u):
        def fn(t, s):
            first_blk = jnp.clip(t, 0, nsb // bpt - 1) * bpt
            rows = []
            for r in range(bpt):
                brow = selb_ref[u, pl.ds(first_blk + r, 1), :]
                brow = jnp.concatenate([brow] * NSA_HPG, axis=1)
                rows.append(jnp.broadcast_to(brow, (SEL_LEN, m_cols)))
            return s + jnp.concatenate(rows, axis=0)
        return fn

    def selected(u, j):
        return lambda t, s: jnp.where(krel <= qposs[j] - t * tk, biased(u)(t, s), NEG_INF)

    streams = [
        _Stream(
            qst=qsts[u],
            load_k=lambda row, g=g: ks_ref[pl.ds(row, tk), g * kwid:(g + 1) * kwid],
            load_vt=lambda pair, g=g: vst_ref[pair, g * vrows:(g + 1) * vrows, :],
            scratch=scratch[u * n_sc:(u + 1) * n_sc],
            mask_fn=selected(u, j),
            inner_fn=biased(u),
        )
        for u, (j, g) in enumerate(units)
    ]
    last = (q0s[-1] + tq - 1) // tk
    sel_out = _flash_tiles(streams, tk, ks_ref.shape[0] // tk, last, (pl.program_id(0) == 0) & (step == 0))
    o_ss = [acc_s * (1.0 / l_s) for l_s, acc_s in sel_out]

    n_win = n_back + tq // tk
    win_tile = lambda j, r: (step * nsub + j) * (tq // tk) - n_back + r
    s_ws = [[] for _ in units]
    for r in range(n_win):
        d_min = (n_back - r) * tk - (tk - 1)
        d_max = (n_back - r) * tk + (tq - 1)
        for u, (j, g) in enumerate(units):
            t = win_tile(j, r)
            k = kw_ref[pl.ds(pl.multiple_of(jnp.maximum(t, 0) * tk, tk), tk), g * kwid:(g + 1) * kwid]
            s = jnp.dot(k, qsts[u], preferred_element_type=F32)
            if d_min < 0:
                s = jnp.where(krel <= qposs[j] - t * tk, s, NEG_INF)
            if d_max >= WINDOW:
                s = jnp.where(krel > qposs[j] - t * tk - WINDOW, s, NEG_INF)
            if r < n_back:
                s = s + jnp.where(t >= 0, 0.0, NEG_INF)
            s_ws[u].append(s)
    m_ws = [functools.reduce(jnp.maximum, [jnp.max(s, axis=0, keepdims=True) for s in s_w]) for s_w in s_ws]
    acc_ws = [jnp.zeros((vrows, m_cols), F32) for _ in units]
    for r in range(n_win):
        for u, (j, g) in enumerate(units):
            p = jnp.exp2(s_ws[u][r] - m_ws[u]).astype(BF16)
            vt = vwt_ref[jnp.maximum(win_tile(j, r), 0), g * vrows:(g + 1) * vrows, :]
            acc_ws[u] = acc_ws[u] + jnp.dot(vt, p, preferred_element_type=F32)
    o_ws = [acc_w[:dh] * (1.0 / acc_w[dh:dh + 1]) for acc_w in acc_ws]

    for u, (j, g) in enumerate(units):
        gates = gate_ref[j * tq:(j + 1) * tq, :].T
        outs = []
        for h in range(NSA_HPG):
            sl = slice(h * tq, (h + 1) * tq)
            gate = lambda branch, h=h: gates[branch * NSA_HEADS + g * NSA_HPG + h:branch * NSA_HEADS + g * NSA_HPG + h + 1, :]
            outs.append(gate(0) * o_cs[u][:, sl] + gate(1) * o_ss[u][:, sl] + gate(2) * o_ws[u][:, sl])
        y = jnp.concatenate(outs, axis=0)
        o_ref[j * tq:(j + 1) * tq, g * gw:(g + 1) * gw] = y.T.astype(o_ref.dtype)


def _overlap_matrix(s):
    nc = s // CMP_STRIDE
    nsb = s // SEL_LEN
    ci = np.arange(nc)[None, :] * CMP_STRIDE
    sj = np.arange(nsb)[:, None] * SEL_LEN
    ovl = ((ci < sj + SEL_LEN) & (ci + CMP_LEN > sj)).astype(np.float32)
    ovl[:, (s - CMP_LEN) // CMP_STRIDE + 1:] = 0.0
    return jnp.asarray(ovl)


def _nsa_attention(nqt, ng, kcmp, vcmp, ksd, vst, kwd, vwt, b, s):
    tk = tq = KEY_TILE
    nsub = KEY_PAIR // tq
    tstep = nsub * tq
    nq = s // tstep
    g_ = NSA_KV_GROUPS
    qw = NSA_HEADS * HEAD_DIM
    kw_ = g_ * 2 * HEAD_DIM
    vrows = g_ * (HEAD_DIM + ONES_ROWS)
    ncp = s // CMP_STRIDE
    nsb = s // SEL_LEN
    seq3 = lambda a: a.reshape(b, s, a.shape[-1])
    vt4 = lambda a: a.reshape(b, -1, a.shape[-2], a.shape[-1])
    k_spec = pl.BlockSpec((None, s, kw_), lambda bi, i: (bi, 0, 0))
    vt_spec = lambda tile: pl.BlockSpec((None, s // tile, vrows, tile), lambda bi, i: (bi, 0, 0, 0))
    cmp_spec = pl.BlockSpec((None, ncp, kw_), lambda bi, i: (bi, 0, 0))
    n_units = g_ * nsub
    per_unit = lambda dt: pltpu.VMEM((n_units, nsb, tq), dt)
    return pl.pallas_call(
        functools.partial(_nsa_kernel, tq=tq, tk=tk, nsub=nsub),
        grid=(b, nq),
        in_specs=[
            _const_spec((nsb, ncp)),
            pl.BlockSpec((qw, tstep), lambda bi, i: (0, bi * nq + i)),
            pl.BlockSpec((None, tstep, LANES), lambda bi, i: (bi, i, 0)),
            cmp_spec, cmp_spec, k_spec, vt_spec(KEY_PAIR), k_spec, vt_spec(KEY_TILE),
        ],
        out_specs=pl.BlockSpec((None, tstep, qw), lambda bi, i: (bi, i, 0)),
        out_shape=jax.ShapeDtypeStruct((b, s, qw), BF16),
        scratch_shapes=[per_unit(F32), per_unit(F32), per_unit(jnp.int32)] + _flash_scratch(tk, NSA_HPG * tq, HEAD_DIM) * n_units,
        compiler_params=_params(2),
        name="nsa_attn",
    )(_overlap_matrix(s), nqt, seq3(ng), kcmp, vcmp, seq3(ksd), vt4(vst), seq3(kwd), vt4(vwt))


def _merge_kernel(x_ref, g_ref, ya_ref, yb_ref, wmg_ref, wa_ref, wb_ref, wo_ref, o_ref):
    x = x_ref[...]
    d = x.shape[1]
    ms = jnp.mean(x * x, axis=-1, keepdims=True)
    u = (x * lax.rsqrt(ms + NORM_EPS) * g_ref[...]).astype(BF16)
    ya = jnp.dot(ya_ref[...], wa_ref[...], preferred_element_type=F32)
    yb = jnp.dot(yb_ref[...], wb_ref[...], preferred_element_type=F32)
    mg_a = jax.nn.sigmoid(jnp.dot(u, wmg_ref[:, :d], preferred_element_type=F32))
    mg_b = jax.nn.sigmoid(jnp.dot(u, wmg_ref[:, d:], preferred_element_type=F32))
    merged = (mg_a * ya + mg_b * yb).astype(BF16)
    o_ref[...] = x + jnp.dot(merged, wo_ref[...], preferred_element_type=F32)


def _merge(x2, g, ya, yb, w_mg, w_a, w_b, w_o):
    t, d = x2.shape
    tm = min(t, ROW_TILE)
    ws = [w.astype(BF16) for w in (w_mg, w_a, w_b, w_o)]
    return pl.pallas_call(
        _merge_kernel,
        grid=(t // tm,),
        in_specs=[
            pl.BlockSpec((tm, d), lambda i: (i, 0)),
            _const_spec((1, d)),
            pl.BlockSpec((tm, ya.shape[1]), lambda i: (i, 0)),
            pl.BlockSpec((tm, yb.shape[1]), lambda i: (i, 0)),
        ] + [_const_spec(w.shape) for w in ws],
        out_specs=pl.BlockSpec((tm, d), lambda i: (i, 0)),
        out_shape=jax.ShapeDtypeStruct((t, d), F32),
        compiler_params=_params(1),
        name="merge_out_proj",
    )(x2, g[None, :], ya, yb, *ws)


def _ffn_kernel(h_ref, g_ref, gf_ref, wg_ref, wu_ref, wd_ref, o_ref, *, chunk, final_norm):
    h = h_ref[...]
    ms = jnp.mean(h * h, axis=-1, keepdims=True)
    u = (h * lax.rsqrt(ms + NORM_EPS) * g_ref[...]).astype(BF16)
    d_ff = wg_ref.shape[1]
    acc = h
    for c0 in range(0, d_ff, chunk):
        c1 = min(c0 + chunk, d_ff)
        a = jnp.dot(u, wg_ref[:, c0:c1], preferred_element_type=F32)
        up = jnp.dot(u, wu_ref[:, c0:c1], preferred_element_type=F32)
        act = (a * jax.nn.sigmoid(a) * up).astype(BF16)
        acc = acc + jnp.dot(act, wd_ref[c0:c1, :], preferred_element_type=F32)
    if final_norm:
        ms = jnp.mean(acc * acc, axis=-1, keepdims=True)
        acc = acc * lax.rsqrt(ms + NORM_EPS) * gf_ref[...]
    o_ref[...] = acc


def _ffn(h2, g, g_final, w_gate, w_up, w_down, final_norm):
    t, d = h2.shape
    tm = min(t, ROW_TILE)
    d_ff = w_gate.shape[1]
    chunk = 256
    ws = [w.astype(BF16) for w in (w_gate, w_up, w_down)]
    return pl.pallas_call(
        functools.partial(_ffn_kernel, chunk=chunk, final_norm=final_norm),
        grid=(t // tm,),
        in_specs=[pl.BlockSpec((tm, d), lambda i: (i, 0)), _const_spec((1, d)), _const_spec((1, d))]
        + [_const_spec(w.shape) for w in ws],
        out_specs=pl.BlockSpec((tm, d), lambda i: (i, 0)),
        out_shape=jax.ShapeDtypeStruct((t, d), F32),
        compiler_params=_params(1),
        name="swiglu_ffn",
    )(h2, g[None, :], g_final[None, :], *ws)


def kernel(x, positions, attn_norm_g, w_in, diff_lambda, diff_subln_g, cmp_pe_k, cmp_pe_v, cmp_k_w1, cmp_k_w2, cmp_v_w1, cmp_v_w2, w_branch_a, w_branch_b, w_out, ffn_norm_g, w_gate, w_up, w_down, final_norm_g):
    b, s, d = x.shape
    depth = w_in.shape[0]
    cos, sin = _rope_tables(positions)
    h = x.reshape(b * s, d)
    gates_off = w_in.shape[2] - 2 * d
    for layer in range(depth):
        w_in_b = w_in[layer].astype(BF16)
        p = _project(h, attn_norm_g[layer], cos, sin, w_in_b[:, :gates_off])
        kcmp = _compress(p["kc"], cmp_pe_k[layer], cmp_k_w1[layer], cmp_k_w2[layer], b, s)
        vcmp = _compress(p["vc"], cmp_pe_v[layer], cmp_v_w1[layer], cmp_v_w2[layer], b, s)
        ya = _diff_attention(p["dqT"], p["dk"], p["dvT"], diff_lambda[layer], diff_subln_g[layer], b, s, layer)
        yb = _nsa_attention(p["nqT"], p["ng"], kcmp, vcmp, p["ksd"], p["vsT"], p["kwd"], p["vwT"], b, s)
        h = _merge(h, attn_norm_g[layer], ya.reshape(b * s, -1), yb.reshape(b * s, -1),
                   w_in_b[:, gates_off:], w_branch_a[layer], w_branch_b[layer], w_out[layer])
        h = _ffn(h, ffn_norm_g[layer], final_norm_g, w_gate[layer], w_up[layer], w_down[layer], layer == depth - 1)
    return h.reshape(b, s, d)
```

```python
import functools
import math
from typing import Callable, NamedTuple, Sequence

import jax
import jax.numpy as jnp
import numpy as np
from jax import lax
from jax.experimental import pallas as pl
from jax.experimental.pallas import tpu as pltpu

HEAD_DIM = 64
ROPE_THETA = 10000.0
NORM_EPS = 1e-6
NEG_INF = -1e30
FORCE_SCORE = 1e9

DIFF_HEADS = 4
DIFF_VDIM = 2 * HEAD_DIM
NSA_HEADS = 8
NSA_KV_GROUPS = 2
NSA_HPG = NSA_HEADS // NSA_KV_GROUPS
CMP_LEN = 32
CMP_STRIDE = 16
SEL_LEN = 64
SEL_TOPK = 16
WINDOW = 512

LANES = 128
VMEM_LIMIT = 56 * 1024 * 1024
KEY_TILE = 256
ROW_TILE = 512

BF16 = jnp.bfloat16
F32 = jnp.float32


def _lambda_init(layer):
    return 0.8 - 0.6 * math.exp(-0.3 * layer)


def _params(n_axes):
    return pltpu.CompilerParams(dimension_semantics=("arbitrary",) * n_axes, vmem_limit_bytes=VMEM_LIMIT)


def _const_spec(shape):
    nd = len(shape)
    return pl.BlockSpec(shape, lambda *_: (0,) * nd)


def _rope_table_kernel(pos_ref, invf_ref, cos_ref, sin_ref):
    ang = pos_ref[...].astype(F32) * invf_ref[...]
    cos_ref[...] = jnp.cos(ang)
    sin_ref[...] = jnp.sin(ang)


def _rope_tables(positions):
    half = HEAD_DIM // 2
    per_row = LANES // half
    t = positions.size
    rows = t // per_row
    pos_e = jnp.repeat(positions.reshape(rows, per_row), half, axis=1)
    inv_freq = 1.0 / (ROPE_THETA ** (jnp.arange(0, HEAD_DIM, 2, dtype=F32) / HEAD_DIM))
    invf = jnp.tile(inv_freq, per_row)[None, :]
    tr = min(rows, 1024)
    cos, sin = pl.pallas_call(
        _rope_table_kernel,
        grid=(rows // tr,),
        in_specs=[pl.BlockSpec((tr, LANES), lambda i: (i, 0)), _const_spec((1, LANES))],
        out_specs=[pl.BlockSpec((tr, LANES), lambda i: (i, 0))] * 2,
        out_shape=[jax.ShapeDtypeStruct((rows, LANES), F32)] * 2,
        compiler_params=_params(1),
        name="rope_tables",
    )(pos_e, invf)
    return cos, sin


_ROW_SEGS = (("dk", 512, BF16, True), ("ksd", 128, BF16, True), ("kwd", 128, BF16, True), ("kc", 128, F32, True),
             ("vc", 128, F32, False), ("ng", 128, F32, False))
_DUP_SEGS = ("ksd", "kwd")
Q_SCALE = HEAD_DIM ** -0.5 * math.log2(math.e)
KEY_PAIR = 2 * KEY_TILE
ONES_ROWS = 16
_COL_SEGS = (("dqT", 512, True, None, None), ("nqT", 512, True, None, None),
             ("dvT", 512, False, KEY_PAIR, DIFF_VDIM), ("vsT", 128, False, KEY_PAIR, HEAD_DIM), ("vwT", 128, False, KEY_TILE, HEAD_DIM))
_STRIDED_SEGS = ("kc", "vc")


def _proj_kernel(x_ref, g_ref, cos_ref, sin_ref, wr_ref, wt_ref, *out_refs):
    x = x_ref[...]
    ms = jnp.mean(x * x, axis=-1, keepdims=True)
    u = (x * lax.rsqrt(ms + NORM_EPS) * g_ref[...]).astype(BF16)
    half = HEAD_DIM // 2
    reps = LANES // half

    def rope_fn(c, s, axis):
        idx = lax.broadcasted_iota(jnp.int32, c.shape, axis)
        upper = (idx % HEAD_DIM) >= half
        s_up = jnp.where(upper, s, 0.0)
        s_lo = jnp.where(upper, 0.0, -s)
        return lambda y: y * c + pltpu.roll(y, half, axis) * s_up + pltpu.roll(y, LANES - half, axis) * s_lo

    *refs, stage_ref = out_refs

    def expand(t_ref):
        t4 = t_ref[...]
        rolled = [t4] + [pltpu.roll(t4, half * m, 1) for m in range(1, reps)]
        slot = lax.broadcasted_iota(jnp.int32, t4.shape, 1) // half
        for j in range(reps):
            ej = rolled[-j % reps]
            for k in range(1, reps):
                ej = jnp.where(slot == k, rolled[(k - j) % reps], ej)
            stage_ref[pl.ds(j, t4.shape[0], stride=reps), :] = ej
        return stage_ref[...]

    c = expand(cos_ref)
    s = expand(sin_ref)
    rope_rows = rope_fn(c, s, 1)
    rope_cols = rope_fn(c.T, s.T, 0)

    tm = x.shape[0]
    pieces = []
    for name, width, _, roped in _ROW_SEGS:
        o_ref = refs.pop(0)
        pieces += [(name, o_ref, c, roped) for c in range(0, width, LANES)]
    for p0 in range(0, len(pieces), 2):
        pair = pieces[p0:p0 + 2]
        y = jnp.dot(u, wr_ref[:, p0 * LANES:(p0 + len(pair)) * LANES], preferred_element_type=F32)
        for i, (name, o_ref, c, roped) in enumerate(pair):
            yl = y[:, i * LANES:(i + 1) * LANES]
            if roped:
                yl = rope_rows(yl)
            if name == "ng":
                yl = jax.nn.sigmoid(yl)
            if name in _DUP_SEGS:
                other = pltpu.roll(yl, HEAD_DIM, 1)
                first = lax.broadcasted_iota(jnp.int32, yl.shape, 1) < HEAD_DIM
                o_ref[:, :LANES] = jnp.where(first, yl, other).astype(o_ref.dtype)
                o_ref[:, LANES:] = jnp.where(first, other, yl).astype(o_ref.dtype)
            elif name in _STRIDED_SEGS:
                stage_ref[...] = yl
                for l in range(CMP_STRIDE):
                    o_ref[:, l * LANES:(l + 1) * LANES] = stage_ref[pl.ds(l, tm // CMP_STRIDE, stride=CMP_STRIDE), :]
            else:
                o_ref[:, c:c + LANES] = yl.astype(o_ref.dtype)

    ut = u.T
    pieces = []
    for name, rows, roped, tile, group in _COL_SEGS:
        o_ref = refs.pop(0)
        pieces += [(o_ref, r, roped, tile, group) for r in range(0, rows, LANES)]
    for p0 in range(0, len(pieces), 2):
        pair = pieces[p0:p0 + 2]
        yt = jnp.dot(wt_ref[p0 * LANES:(p0 + len(pair)) * LANES, :], ut, preferred_element_type=F32)
        for i, (o_ref, r, roped, tile, group) in enumerate(pair):
            ys = yt[i * LANES:(i + 1) * LANES]
            if roped:
                ys = rope_cols(ys) * Q_SCALE
            ys = ys.astype(o_ref.dtype)
            if not tile:
                o_ref[r:r + LANES, :] = ys
                continue
            for g0 in range(0, LANES, group):
                dst = (r + g0) // group * (group + ONES_ROWS)
                for j in range(tm // tile):
                    o_ref[j, dst:dst + group, :] = ys[g0:g0 + group, j * tile:(j + 1) * tile]
                    o_ref[j, dst + group:dst + group + ONES_ROWS, :] = jnp.ones((ONES_ROWS, tile), o_ref.dtype)


def _project(x2, g, cos, sin, w_in):
    t, d = x2.shape
    qk = DIFF_HEADS * 2 * HEAD_DIM
    kv = NSA_KV_GROUPS * HEAD_DIM
    splits = (qk, qk, DIFF_HEADS * DIFF_VDIM, NSA_HEADS * HEAD_DIM, kv, kv, kv, kv, kv, kv, 3 * NSA_HEADS)
    offs = np.cumsum((0,) + splits)
    dq, dk, dv, nq, kc, vc, ks, vs, kw, vw, ng = (w_in[:, offs[i]:offs[i + 1]] for i in range(len(splits)))
    ng = jnp.pad(ng, ((0, 0), (0, LANES - ng.shape[1])))
    wr = jnp.concatenate([dk, ks, kw, kc, vc, ng], axis=1)
    wt = jnp.concatenate([dq, nq, dv, vs, vw], axis=1).T
    tm = min(t, ROW_TILE)
    assert tm % KEY_TILE == 0
    half = HEAD_DIM // 2
    out_specs, out_shape = [], []
    for name, w, dt, _ in _ROW_SEGS:
        fold = CMP_STRIDE if name in _STRIDED_SEGS else 1
        w_out = w * fold * (2 if name in _DUP_SEGS else 1)
        out_specs.append(pl.BlockSpec((tm // fold, w_out), lambda i: (i, 0)))
        out_shape.append(jax.ShapeDtypeStruct((t // fold, w_out), dt))
    for _, rows, _, tile, group in _COL_SEGS:
        if tile:
            assert tm % tile == 0
            padded = rows // group * (group + ONES_ROWS)
            out_specs.append(pl.BlockSpec((tm // tile, padded, tile), lambda i: (i, 0, 0)))
            out_shape.append(jax.ShapeDtypeStruct((t // tile, padded, tile), BF16))
        else:
            out_specs.append(pl.BlockSpec((rows, tm), lambda i: (0, i)))
            out_shape.append(jax.ShapeDtypeStruct((rows, t), BF16))
    outs = pl.pallas_call(
        _proj_kernel,
        grid=(t // tm,),
        in_specs=[
            pl.BlockSpec((tm, d), lambda i: (i, 0)),
            _const_spec((1, d)),
            pl.BlockSpec((tm * half // LANES, LANES), lambda i: (i, 0)),
            pl.BlockSpec((tm * half // LANES, LANES), lambda i: (i, 0)),
            _const_spec(wr.shape),
            _const_spec(wt.shape),
        ],
        out_specs=out_specs,
        out_shape=out_shape,
        scratch_shapes=[pltpu.VMEM((tm, LANES), F32)],
        compiler_params=_params(1),
        name="in_proj",
    )(x2, g[None, :], cos, sin, wr, wt)
    return dict(zip([n for n, *_ in _ROW_SEGS] + [n for n, *_ in _COL_SEGS], outs))


def _compress_kernel(x_ref, pea_ref, peb_ref, w1a_ref, w1b_ref, w2_ref, o_ref):
    x = x_ref[...]
    a = jnp.dot((x + pea_ref[...]).astype(BF16), w1a_ref[...], preferred_element_type=F32)
    b = jnp.dot((x + peb_ref[...]).astype(BF16), w1b_ref[...], preferred_element_type=F32)
    ncp = x.shape[0]
    h = a + pltpu.roll(b, ncp - 1, 0)
    h = h * jax.nn.sigmoid(h)
    o_ref[...] = jnp.dot(h.astype(BF16), w2_ref[...], preferred_element_type=F32).astype(o_ref.dtype)


def _compress(xc, pe, w1, w2, b, s):
    g_, dh = NSA_KV_GROUPS, HEAD_DIM
    ncp = s // CMP_STRIDE
    hid = w1.shape[1]
    halves = CMP_LEN // CMP_STRIDE
    assert halves == 2
    xr = xc.reshape(b, ncp, CMP_STRIDE * g_ * dh)
    eye = jnp.eye(g_, dtype=F32)
    pe_r = pe.reshape(halves, CMP_STRIDE, 1, dh)
    pe_t = jnp.broadcast_to(pe_r, (halves, CMP_STRIDE, g_, dh)).reshape(halves, 1, CMP_STRIDE * g_ * dh)
    w1r = w1.reshape(halves, CMP_STRIDE, dh, hid)
    w1bd = jnp.einsum("hldj,pg->hlpdgj", w1r, eye).reshape(halves, CMP_STRIDE * g_ * dh, g_ * hid).astype(BF16)
    w2bd = jnp.einsum("jd,pg,r->pjgrd", w2, eye, jnp.ones((2,), F32)).reshape(g_ * hid, g_ * 2 * dh).astype(BF16)
    kdim = CMP_STRIDE * g_ * dh
    return pl.pallas_call(
        _compress_kernel,
        grid=(b,),
        in_specs=[
            pl.BlockSpec((None, ncp, kdim), lambda i: (i, 0, 0)),
            _const_spec((1, kdim)),
            _const_spec((1, kdim)),
            _const_spec((kdim, g_ * hid)),
            _const_spec((kdim, g_ * hid)),
            _const_spec((g_ * hid, g_ * 2 * dh)),
        ],
        out_specs=pl.BlockSpec((None, ncp, g_ * 2 * dh), lambda i: (i, 0, 0)),
        out_shape=jax.ShapeDtypeStruct((b, ncp, g_ * 2 * dh), BF16),
        compiler_params=_params(1),
        name="compress",
    )(xr, pe_t[0], pe_t[1], w1bd[0], w1bd[1], w2bd)


def _split_heads(qt):
    row = lax.broadcasted_iota(jnp.int32, qt.shape, 0)
    zero = jnp.zeros_like(qt)
    return jnp.concatenate([jnp.where(row < HEAD_DIM, qt, zero), jnp.where(row >= HEAD_DIM, qt, zero)], axis=1)


def _flash_scratch(tk, m_cols, dv):
    row = pltpu.VMEM((1, m_cols), F32)
    sc = pltpu.VMEM((tk, m_cols), F32)
    return [sc, sc, row, row, pltpu.VMEM((2 * tk, m_cols), BF16), row, row, pltpu.VMEM((dv + ONES_ROWS, m_cols), F32)]


class _Stream(NamedTuple):
    qst: jax.Array
    load_k: Callable
    load_vt: Callable
    scratch: Sequence
    mask_fn: Callable
    inner_fn: Callable
    skip_last_tile: bool = False


def _flash_tiles(streams, tk, n_k, hi, first_step):
    def scores(st, t, slot, masked):
        s_ref, t_ref = st.scratch[slot], st.scratch[2 + slot]
        if masked and slot == 1 and st.skip_last_tile:
            t_ref[...] = jnp.full(t_ref.shape, NEG_INF, F32)
            s_ref[...] = jnp.full(s_ref.shape, NEG_INF, F32)
            return
        k = st.load_k(pl.multiple_of(jnp.clip(t, 0, n_k - 1) * tk, tk))
        s = (st.mask_fn if masked else st.inner_fn)(t, jnp.dot(k, st.qst, preferred_element_type=F32))
        t_ref[...] = jnp.max(s, axis=0, keepdims=True)
        s_ref[...] = s

    def probs(st):
        sa_ref, sb_ref, ta_ref, tb_ref, p_ref, al_ref, m_ref, _ = st.scratch
        m_old = m_ref[...]
        m_new = jnp.maximum(m_old, jnp.maximum(ta_ref[...], tb_ref[...]))
        p_ref[:tk, :] = jnp.exp2(sa_ref[...] - m_new).astype(BF16)
        p_ref[tk:, :] = jnp.exp2(sb_ref[...] - m_new).astype(BF16)
        al_ref[...] = jnp.exp2(m_old - m_new)
        m_ref[...] = m_new

    def values(st, pair):
        p_ref, al_ref, acc_ref = st.scratch[4], st.scratch[5], st.scratch[7]
        vt = st.load_vt(jnp.clip(pair, 0, n_k // 2 - 1))
        acc_ref[...] = al_ref[...] * acc_ref[...] + jnp.dot(vt, p_ref[...], preferred_element_type=F32)

    def each(fn, *args):
        for st in streams:
            fn(st, *args)

    def init(st):
        m_ref, al_ref = st.scratch[6], st.scratch[5]
        m_ref[...] = jnp.full(m_ref.shape, NEG_INF, F32)
        al_ref[...] = jnp.ones(al_ref.shape, F32)

    @pl.when(first_step)
    def _():
        for st in streams:
            st.scratch[4][...] = jnp.zeros(st.scratch[4].shape, BF16)
            st.scratch[7][...] = jnp.zeros(st.scratch[7].shape, F32)

    n_pairs = hi // 2 + 1

    def pair_at(r):
        return jnp.where(r == 0, n_pairs - 1, r - 1)

    each(init)
    each(scores, 2 * pair_at(0), 0, True)
    each(scores, 2 * pair_at(0) + 1, 1, True)

    def body(r, carry):
        each(values, pair_at(r - 1))
        each(probs)
        each(scores, 2 * r, 0, False)
        each(scores, 2 * r + 1, 1, False)
        return carry

    lax.fori_loop(0, n_pairs - 1, body, 0)
    each(values, pair_at(n_pairs - 2))
    each(probs)
    each(values, pair_at(n_pairs - 1))
    out = []
    for st in streams:
        acc = st.scratch[7][...]
        dv = acc.shape[0] - ONES_ROWS
        out.append((acc[dv:dv + 1], acc[:dv]))
    return out


def _diff_kernel(lam_ref, g_ref, qt_ref, k_ref, vt_ref, o_ref, *scratch, tq, tk, heads, lam_init):
    qi = pl.program_id(2)
    m_cols = 2 * tq
    qpos = qi * tq + lax.broadcasted_iota(jnp.int32, (1, m_cols), 1) % tq
    krel = lax.broadcasted_iota(jnp.int32, (tk, m_cols), 0)
    hw = 2 * HEAD_DIM
    vrows = DIFF_VDIM + ONES_ROWS
    n_sc = len(scratch) // heads

    def causal(t, s):
        return jnp.where(krel <= qpos - t * tk, s, NEG_INF)

    streams = [
        _Stream(
            qst=_split_heads(qt_ref[h * hw:(h + 1) * hw, :]),
            load_k=lambda row, h=h: k_ref[pl.ds(row, tk), h * hw:(h + 1) * hw],
            load_vt=lambda pair, h=h: vt_ref[pair, h * vrows:(h + 1) * vrows, :],
            scratch=scratch[h * n_sc:(h + 1) * n_sc],
            mask_fn=causal,
            inner_fn=lambda t, s: s,
        )
        for h in range(heads)
    ]
    first_step = (pl.program_id(0) == 0) & (pl.program_id(1) == 0) & (qi == 0)
    results = _flash_tiles(streams, tk, k_ref.shape[0] // tk, (qi * tq + tq - 1) // tk, first_step)
    lf = lam_ref[...]
    lam = (jnp.exp(jnp.sum(lf[0:1] * lf[1:2], axis=1, keepdims=True))
           - jnp.exp(jnp.sum(lf[2:3] * lf[3:4], axis=1, keepdims=True)) + lam_init)
    for h, (l, acc) in enumerate(results):
        o = acc * (1.0 / l)
        y = o[:, :tq] - lam * o[:, tq:]
        y = y * lax.rsqrt(jnp.mean(y * y, axis=0, keepdims=True) + NORM_EPS)
        y = y.T * g_ref[...] * (1.0 - lam_init)
        o_ref[:, h * DIFF_VDIM:(h + 1) * DIFF_VDIM] = y.astype(o_ref.dtype)


def _diff_attention(dqt, dk, dvt, diff_lambda, subln_g, b, s, layer):
    tk = KEY_TILE
    tq = 2 * tk
    heads = DIFF_HEADS
    hw = heads * 2 * HEAD_DIM
    vrows = heads * (DIFF_VDIM + ONES_ROWS)
    nq = s // tq
    return pl.pallas_call(
        functools.partial(_diff_kernel, tq=tq, tk=tk, heads=heads, lam_init=_lambda_init(layer)),
        grid=(b, DIFF_HEADS // heads, nq),
        in_specs=[
            _const_spec(diff_lambda.shape),
            _const_spec((1, DIFF_VDIM)),
            pl.BlockSpec((hw, tq), lambda bi, h, i: (h, bi * nq + i)),
            pl.BlockSpec((None, s, hw), lambda bi, h, i: (bi, 0, h)),
            pl.BlockSpec((None, s // KEY_PAIR, vrows, KEY_PAIR), lambda bi, h, i: (bi, 0, h, 0)),
        ],
        out_specs=pl.BlockSpec((None, tq, heads * DIFF_VDIM), lambda bi, h, i: (bi, i, h)),
        out_shape=jax.ShapeDtypeStruct((b, s, DIFF_HEADS * DIFF_VDIM), BF16),
        scratch_shapes=_flash_scratch(tk, 2 * tq, DIFF_VDIM) * heads,
        compiler_params=_params(3),
        name="diff_attn",
    )(diff_lambda, subln_g[None, :], dqt, dk.reshape(b, s, -1), dvt.reshape(b, s // KEY_PAIR, -1, KEY_PAIR))


def _nsa_kernel(ovl_ref, qt_ref, gate_ref, kc_ref, vc_ref, ks_ref, vst_ref, kw_ref, vwt_ref, o_ref, imp_ref, selb_ref, cnt_ref, *scratch, tq, tk, nsub):
    step = pl.program_id(1)
    m_cols = NSA_HPG * tq
    dh = HEAD_DIM
    gw = NSA_HPG * dh
    kwid = 2 * dh
    vrows = dh + ONES_ROWS
    units = [(j, g) for j in range(nsub) for g in range(NSA_KV_GROUPS)]
    n_sc = len(scratch) // len(units)
    ncp = kc_ref.shape[0]
    nsb = imp_ref.shape[1]
    topk = min(SEL_TOPK, nsb)
    bpt = tk // SEL_LEN
    n_back = -(-(WINDOW - 1) // tk)
    krel = lax.broadcasted_iota(jnp.int32, (tk, m_cols), 0)
    blk = lax.broadcasted_iota(jnp.int32, (nsb, tq), 0)
    crow = lax.broadcasted_iota(jnp.int32, (ncp, m_cols), 0)

    q0s = [(step * nsub + j) * tq for j in range(nsub)]
    qpos_ts = [q0 + lax.broadcasted_iota(jnp.int32, (1, tq), 1) for q0 in q0s]
    qposs = [jnp.concatenate([qp] * NSA_HPG, axis=1) for qp in qpos_ts]
    cmasks = [crow <= lax.shift_right_arithmetic(qp - (CMP_LEN - 1), int(math.log2(CMP_STRIDE))) for qp in qposs]
    qblks = [qp // SEL_LEN for qp in qpos_ts]
    forceds = [(blk == 0) | (blk == qb) | (blk == qb - 1) for qb in qblks]

    qsts = []
    for j, g in units:
        qt = qt_ref[g * gw:(g + 1) * gw, j * tq:(j + 1) * tq]
        qsts.append(jnp.concatenate([_split_heads(qt[:LANES]), _split_heads(qt[LANES:])], axis=1))
    scs = [jnp.where(cmasks[j], jnp.dot(kc_ref[:, g * kwid:(g + 1) * kwid], qsts[u], preferred_element_type=F32), NEG_INF)
           for u, (j, g) in enumerate(units)]
    pcs = [jnp.exp2(sc - jnp.max(sc, axis=0, keepdims=True)) for sc in scs]
    pcs = [pc * jnp.where(qposs[j] >= CMP_LEN - 1, 1.0 / jnp.sum(pc, axis=0, keepdims=True), 0.0)
           for pc, (j, g) in zip(pcs, units)]
    o_cs = [lax.dot_general(vc_ref[:, g * kwid:(g + 1) * kwid], pcs[u].astype(BF16), (((0,), (0,)), ((), ())),
                            preferred_element_type=F32)[:dh] for u, (j, g) in enumerate(units)]
    for u, (j, g) in enumerate(units):
        pc_sum = functools.reduce(jnp.add, [pcs[u][:, h * tq:(h + 1) * tq] for h in range(NSA_HPG)])
        imp = jnp.dot(ovl_ref[...], pc_sum, preferred_element_type=F32, precision=lax.Precision.HIGHEST)
        imp = jnp.where(forceds[j], FORCE_SCORE, imp)
        imp_ref[u] = jnp.where(blk <= qblks[j], imp, NEG_INF)
        cnt_ref[u] = jnp.zeros((nsb, tq), jnp.int32)

    sub = 8
    for g0 in range(0, nsb, sub):
        for j in range(nsub):
            @pl.when(g0 <= (q0s[j] + tq - 1) // SEL_LEN)
            def _(g0=g0, j=j):
                for u in [u for u, (ju, _) in enumerate(units) if ju == j]:
                    cnts = [cnt_ref[u, b0:b0 + sub, :] for b0 in range(0, nsb, sub)]
                    for i in range(g0, g0 + sub):
                        row = jnp.broadcast_to(imp_ref[u, i:i + 1, :], (sub, tq))
                        for n, b0 in enumerate(range(0, nsb, sub)):
                            cur = imp_ref[u, b0:b0 + sub, :]
                            if b0 > i:
                                beats = row >= cur
                            elif b0 + sub - 1 < i:
                                beats = row > cur
                            else:
                                beats = (row > cur) | ((row == cur) & (lax.broadcasted_iota(jnp.int32, (sub, tq), 0) > i - b0))
                            cnts[n] = cnts[n] + jnp.where(beats, 1, 0)
                    for n, b0 in enumerate(range(0, nsb, sub)):
                        cnt_ref[u, b0:b0 + sub, :] = cnts[n]
    for u in range(len(units)):
        selb_ref[u] = jnp.where(cnt_ref[u] < topk, 0.0, NEG_INF)

    def biased(u):
        def fn(t, s):
            first_blk = jnp.clip(t, 0, nsb // bpt - 1) * bpt
            rows = []
            for r in range(bpt):
                brow = selb_ref[u, pl.ds(first_blk + r, 1), :]
                brow = jnp.concatenate([brow] * NSA_HPG, axis=1)
                rows.append(jnp.broadcast_to(brow, (SEL_LEN, m_cols)))
            return s + jnp.concatenate(rows, axis=0)
        return fn

    def selected(u, j):
        return lambda t, s: jnp.where(krel <= qposs[j] - t * tk, biased(u)(t, s), NEG_INF)

    streams = [
        _Stream(
            qst=qsts[u],
            load_k=lambda row, g=g: ks_ref[pl.ds(row, tk), g * kwid:(g + 1) * kwid],
            load_vt=lambda pair, g=g: vst_ref[pair, g * vrows:(g + 1) * vrows, :],
            scratch=scratch[u * n_sc:(u + 1) * n_sc],
            mask_fn=selected(u, j),
            inner_fn=biased(u),
            skip_last_tile=j < nsub - 1,
        )
        for u, (j, g) in enumerate(units)
    ]
    last = (q0s[-1] + tq - 1) // tk
    sel_out = _flash_tiles(streams, tk, ks_ref.shape[0] // tk, last, (pl.program_id(0) == 0) & (step == 0))
    o_ss = [acc_s * (1.0 / l_s) for l_s, acc_s in sel_out]

    n_win = n_back + tq // tk
    win_tile = lambda j, r: (step * nsub + j) * (tq // tk) - n_back + r
    s_ws = [[] for _ in units]
    for r in range(n_win):
        d_min = (n_back - r) * tk - (tk - 1)
        d_max = (n_back - r) * tk + (tq - 1)
        for u, (j, g) in enumerate(units):
            t = win_tile(j, r)
            k = kw_ref[pl.ds(pl.multiple_of(jnp.maximum(t, 0) * tk, tk), tk), g * kwid:(g + 1) * kwid]
            s = jnp.dot(k, qsts[u], preferred_element_type=F32)
            if d_min < 0:
                s = jnp.where(krel <= qposs[j] - t * tk, s, NEG_INF)
            if d_max >= WINDOW:
                s = jnp.where(krel > qposs[j] - t * tk - WINDOW, s, NEG_INF)
            if r < n_back:
                s = s + jnp.where(t >= 0, 0.0, NEG_INF)
            s_ws[u].append(s)
    m_ws = [functools.reduce(jnp.maximum, [jnp.max(s, axis=0, keepdims=True) for s in s_w]) for s_w in s_ws]
    acc_ws = [jnp.zeros((vrows, m_cols), F32) for _ in units]
    for r in range(n_win):
        for u, (j, g) in enumerate(units):
            p = jnp.exp2(s_ws[u][r] - m_ws[u]).astype(BF16)
            vt = vwt_ref[jnp.maximum(win_tile(j, r), 0), g * vrows:(g + 1) * vrows, :]
            acc_ws[u] = acc_ws[u] + jnp.dot(vt, p, preferred_element_type=F32)
    o_ws = [acc_w[:dh] * (1.0 / acc_w[dh:dh + 1]) for acc_w in acc_ws]

    gate_ts = [gate_ref[j * tq:(j + 1) * tq, :].T for j in range(nsub)]
    for u, (j, g) in enumerate(units):
        gates = gate_ts[j]
        outs = []
        for h in range(NSA_HPG):
            sl = slice(h * tq, (h + 1) * tq)
            gate = lambda branch, h=h: gates[branch * NSA_HEADS + g * NSA_HPG + h:branch * NSA_HEADS + g * NSA_HPG + h + 1, :]
            outs.append(gate(0) * o_cs[u][:, sl] + gate(1) * o_ss[u][:, sl] + gate(2) * o_ws[u][:, sl])
        y = jnp.concatenate(outs, axis=0)
        o_ref[j * tq:(j + 1) * tq, g * gw:(g + 1) * gw] = y.T.astype(o_ref.dtype)


def _overlap_matrix(s):
    nc = s // CMP_STRIDE
    nsb = s // SEL_LEN
    ci = np.arange(nc)[None, :] * CMP_STRIDE
    sj = np.arange(nsb)[:, None] * SEL_LEN
    ovl = ((ci < sj + SEL_LEN) & (ci + CMP_LEN > sj)).astype(np.float32)
    ovl[:, (s - CMP_LEN) // CMP_STRIDE + 1:] = 0.0
    return jnp.asarray(ovl)


def _nsa_attention(nqt, ng, kcmp, vcmp, ksd, vst, kwd, vwt, b, s):
    tk = tq = KEY_TILE
    nsub = KEY_PAIR // tq
    tstep = nsub * tq
    assert tq == tk and tstep == 2 * tk
    nq = s // tstep
    g_ = NSA_KV_GROUPS
    qw = NSA_HEADS * HEAD_DIM
    kw_ = g_ * 2 * HEAD_DIM
    vrows = g_ * (HEAD_DIM + ONES_ROWS)
    ncp = s // CMP_STRIDE
    nsb = s // SEL_LEN
    seq3 = lambda a: a.reshape(b, s, a.shape[-1])
    vt4 = lambda a: a.reshape(b, -1, a.shape[-2], a.shape[-1])
    k_spec = pl.BlockSpec((None, s, kw_), lambda bi, i: (bi, 0, 0))
    vt_spec = lambda tile: pl.BlockSpec((None, s // tile, vrows, tile), lambda bi, i: (bi, 0, 0, 0))
    cmp_spec = pl.BlockSpec((None, ncp, kw_), lambda bi, i: (bi, 0, 0))
    n_units = g_ * nsub
    per_unit = lambda dt: pltpu.VMEM((n_units, nsb, tq), dt)
    return pl.pallas_call(
        functools.partial(_nsa_kernel, tq=tq, tk=tk, nsub=nsub),
        grid=(b, nq),
        in_specs=[
            _const_spec((nsb, ncp)),
            pl.BlockSpec((qw, tstep), lambda bi, i: (0, bi * nq + i)),
            pl.BlockSpec((None, tstep, LANES), lambda bi, i: (bi, i, 0)),
            cmp_spec, cmp_spec, k_spec, vt_spec(KEY_PAIR), k_spec, vt_spec(KEY_TILE),
        ],
        out_specs=pl.BlockSpec((None, tstep, qw), lambda bi, i: (bi, i, 0)),
        out_shape=jax.ShapeDtypeStruct((b, s, qw), BF16),
        scratch_shapes=[per_unit(F32), per_unit(F32), per_unit(jnp.int32)] + _flash_scratch(tk, NSA_HPG * tq, HEAD_DIM) * n_units,
        compiler_params=_params(2),
        name="nsa_attn",
    )(_overlap_matrix(s), nqt, seq3(ng), kcmp, vcmp, seq3(ksd), vt4(vst), seq3(kwd), vt4(vwt))


def _merge_kernel(x_ref, g_ref, ya_ref, yb_ref, wmg_ref, wa_ref, wb_ref, wo_ref, o_ref):
    x = x_ref[...]
    d = x.shape[1]
    ms = jnp.mean(x * x, axis=-1, keepdims=True)
    u = (x * lax.rsqrt(ms + NORM_EPS) * g_ref[...]).astype(BF16)
    ya = jnp.dot(ya_ref[...], wa_ref[...], preferred_element_type=F32)
    yb = jnp.dot(yb_ref[...], wb_ref[...], preferred_element_type=F32)
    mg_a = jax.nn.sigmoid(jnp.dot(u, wmg_ref[:, :d], preferred_element_type=F32))
    mg_b = jax.nn.sigmoid(jnp.dot(u, wmg_ref[:, d:], preferred_element_type=F32))
    merged = (mg_a * ya + mg_b * yb).astype(BF16)
    o_ref[...] = x + jnp.dot(merged, wo_ref[...], preferred_element_type=F32)


def _merge(x2, g, ya, yb, w_mg, w_a, w_b, w_o):
    t, d = x2.shape
    tm = min(t, ROW_TILE)
    ws = [w.astype(BF16) for w in (w_mg, w_a, w_b, w_o)]
    return pl.pallas_call(
        _merge_kernel,
        grid=(t // tm,),
        in_specs=[
            pl.BlockSpec((tm, d), lambda i: (i, 0)),
            _const_spec((1, d)),
            pl.BlockSpec((tm, ya.shape[1]), lambda i: (i, 0)),
            pl.BlockSpec((tm, yb.shape[1]), lambda i: (i, 0)),
        ] + [_const_spec(w.shape) for w in ws],
        out_specs=pl.BlockSpec((tm, d), lambda i: (i, 0)),
        out_shape=jax.ShapeDtypeStruct((t, d), F32),
        compiler_params=_params(1),
        name="merge_out_proj",
    )(x2, g[None, :], ya, yb, *ws)


def _ffn_kernel(h_ref, g_ref, gf_ref, wg_ref, wu_ref, wd_ref, o_ref, *, chunk, final_norm):
    h = h_ref[...]
    ms = jnp.mean(h * h, axis=-1, keepdims=True)
    u = (h * lax.rsqrt(ms + NORM_EPS) * g_ref[...]).astype(BF16)
    d_ff = wg_ref.shape[1]
    acc = h
    for c0 in range(0, d_ff, chunk):
        c1 = min(c0 + chunk, d_ff)
        a = jnp.dot(u, wg_ref[:, c0:c1], preferred_element_type=F32)
        up = jnp.dot(u, wu_ref[:, c0:c1], preferred_element_type=F32)
        act = (a * jax.nn.sigmoid(a) * up).astype(BF16)
        acc = acc + jnp.dot(act, wd_ref[c0:c1, :], preferred_element_type=F32)
    if final_norm:
        ms = jnp.mean(acc * acc, axis=-1, keepdims=True)
        acc = acc * lax.rsqrt(ms + NORM_EPS) * gf_ref[...]
    o_ref[...] = acc


def _ffn(h2, g, g_final, w_gate, w_up, w_down, final_norm):
    t, d = h2.shape
    tm = min(t, ROW_TILE)
    d_ff = w_gate.shape[1]
    chunk = 256
    ws = [w.astype(BF16) for w in (w_gate, w_up, w_down)]
    return pl.pallas_call(
        functools.partial(_ffn_kernel, chunk=chunk, final_norm=final_norm),
        grid=(t // tm,),
        in_specs=[pl.BlockSpec((tm, d), lambda i: (i, 0)), _const_spec((1, d)), _const_spec((1, d))]
        + [_const_spec(w.shape) for w in ws],
        out_specs=pl.BlockSpec((tm, d), lambda i: (i, 0)),
        out_shape=jax.ShapeDtypeStruct((t, d), F32),
        compiler_params=_params(1),
        name="swiglu_ffn",
    )(h2, g[None, :], g_final[None, :], *ws)


def kernel(x, positions, attn_norm_g, w_in, diff_lambda, diff_subln_g, cmp_pe_k, cmp_pe_v, cmp_k_w1, cmp_k_w2, cmp_v_w1, cmp_v_w2, w_branch_a, w_branch_b, w_out, ffn_norm_g, w_gate, w_up, w_down, final_norm_g):
    b, s, d = x.shape
    depth = w_in.shape[0]
    cos, sin = _rope_tables(positions)
    h = x.reshape(b * s, d)
    gates_off = w_in.shape[2] - 2 * d
    for layer in range(depth):
        w_in_b = w_in[layer].astype(BF16)
        p = _project(h, attn_norm_g[layer], cos, sin, w_in_b[:, :gates_off])
        kcmp = _compress(p["kc"], cmp_pe_k[layer], cmp_k_w1[layer], cmp_k_w2[layer], b, s)
        vcmp = _compress(p["vc"], cmp_pe_v[layer], cmp_v_w1[layer], cmp_v_w2[layer], b, s)
        ya = _diff_attention(p["dqT"], p["dk"], p["dvT"], diff_lambda[layer], diff_subln_g[layer], b, s, layer)
        yb = _nsa_attention(p["nqT"], p["ng"], kcmp, vcmp, p["ksd"], p["vsT"], p["kwd"], p["vwT"], b, s)
        h = _merge(h, attn_norm_g[layer], ya.reshape(b * s, -1), yb.reshape(b * s, -1),
                   w_in_b[:, gates_off:], w_branch_a[layer], w_branch_b[layer], w_out[layer])
        h = _ffn(h, ffn_norm_g[layer], final_norm_g, w_gate[layer], w_up[layer], w_down[layer], layer == depth - 1)
    return h.reshape(b, s, d)
```

```python
import functools
import math
from typing import Callable, NamedTuple, Sequence

import jax
import jax.numpy as jnp
import numpy as np
from jax import lax
from jax.experimental import pallas as pl
from jax.experimental.pallas import tpu as pltpu

HEAD_DIM = 64
ROPE_THETA = 10000.0
NORM_EPS = 1e-6
NEG_INF = -1e30
FORCE_SCORE = 1e9

DIFF_HEADS = 4
DIFF_VDIM = 2 * HEAD_DIM
NSA_HEADS = 8
NSA_KV_GROUPS = 2
NSA_HPG = NSA_HEADS // NSA_KV_GROUPS
CMP_LEN = 32
CMP_STRIDE = 16
SEL_LEN = 64
SEL_TOPK = 16
WINDOW = 512

LANES = 128
VMEM_LIMIT = 56 * 1024 * 1024
KEY_TILE = 256
ROW_TILE = 512

BF16 = jnp.bfloat16
F32 = jnp.float32


def _lambda_init(layer):
    return 0.8 - 0.6 * math.exp(-0.3 * layer)


def _params(n_axes):
    return pltpu.CompilerParams(dimension_semantics=("arbitrary",) * n_axes, vmem_limit_bytes=VMEM_LIMIT)


def _const_spec(shape):
    nd = len(shape)
    return pl.BlockSpec(shape, lambda *_: (0,) * nd)


def _rope_table_kernel(pos_ref, invf_ref, cos_ref, sin_ref):
    ang = pos_ref[...].astype(F32) * invf_ref[...]
    cos_ref[...] = jnp.cos(ang)
    sin_ref[...] = jnp.sin(ang)


def _rope_tables(positions):
    half = HEAD_DIM // 2
    per_row = LANES // half
    t = positions.size
    rows = t // per_row
    pos_e = jnp.repeat(positions.reshape(rows, per_row), half, axis=1)
    inv_freq = 1.0 / (ROPE_THETA ** (jnp.arange(0, HEAD_DIM, 2, dtype=F32) / HEAD_DIM))
    invf = jnp.tile(inv_freq, per_row)[None, :]
    tr = min(rows, 1024)
    cos, sin = pl.pallas_call(
        _rope_table_kernel,
        grid=(rows // tr,),
        in_specs=[pl.BlockSpec((tr, LANES), lambda i: (i, 0)), _const_spec((1, LANES))],
        out_specs=[pl.BlockSpec((tr, LANES), lambda i: (i, 0))] * 2,
        out_shape=[jax.ShapeDtypeStruct((rows, LANES), F32)] * 2,
        compiler_params=_params(1),
        name="rope_tables",
    )(pos_e, invf)
    return cos, sin


_ROW_SEGS = (("dk", 512, BF16, True), ("ksd", 128, BF16, True), ("kwd", 128, BF16, True), ("kc", 128, F32, True),
             ("vc", 128, F32, False), ("ng", 128, F32, False))
_DUP_SEGS = ("ksd", "kwd")
Q_SCALE = HEAD_DIM ** -0.5 * math.log2(math.e)
KEY_PAIR = 2 * KEY_TILE
ONES_ROWS = 16
_COL_SEGS = (("dqT", 512, True, None, None), ("nqT", 512, True, None, None),
             ("dvT", 512, False, KEY_PAIR, DIFF_VDIM), ("vsT", 128, False, KEY_PAIR, HEAD_DIM), ("vwT", 128, False, KEY_TILE, HEAD_DIM))
_STRIDED_SEGS = ("kc", "vc")


def _proj_kernel(x_ref, g_ref, cos_ref, sin_ref, wr_ref, wt_ref, *out_refs):
    x = x_ref[...]
    ms = jnp.mean(x * x, axis=-1, keepdims=True)
    u = (x * lax.rsqrt(ms + NORM_EPS) * g_ref[...]).astype(BF16)
    half = HEAD_DIM // 2
    reps = LANES // half

    def rope_fn(c, s, axis):
        idx = lax.broadcasted_iota(jnp.int32, c.shape, axis)
        upper = (idx % HEAD_DIM) >= half
        s_up = jnp.where(upper, s, 0.0)
        s_lo = jnp.where(upper, 0.0, -s)
        return lambda y: y * c + pltpu.roll(y, half, axis) * s_up + pltpu.roll(y, LANES - half, axis) * s_lo

    *refs, stage_ref = out_refs

    def expand(t_ref):
        t4 = t_ref[...]
        rolled = [t4] + [pltpu.roll(t4, half * m, 1) for m in range(1, reps)]
        slot = lax.broadcasted_iota(jnp.int32, t4.shape, 1) // half
        for j in range(reps):
            ej = rolled[-j % reps]
            for k in range(1, reps):
                ej = jnp.where(slot == k, rolled[(k - j) % reps], ej)
            stage_ref[pl.ds(j, t4.shape[0], stride=reps), :] = ej
        return stage_ref[...]

    c = expand(cos_ref)
    s = expand(sin_ref)
    rope_rows = rope_fn(c, s, 1)
    rope_cols = rope_fn(c.T, s.T, 0)

    tm = x.shape[0]
    pieces = []
    for name, width, _, roped in _ROW_SEGS:
        o_ref = refs.pop(0)
        pieces += [(name, o_ref, c, roped) for c in range(0, width, LANES)]
    for p0 in range(0, len(pieces), 2):
        pair = pieces[p0:p0 + 2]
        y = jnp.dot(u, wr_ref[:, p0 * LANES:(p0 + len(pair)) * LANES], preferred_element_type=F32)
        for i, (name, o_ref, c, roped) in enumerate(pair):
            yl = y[:, i * LANES:(i + 1) * LANES]
            if roped:
                yl = rope_rows(yl)
            if name == "ng":
                yl = jax.nn.sigmoid(yl)
            if name in _DUP_SEGS:
                other = pltpu.roll(yl, HEAD_DIM, 1)
                first = lax.broadcasted_iota(jnp.int32, yl.shape, 1) < HEAD_DIM
                o_ref[:, :LANES] = jnp.where(first, yl, other).astype(o_ref.dtype)
                o_ref[:, LANES:] = jnp.where(first, other, yl).astype(o_ref.dtype)
            elif name in _STRIDED_SEGS:
                stage_ref[...] = yl
                for l in range(CMP_STRIDE):
                    o_ref[:, l * LANES:(l + 1) * LANES] = stage_ref[pl.ds(l, tm // CMP_STRIDE, stride=CMP_STRIDE), :]
            else:
                o_ref[:, c:c + LANES] = yl.astype(o_ref.dtype)

    ut = u.T
    pieces = []
    for name, rows, roped, tile, group in _COL_SEGS:
        o_ref = refs.pop(0)
        pieces += [(o_ref, r, roped, tile, group) for r in range(0, rows, LANES)]
    for p0 in range(0, len(pieces), 2):
        pair = pieces[p0:p0 + 2]
        yt = jnp.dot(wt_ref[p0 * LANES:(p0 + len(pair)) * LANES, :], ut, preferred_element_type=F32)
        for i, (o_ref, r, roped, tile, group) in enumerate(pair):
            ys = yt[i * LANES:(i + 1) * LANES]
            if roped:
                ys = rope_cols(ys) * Q_SCALE
            ys = ys.astype(o_ref.dtype)
            if not tile:
                o_ref[r:r + LANES, :] = ys
                continue
            for g0 in range(0, LANES, group):
                dst = (r + g0) // group * (group + ONES_ROWS)
                for j in range(tm // tile):
                    o_ref[j, dst:dst + group, :] = ys[g0:g0 + group, j * tile:(j + 1) * tile]
                    o_ref[j, dst + group:dst + group + ONES_ROWS, :] = jnp.ones((ONES_ROWS, tile), o_ref.dtype)


def _project(x2, g, cos, sin, w_in):
    t, d = x2.shape
    qk = DIFF_HEADS * 2 * HEAD_DIM
    kv = NSA_KV_GROUPS * HEAD_DIM
    splits = (qk, qk, DIFF_HEADS * DIFF_VDIM, NSA_HEADS * HEAD_DIM, kv, kv, kv, kv, kv, kv, 3 * NSA_HEADS)
    offs = np.cumsum((0,) + splits)
    dq, dk, dv, nq, kc, vc, ks, vs, kw, vw, ng = (w_in[:, offs[i]:offs[i + 1]] for i in range(len(splits)))
    ng = jnp.pad(ng, ((0, 0), (0, LANES - ng.shape[1])))
    wr = jnp.concatenate([dk, ks, kw, kc, vc, ng], axis=1)
    wt = jnp.concatenate([dq, nq, dv, vs, vw], axis=1).T
    tm = min(t, ROW_TILE)
    assert tm % KEY_TILE == 0
    half = HEAD_DIM // 2
    out_specs, out_shape = [], []
    for name, w, dt, _ in _ROW_SEGS:
        fold = CMP_STRIDE if name in _STRIDED_SEGS else 1
        w_out = w * fold * (2 if name in _DUP_SEGS else 1)
        out_specs.append(pl.BlockSpec((tm // fold, w_out), lambda i: (i, 0)))
        out_shape.append(jax.ShapeDtypeStruct((t // fold, w_out), dt))
    for _, rows, _, tile, group in _COL_SEGS:
        if tile:
            assert tm % tile == 0
            padded = rows // group * (group + ONES_ROWS)
            out_specs.append(pl.BlockSpec((tm // tile, padded, tile), lambda i: (i, 0, 0)))
            out_shape.append(jax.ShapeDtypeStruct((t // tile, padded, tile), BF16))
        else:
            out_specs.append(pl.BlockSpec((rows, tm), lambda i: (0, i)))
            out_shape.append(jax.ShapeDtypeStruct((rows, t), BF16))
    outs = pl.pallas_call(
        _proj_kernel,
        grid=(t // tm,),
        in_specs=[
            pl.BlockSpec((tm, d), lambda i: (i, 0)),
            _const_spec((1, d)),
            pl.BlockSpec((tm * half // LANES, LANES), lambda i: (i, 0)),
            pl.BlockSpec((tm * half // LANES, LANES), lambda i: (i, 0)),
            _const_spec(wr.shape),
            _const_spec(wt.shape),
        ],
        out_specs=out_specs,
        out_shape=out_shape,
        scratch_shapes=[pltpu.VMEM((tm, LANES), F32)],
        compiler_params=_params(1),
        name="in_proj",
    )(x2, g[None, :], cos, sin, wr, wt)
    return dict(zip([n for n, *_ in _ROW_SEGS] + [n for n, *_ in _COL_SEGS], outs))


def _compress_kernel(x_ref, pea_ref, peb_ref, w1a_ref, w1b_ref, w2_ref, o_ref):
    x = x_ref[...]
    a = jnp.dot((x + pea_ref[...]).astype(BF16), w1a_ref[...], preferred_element_type=F32)
    b = jnp.dot((x + peb_ref[...]).astype(BF16), w1b_ref[...], preferred_element_type=F32)
    ncp = x.shape[0]
    h = a + pltpu.roll(b, ncp - 1, 0)
    h = h * jax.nn.sigmoid(h)
    o_ref[...] = jnp.dot(h.astype(BF16), w2_ref[...], preferred_element_type=F32).astype(o_ref.dtype)


def _compress(xc, pe, w1, w2, b, s):
    g_, dh = NSA_KV_GROUPS, HEAD_DIM
    ncp = s // CMP_STRIDE
    hid = w1.shape[1]
    halves = CMP_LEN // CMP_STRIDE
    assert halves == 2
    xr = xc.reshape(b, ncp, CMP_STRIDE * g_ * dh)
    eye = jnp.eye(g_, dtype=F32)
    pe_r = pe.reshape(halves, CMP_STRIDE, 1, dh)
    pe_t = jnp.broadcast_to(pe_r, (halves, CMP_STRIDE, g_, dh)).reshape(halves, 1, CMP_STRIDE * g_ * dh)
    w1r = w1.reshape(halves, CMP_STRIDE, dh, hid).astype(BF16)
    w1bd = jnp.zeros((halves, CMP_STRIDE, g_, dh, g_ * hid), BF16)
    for g in range(g_):
        w1bd = w1bd.at[:, :, g, :, g * hid:(g + 1) * hid].set(w1r)
    w1bd = w1bd.reshape(halves, CMP_STRIDE * g_ * dh, g_ * hid)
    w2bd = jnp.einsum("jd,pg,r->pjgrd", w2, eye, jnp.ones((2,), F32)).reshape(g_ * hid, g_ * 2 * dh).astype(BF16)
    kdim = CMP_STRIDE * g_ * dh
    return pl.pallas_call(
        _compress_kernel,
        grid=(b,),
        in_specs=[
            pl.BlockSpec((None, ncp, kdim), lambda i: (i, 0, 0)),
            _const_spec((1, kdim)),
            _const_spec((1, kdim)),
            _const_spec((kdim, g_ * hid)),
            _const_spec((kdim, g_ * hid)),
            _const_spec((g_ * hid, g_ * 2 * dh)),
        ],
        out_specs=pl.BlockSpec((None, ncp, g_ * 2 * dh), lambda i: (i, 0, 0)),
        out_shape=jax.ShapeDtypeStruct((b, ncp, g_ * 2 * dh), BF16),
        compiler_params=_params(1),
        name="compress",
    )(xr, pe_t[0], pe_t[1], w1bd[0], w1bd[1], w2bd)


def _split_heads(qt):
    row = lax.broadcasted_iota(jnp.int32, qt.shape, 0)
    zero = jnp.zeros_like(qt)
    return jnp.concatenate([jnp.where(row < HEAD_DIM, qt, zero), jnp.where(row >= HEAD_DIM, qt, zero)], axis=1)


def _flash_scratch(tk, m_cols, dv):
    row = pltpu.VMEM((1, m_cols), F32)
    sc = pltpu.VMEM((tk, m_cols), F32)
    return [sc, sc, row, row, pltpu.VMEM((2 * tk, m_cols), BF16), row, row, pltpu.VMEM((dv + ONES_ROWS, m_cols), F32)]


class _Stream(NamedTuple):
    qst: jax.Array
    load_k: Callable
    load_vt: Callable
    scratch: Sequence
    mask_fn: Callable
    inner_fn: Callable
    skip_last_tile: bool = False


def _flash_tiles(streams, tk, n_k, hi, first_step):
    def scores(st, t, slot, masked):
        s_ref, t_ref = st.scratch[slot], st.scratch[2 + slot]
        if masked and slot == 1 and st.skip_last_tile:
            t_ref[...] = jnp.full(t_ref.shape, NEG_INF, F32)
            s_ref[...] = jnp.full(s_ref.shape, NEG_INF, F32)
            return
        k = st.load_k(pl.multiple_of(jnp.clip(t, 0, n_k - 1) * tk, tk))
        s = (st.mask_fn if masked else st.inner_fn)(t, jnp.dot(k, st.qst, preferred_element_type=F32))
        t_ref[...] = jnp.max(s, axis=0, keepdims=True)
        s_ref[...] = s

    def probs(st):
        sa_ref, sb_ref, ta_ref, tb_ref, p_ref, al_ref, m_ref, _ = st.scratch
        m_old = m_ref[...]
        m_new = jnp.maximum(m_old, jnp.maximum(ta_ref[...], tb_ref[...]))
        p_ref[:tk, :] = jnp.exp2(sa_ref[...] - m_new).astype(BF16)
        p_ref[tk:, :] = jnp.exp2(sb_ref[...] - m_new).astype(BF16)
        al_ref[...] = jnp.exp2(m_old - m_new)
        m_ref[...] = m_new

    def values(st, pair):
        p_ref, al_ref, acc_ref = st.scratch[4], st.scratch[5], st.scratch[7]
        vt = st.load_vt(jnp.clip(pair, 0, n_k // 2 - 1))
        acc_ref[...] = al_ref[...] * acc_ref[...] + jnp.dot(vt, p_ref[...], preferred_element_type=F32)

    def each(fn, *args):
        for st in streams:
            fn(st, *args)

    def init(st):
        m_ref, al_ref = st.scratch[6], st.scratch[5]
        m_ref[...] = jnp.full(m_ref.shape, NEG_INF, F32)
        al_ref[...] = jnp.ones(al_ref.shape, F32)

    @pl.when(first_step)
    def _():
        for st in streams:
            st.scratch[4][...] = jnp.zeros(st.scratch[4].shape, BF16)
            st.scratch[7][...] = jnp.zeros(st.scratch[7].shape, F32)

    n_pairs = hi // 2 + 1

    def pair_at(r):
        return jnp.where(r == 0, n_pairs - 1, r - 1)

    each(init)
    each(scores, 2 * pair_at(0), 0, True)
    each(scores, 2 * pair_at(0) + 1, 1, True)

    def body(r, carry):
        each(values, pair_at(r - 1))
        each(probs)
        each(scores, 2 * r, 0, False)
        each(scores, 2 * r + 1, 1, False)
        return carry

    lax.fori_loop(0, n_pairs - 1, body, 0)
    each(values, pair_at(n_pairs - 2))
    each(probs)
    each(values, pair_at(n_pairs - 1))
    out = []
    for st in streams:
        acc = st.scratch[7][...]
        dv = acc.shape[0] - ONES_ROWS
        out.append((acc[dv:dv + 1], acc[:dv]))
    return out


def _diff_kernel(lam_ref, g_ref, qt_ref, k_ref, vt_ref, o_ref, *scratch, tq, tk, heads, lam_init):
    qi = pl.program_id(2)
    m_cols = 2 * tq
    qpos = qi * tq + lax.broadcasted_iota(jnp.int32, (1, m_cols), 1) % tq
    krel = lax.broadcasted_iota(jnp.int32, (tk, m_cols), 0)
    hw = 2 * HEAD_DIM
    vrows = DIFF_VDIM + ONES_ROWS
    n_sc = len(scratch) // heads

    def causal(t, s):
        return jnp.where(krel <= qpos - t * tk, s, NEG_INF)

    streams = [
        _Stream(
            qst=_split_heads(qt_ref[h * hw:(h + 1) * hw, :]),
            load_k=lambda row, h=h: k_ref[pl.ds(row, tk), h * hw:(h + 1) * hw],
            load_vt=lambda pair, h=h: vt_ref[pair, h * vrows:(h + 1) * vrows, :],
            scratch=scratch[h * n_sc:(h + 1) * n_sc],
            mask_fn=causal,
            inner_fn=lambda t, s: s,
        )
        for h in range(heads)
    ]
    first_step = (pl.program_id(0) == 0) & (pl.program_id(1) == 0) & (qi == 0)
    results = _flash_tiles(streams, tk, k_ref.shape[0] // tk, (qi * tq + tq - 1) // tk, first_step)
    lf = lam_ref[...]
    lam = (jnp.exp(jnp.sum(lf[0:1] * lf[1:2], axis=1, keepdims=True))
           - jnp.exp(jnp.sum(lf[2:3] * lf[3:4], axis=1, keepdims=True)) + lam_init)
    for h, (l, acc) in enumerate(results):
        o = acc * (1.0 / l)
        y = o[:, :tq] - lam * o[:, tq:]
        y = y * lax.rsqrt(jnp.mean(y * y, axis=0, keepdims=True) + NORM_EPS)
        y = y.T * g_ref[...] * (1.0 - lam_init)
        o_ref[:, h * DIFF_VDIM:(h + 1) * DIFF_VDIM] = y.astype(o_ref.dtype)


def _diff_attention(dqt, dk, dvt, diff_lambda, subln_g, b, s, layer):
    tk = KEY_TILE
    tq = 2 * tk
    heads = DIFF_HEADS
    hw = heads * 2 * HEAD_DIM
    vrows = heads * (DIFF_VDIM + ONES_ROWS)
    nq = s // tq
    return pl.pallas_call(
        functools.partial(_diff_kernel, tq=tq, tk=tk, heads=heads, lam_init=_lambda_init(layer)),
        grid=(b, DIFF_HEADS // heads, nq),
        in_specs=[
            _const_spec(diff_lambda.shape),
            _const_spec((1, DIFF_VDIM)),
            pl.BlockSpec((hw, tq), lambda bi, h, i: (h, bi * nq + i)),
            pl.BlockSpec((None, s, hw), lambda bi, h, i: (bi, 0, h)),
            pl.BlockSpec((None, s // KEY_PAIR, vrows, KEY_PAIR), lambda bi, h, i: (bi, 0, h, 0)),
        ],
        out_specs=pl.BlockSpec((None, tq, heads * DIFF_VDIM), lambda bi, h, i: (bi, i, h)),
        out_shape=jax.ShapeDtypeStruct((b, s, DIFF_HEADS * DIFF_VDIM), BF16),
        scratch_shapes=_flash_scratch(tk, 2 * tq, DIFF_VDIM) * heads,
        compiler_params=_params(3),
        name="diff_attn",
    )(diff_lambda, subln_g[None, :], dqt, dk.reshape(b, s, -1), dvt.reshape(b, s // KEY_PAIR, -1, KEY_PAIR))


def _nsa_kernel(ovl_ref, qt_ref, gate_ref, kc_ref, vc_ref, ks_ref, vst_ref, kw_ref, vwt_ref, o_ref, imp_ref, selb_ref, cnt_ref, *scratch, tq, tk, nsub):
    step = pl.program_id(1)
    m_cols = NSA_HPG * tq
    dh = HEAD_DIM
    gw = NSA_HPG * dh
    kwid = 2 * dh
    vrows = dh + ONES_ROWS
    units = [(j, g) for j in range(nsub) for g in range(NSA_KV_GROUPS)]
    n_sc = len(scratch) // len(units)
    ncp = kc_ref.shape[0]
    nsb = imp_ref.shape[1]
    topk = min(SEL_TOPK, nsb)
    bpt = tk // SEL_LEN
    n_back = -(-(WINDOW - 1) // tk)
    krel = lax.broadcasted_iota(jnp.int32, (tk, m_cols), 0)
    blk = lax.broadcasted_iota(jnp.int32, (nsb, tq), 0)
    crow = lax.broadcasted_iota(jnp.int32, (ncp, m_cols), 0)

    q0s = [(step * nsub + j) * tq for j in range(nsub)]
    qpos_ts = [q0 + lax.broadcasted_iota(jnp.int32, (1, tq), 1) for q0 in q0s]
    qposs = [jnp.concatenate([qp] * NSA_HPG, axis=1) for qp in qpos_ts]
    cmasks = [crow <= lax.shift_right_arithmetic(qp - (CMP_LEN - 1), int(math.log2(CMP_STRIDE))) for qp in qposs]
    qblks = [qp // SEL_LEN for qp in qpos_ts]
    forceds = [(blk == 0) | (blk == qb) | (blk == qb - 1) for qb in qblks]

    qsts = []
    for j, g in units:
        qt = qt_ref[g * gw:(g + 1) * gw, j * tq:(j + 1) * tq]
        qsts.append(jnp.concatenate([_split_heads(qt[:LANES]), _split_heads(qt[LANES:])], axis=1))
    scs = [jnp.where(cmasks[j], jnp.dot(kc_ref[:, g * kwid:(g + 1) * kwid], qsts[u], preferred_element_type=F32), NEG_INF)
           for u, (j, g) in enumerate(units)]
    pcs = [jnp.exp2(sc - jnp.max(sc, axis=0, keepdims=True)) for sc in scs]
    pcs = [pc * jnp.where(qposs[j] >= CMP_LEN - 1, 1.0 / jnp.sum(pc, axis=0, keepdims=True), 0.0)
           for pc, (j, g) in zip(pcs, units)]
    o_cs = [lax.dot_general(vc_ref[:, g * kwid:(g + 1) * kwid], pcs[u].astype(BF16), (((0,), (0,)), ((), ())),
                            preferred_element_type=F32)[:dh] for u, (j, g) in enumerate(units)]
    for u, (j, g) in enumerate(units):
        pc_sum = functools.reduce(jnp.add, [pcs[u][:, h * tq:(h + 1) * tq] for h in range(NSA_HPG)])
        imp = jnp.dot(ovl_ref[...], pc_sum, preferred_element_type=F32, precision=lax.Precision.HIGHEST)
        imp = jnp.where(forceds[j], FORCE_SCORE, imp)
        imp_ref[u] = jnp.where(blk <= qblks[j], imp, NEG_INF)
        cnt_ref[u] = jnp.zeros((nsb, tq), jnp.int32)

    sub = 8
    for g0 in range(0, nsb, sub):
        for j in range(nsub):
            @pl.when(g0 <= (q0s[j] + tq - 1) // SEL_LEN)
            def _(g0=g0, j=j):
                for u in [u for u, (ju, _) in enumerate(units) if ju == j]:
                    cnts = [cnt_ref[u, b0:b0 + sub, :] for b0 in range(0, nsb, sub)]
                    for i in range(g0, g0 + sub):
                        row = jnp.broadcast_to(imp_ref[u, i:i + 1, :], (sub, tq))
                        for n, b0 in enumerate(range(0, nsb, sub)):
                            cur = imp_ref[u, b0:b0 + sub, :]
                            if b0 > i:
                                beats = row >= cur
                            elif b0 + sub - 1 < i:
                                beats = row > cur
                            else:
                                beats = (row > cur) | ((row == cur) & (lax.broadcasted_iota(jnp.int32, (sub, tq), 0) > i - b0))
                            cnts[n] = cnts[n] + jnp.where(beats, 1, 0)
                    for n, b0 in enumerate(range(0, nsb, sub)):
                        cnt_ref[u, b0:b0 + sub, :] = cnts[n]
    for u in range(len(units)):
        selb_ref[u] = jnp.where(cnt_ref[u] < topk, 0.0, NEG_INF)

    def biased(u):
        def fn(t, s):
            first_blk = jnp.clip(t, 0, nsb // bpt - 1) * bpt
            rows = []
            for r in range(bpt):
                brow = selb_ref[u, pl.ds(first_blk + r, 1), :]
                brow = jnp.concatenate([brow] * NSA_HPG, axis=1)
                rows.append(jnp.broadcast_to(brow, (SEL_LEN, m_cols)))
            return s + jnp.concatenate(rows, axis=0)
        return fn

    def selected(u, j):
        return lambda t, s: jnp.where(krel <= qposs[j] - t * tk, biased(u)(t, s), NEG_INF)

    streams = [
        _Stream(
            qst=qsts[u],
            load_k=lambda row, g=g: ks_ref[pl.ds(row, tk), g * kwid:(g + 1) * kwid],
            load_vt=lambda pair, g=g: vst_ref[pair, g * vrows:(g + 1) * vrows, :],
            scratch=scratch[u * n_sc:(u + 1) * n_sc],
            mask_fn=selected(u, j),
            inner_fn=biased(u),
            skip_last_tile=j < nsub - 1,
        )
        for u, (j, g) in enumerate(units)
    ]
    last = (q0s[-1] + tq - 1) // tk
    sel_out = _flash_tiles(streams, tk, ks_ref.shape[0] // tk, last, (pl.program_id(0) == 0) & (step == 0))
    o_ss = [acc_s * (1.0 / l_s) for l_s, acc_s in sel_out]

    n_win = n_back + tq // tk
    win_tile = lambda j, r: (step * nsub + j) * (tq // tk) - n_back + r
    s_ws = [[] for _ in units]
    for r in range(n_win):
        d_min = (n_back - r) * tk - (tk - 1)
        d_max = (n_back - r) * tk + (tq - 1)
        for u, (j, g) in enumerate(units):
            t = win_tile(j, r)
            k = kw_ref[pl.ds(pl.multiple_of(jnp.maximum(t, 0) * tk, tk), tk), g * kwid:(g + 1) * kwid]
            s = jnp.dot(k, qsts[u], preferred_element_type=F32)
            if d_min < 0:
                s = jnp.where(krel <= qposs[j] - t * tk, s, NEG_INF)
            if d_max >= WINDOW:
                s = jnp.where(krel > qposs[j] - t * tk - WINDOW, s, NEG_INF)
            if r < n_back:
                s = s + jnp.where(t >= 0, 0.0, NEG_INF)
            s_ws[u].append(s)
    m_ws = [functools.reduce(jnp.maximum, [jnp.max(s, axis=0, keepdims=True) for s in s_w]) for s_w in s_ws]
    acc_ws = [jnp.zeros((vrows, m_cols), F32) for _ in units]
    for r in range(n_win):
        for u, (j, g) in enumerate(units):
            p = jnp.exp2(s_ws[u][r] - m_ws[u]).astype(BF16)
            vt = vwt_ref[jnp.maximum(win_tile(j, r), 0), g * vrows:(g + 1) * vrows, :]
            acc_ws[u] = acc_ws[u] + jnp.dot(vt, p, preferred_element_type=F32)
    o_ws = [acc_w[:dh] * (1.0 / acc_w[dh:dh + 1]) for acc_w in acc_ws]

    gate_ts = [gate_ref[j * tq:(j + 1) * tq, :].T for j in range(nsub)]
    for u, (j, g) in enumerate(units):
        gates = gate_ts[j]
        outs = []
        for h in range(NSA_HPG):
            sl = slice(h * tq, (h + 1) * tq)
            gate = lambda branch, h=h: gates[branch * NSA_HEADS + g * NSA_HPG + h:branch * NSA_HEADS + g * NSA_HPG + h + 1, :]
            outs.append(gate(0) * o_cs[u][:, sl] + gate(1) * o_ss[u][:, sl] + gate(2) * o_ws[u][:, sl])
        y = jnp.concatenate(outs, axis=0)
        o_ref[j * tq:(j + 1) * tq, g * gw:(g + 1) * gw] = y.T.astype(o_ref.dtype)


def _overlap_matrix(s):
    nc = s // CMP_STRIDE
    nsb = s // SEL_LEN
    ci = np.arange(nc)[None, :] * CMP_STRIDE
    sj = np.arange(nsb)[:, None] * SEL_LEN
    ovl = ((ci < sj + SEL_LEN) & (ci + CMP_LEN > sj)).astype(np.float32)
    ovl[:, (s - CMP_LEN) // CMP_STRIDE + 1:] = 0.0
    return jnp.asarray(ovl)


def _nsa_attention(nqt, ng, kcmp, vcmp, ksd, vst, kwd, vwt, b, s):
    tk = tq = KEY_TILE
    nsub = KEY_PAIR // tq
    tstep = nsub * tq
    assert tq == tk and tstep == 2 * tk
    nq = s // tstep
    g_ = NSA_KV_GROUPS
    qw = NSA_HEADS * HEAD_DIM
    kw_ = g_ * 2 * HEAD_DIM
    vrows = g_ * (HEAD_DIM + ONES_ROWS)
    ncp = s // CMP_STRIDE
    nsb = s // SEL_LEN
    seq3 = lambda a: a.reshape(b, s, a.shape[-1])
    vt4 = lambda a: a.reshape(b, -1, a.shape[-2], a.shape[-1])
    k_spec = pl.BlockSpec((None, s, kw_), lambda bi, i: (bi, 0, 0))
    vt_spec = lambda tile: pl.BlockSpec((None, s // tile, vrows, tile), lambda bi, i: (bi, 0, 0, 0))
    cmp_spec = pl.BlockSpec((None, ncp, kw_), lambda bi, i: (bi, 0, 0))
    n_units = g_ * nsub
    per_unit = lambda dt: pltpu.VMEM((n_units, nsb, tq), dt)
    return pl.pallas_call(
        functools.partial(_nsa_kernel, tq=tq, tk=tk, nsub=nsub),
        grid=(b, nq),
        in_specs=[
            _const_spec((nsb, ncp)),
            pl.BlockSpec((qw, tstep), lambda bi, i: (0, bi * nq + i)),
            pl.BlockSpec((None, tstep, LANES), lambda bi, i: (bi, i, 0)),
            cmp_spec, cmp_spec, k_spec, vt_spec(KEY_PAIR), k_spec, vt_spec(KEY_TILE),
        ],
        out_specs=pl.BlockSpec((None, tstep, qw), lambda bi, i: (bi, i, 0)),
        out_shape=jax.ShapeDtypeStruct((b, s, qw), BF16),
        scratch_shapes=[per_unit(F32), per_unit(F32), per_unit(jnp.int32)] + _flash_scratch(tk, NSA_HPG * tq, HEAD_DIM) * n_units,
        compiler_params=_params(2),
        name="nsa_attn",
    )(_overlap_matrix(s), nqt, seq3(ng), kcmp, vcmp, seq3(ksd), vt4(vst), seq3(kwd), vt4(vwt))


def _merge_kernel(x_ref, g_ref, ya_ref, yb_ref, wmg_ref, wa_ref, wb_ref, wo_ref, o_ref):
    x = x_ref[...]
    d = x.shape[1]
    ms = jnp.mean(x * x, axis=-1, keepdims=True)
    u = (x * lax.rsqrt(ms + NORM_EPS) * g_ref[...]).astype(BF16)
    ya = jnp.dot(ya_ref[...], wa_ref[...], preferred_element_type=F32)
    yb = jnp.dot(yb_ref[...], wb_ref[...], preferred_element_type=F32)
    mg_a = jax.nn.sigmoid(jnp.dot(u, wmg_ref[:, :d], preferred_element_type=F32))
    mg_b = jax.nn.sigmoid(jnp.dot(u, wmg_ref[:, d:], preferred_element_type=F32))
    merged = (mg_a * ya + mg_b * yb).astype(BF16)
    o_ref[...] = x + jnp.dot(merged, wo_ref[...], preferred_element_type=F32)


def _merge(x2, g, ya, yb, w_mg, w_a, w_b, w_o):
    t, d = x2.shape
    tm = min(t, ROW_TILE)
    ws = [w.astype(BF16) for w in (w_mg, w_a, w_b, w_o)]
    return pl.pallas_call(
        _merge_kernel,
        grid=(t // tm,),
        in_specs=[
            pl.BlockSpec((tm, d), lambda i: (i, 0)),
            _const_spec((1, d)),
            pl.BlockSpec((tm, ya.shape[1]), lambda i: (i, 0)),
            pl.BlockSpec((tm, yb.shape[1]), lambda i: (i, 0)),
        ] + [_const_spec(w.shape) for w in ws],
        out_specs=pl.BlockSpec((tm, d), lambda i: (i, 0)),
        out_shape=jax.ShapeDtypeStruct((t, d), F32),
        compiler_params=_params(1),
        name="merge_out_proj",
    )(x2, g[None, :], ya, yb, *ws)


def _ffn_kernel(h_ref, g_ref, gf_ref, wg_ref, wu_ref, wd_ref, o_ref, *, chunk, final_norm):
    h = h_ref[...]
    ms = jnp.mean(h * h, axis=-1, keepdims=True)
    u = (h * lax.rsqrt(ms + NORM_EPS) * g_ref[...]).astype(BF16)
    d_ff = wg_ref.shape[1]
    acc = h
    for c0 in range(0, d_ff, chunk):
        c1 = min(c0 + chunk, d_ff)
        a = jnp.dot(u, wg_ref[:, c0:c1], preferred_element_type=F32)
        up = jnp.dot(u, wu_ref[:, c0:c1], preferred_element_type=F32)
        act = (a * jax.nn.sigmoid(a) * up).astype(BF16)
        acc = acc + jnp.dot(act, wd_ref[c0:c1, :], preferred_element_type=F32)
    if final_norm:
        ms = jnp.mean(acc * acc, axis=-1, keepdims=True)
        acc = acc * lax.rsqrt(ms + NORM_EPS) * gf_ref[...]
    o_ref[...] = acc


def _ffn(h2, g, g_final, w_gate, w_up, w_down, final_norm):
    t, d = h2.shape
    tm = min(t, ROW_TILE)
    d_ff = w_gate.shape[1]
    chunk = 256
    ws = [w.astype(BF16) for w in (w_gate, w_up, w_down)]
    return pl.pallas_call(
        functools.partial(_ffn_kernel, chunk=chunk, final_norm=final_norm),
        grid=(t // tm,),
        in_specs=[pl.BlockSpec((tm, d), lambda i: (i, 0)), _const_spec((1, d)), _const_spec((1, d))]
        + [_const_spec(w.shape) for w in ws],
        out_specs=pl.BlockSpec((tm, d), lambda i: (i, 0)),
        out_shape=jax.ShapeDtypeStruct((t, d), F32),
        compiler_params=_params(1),
        name="swiglu_ffn",
    )(h2, g[None, :], g_final[None, :], *ws)


def kernel(x, positions, attn_norm_g, w_in, diff_lambda, diff_subln_g, cmp_pe_k, cmp_pe_v, cmp_k_w1, cmp_k_w2, cmp_v_w1, cmp_v_w2, w_branch_a, w_branch_b, w_out, ffn_norm_g, w_gate, w_up, w_down, final_norm_g):
    b, s, d = x.shape
    depth = w_in.shape[0]
    cos, sin = _rope_tables(positions)
    h = x.reshape(b * s, d)
    gates_off = w_in.shape[2] - 2 * d
    for layer in range(depth):
        w_in_b = w_in[layer].astype(BF16)
        p = _project(h, attn_norm_g[layer], cos, sin, w_in_b[:, :gates_off])
        kcmp = _compress(p["kc"], cmp_pe_k[layer], cmp_k_w1[layer], cmp_k_w2[layer], b, s)
        vcmp = _compress(p["vc"], cmp_pe_v[layer], cmp_v_w1[layer], cmp_v_w2[layer], b, s)
        ya = _diff_attention(p["dqT"], p["dk"], p["dvT"], diff_lambda[layer], diff_subln_g[layer], b, s, layer)
        yb = _nsa_attention(p["nqT"], p["ng"], kcmp, vcmp, p["ksd"], p["vsT"], p["kwd"], p["vwT"], b, s)
        h = _merge(h, attn_norm_g[layer], ya.reshape(b * s, -1), yb.reshape(b * s, -1),
                   w_in_b[:, gates_off:], w_branch_a[layer], w_branch_b[layer], w_out[layer])
        h = _ffn(h, ffn_norm_g[layer], final_norm_g, w_gate[layer], w_up[layer], w_down[layer], layer == depth - 1)
    return h.reshape(b, s, d)
```

```python
import functools
import math
from typing import Callable, NamedTuple, Sequence

import jax
import jax.numpy as jnp
import numpy as np
from jax import lax
from jax.experimental import pallas as pl
from jax.experimental.pallas import tpu as pltpu

HEAD_DIM = 64
ROPE_THETA = 10000.0
NORM_EPS = 1e-6
NEG_INF = -1e30
FORCE_SCORE = 1e9

DIFF_HEADS = 4
DIFF_VDIM = 2 * HEAD_DIM
NSA_HEADS = 8
NSA_KV_GROUPS = 2
NSA_HPG = NSA_HEADS // NSA_KV_GROUPS
CMP_LEN = 32
CMP_STRIDE = 16
SEL_LEN = 64
SEL_TOPK = 16
WINDOW = 512

LANES = 128
VMEM_LIMIT = 56 * 1024 * 1024
KEY_TILE = 256
ROW_TILE = 512

BF16 = jnp.bfloat16
F32 = jnp.float32


def _lambda_init(layer):
    return 0.8 - 0.6 * math.exp(-0.3 * layer)


def _params(n_axes):
    return pltpu.CompilerParams(dimension_semantics=("arbitrary",) * n_axes, vmem_limit_bytes=VMEM_LIMIT)


def _const_spec(shape):
    nd = len(shape)
    return pl.BlockSpec(shape, lambda *_: (0,) * nd)


def _rope_table_kernel(pos_ref, invf_ref, cos_ref, sin_ref):
    ang = pos_ref[...].astype(F32) * invf_ref[...]
    cos_ref[...] = jnp.cos(ang)
    sin_ref[...] = jnp.sin(ang)


def _rope_tables(positions):
    half = HEAD_DIM // 2
    per_row = LANES // half
    t = positions.size
    rows = t // per_row
    pos_e = jnp.repeat(positions.reshape(rows, per_row), half, axis=1)
    inv_freq = 1.0 / (ROPE_THETA ** (jnp.arange(0, HEAD_DIM, 2, dtype=F32) / HEAD_DIM))
    invf = jnp.tile(inv_freq, per_row)[None, :]
    tr = min(rows, 1024)
    cos, sin = pl.pallas_call(
        _rope_table_kernel,
        grid=(rows // tr,),
        in_specs=[pl.BlockSpec((tr, LANES), lambda i: (i, 0)), _const_spec((1, LANES))],
        out_specs=[pl.BlockSpec((tr, LANES), lambda i: (i, 0))] * 2,
        out_shape=[jax.ShapeDtypeStruct((rows, LANES), F32)] * 2,
        compiler_params=_params(1),
        name="rope_tables",
    )(pos_e, invf)
    return cos, sin


_ROW_SEGS = (("dk", 512, BF16, True), ("ksd", 128, BF16, True), ("kwd", 128, BF16, True), ("kc", 128, F32, True),
             ("vc", 128, F32, False), ("ng", 128, F32, False))
_DUP_SEGS = ("ksd", "kwd")
Q_SCALE = HEAD_DIM ** -0.5 * math.log2(math.e)
KEY_PAIR = 2 * KEY_TILE
ONES_ROWS = 16
_COL_SEGS = (("dqT", 512, True, None, None), ("nqT", 512, True, None, None),
             ("dvT", 512, False, KEY_PAIR, DIFF_VDIM), ("vsT", 128, False, KEY_PAIR, HEAD_DIM), ("vwT", 128, False, KEY_TILE, HEAD_DIM))
_STRIDED_SEGS = ("kc", "vc")


def _proj_kernel(x_ref, g_ref, cos_ref, sin_ref, wr_ref, wt_ref, *out_refs):
    x = x_ref[...]
    ms = jnp.mean(x * x, axis=-1, keepdims=True)
    u = (x * lax.rsqrt(ms + NORM_EPS) * g_ref[...]).astype(BF16)
    half = HEAD_DIM // 2
    reps = LANES // half

    def rope_fn(c, s, axis):
        idx = lax.broadcasted_iota(jnp.int32, c.shape, axis)
        upper = (idx % HEAD_DIM) >= half
        s_up = jnp.where(upper, s, 0.0)
        s_lo = jnp.where(upper, 0.0, -s)
        return lambda y: y * c + pltpu.roll(y, half, axis) * s_up + pltpu.roll(y, LANES - half, axis) * s_lo

    *refs, stage_ref = out_refs

    def expand(t_ref):
        t4 = t_ref[...]
        rolled = [t4] + [pltpu.roll(t4, half * m, 1) for m in range(1, reps)]
        slot = lax.broadcasted_iota(jnp.int32, t4.shape, 1) // half
        for j in range(reps):
            ej = rolled[-j % reps]
            for k in range(1, reps):
                ej = jnp.where(slot == k, rolled[(k - j) % reps], ej)
            stage_ref[pl.ds(j, t4.shape[0], stride=reps), :] = ej
        return stage_ref[...]

    c = expand(cos_ref)
    s = expand(sin_ref)
    rope_rows = rope_fn(c, s, 1)
    rope_cols = rope_fn(c.T, s.T, 0)

    tm = x.shape[0]
    pieces = []
    for name, width, _, roped in _ROW_SEGS:
        o_ref = refs.pop(0)
        pieces += [(name, o_ref, c, roped) for c in range(0, width, LANES)]
    for p0 in range(0, len(pieces), 2):
        pair = pieces[p0:p0 + 2]
        y = jnp.dot(u, wr_ref[:, p0 * LANES:(p0 + len(pair)) * LANES], preferred_element_type=F32)
        for i, (name, o_ref, c, roped) in enumerate(pair):
            yl = y[:, i * LANES:(i + 1) * LANES]
            if roped:
                yl = rope_rows(yl)
            if name == "ng":
                yl = jax.nn.sigmoid(yl)
            if name in _DUP_SEGS:
                other = pltpu.roll(yl, HEAD_DIM, 1)
                first = lax.broadcasted_iota(jnp.int32, yl.shape, 1) < HEAD_DIM
                o_ref[:, :LANES] = jnp.where(first, yl, other).astype(o_ref.dtype)
                o_ref[:, LANES:] = jnp.where(first, other, yl).astype(o_ref.dtype)
            elif name in _STRIDED_SEGS:
                stage_ref[...] = yl
                for l in range(CMP_STRIDE):
                    o_ref[:, l * LANES:(l + 1) * LANES] = stage_ref[pl.ds(l, tm // CMP_STRIDE, stride=CMP_STRIDE), :]
            else:
                o_ref[:, c:c + LANES] = yl.astype(o_ref.dtype)

    ut = u.T
    pieces = []
    for name, rows, roped, tile, group in _COL_SEGS:
        o_ref = refs.pop(0)
        pieces += [(o_ref, r, roped, tile, group) for r in range(0, rows, LANES)]
    for p0 in range(0, len(pieces), 2):
        pair = pieces[p0:p0 + 2]
        yt = jnp.dot(wt_ref[p0 * LANES:(p0 + len(pair)) * LANES, :], ut, preferred_element_type=F32)
        for i, (o_ref, r, roped, tile, group) in enumerate(pair):
            ys = yt[i * LANES:(i + 1) * LANES]
            if roped:
                ys = rope_cols(ys) * Q_SCALE
            ys = ys.astype(o_ref.dtype)
            if not tile:
                o_ref[r:r + LANES, :] = ys
                continue
            for g0 in range(0, LANES, group):
                dst = (r + g0) // group * (group + ONES_ROWS)
                for j in range(tm // tile):
                    o_ref[j, dst:dst + group, :] = ys[g0:g0 + group, j * tile:(j + 1) * tile]
                    o_ref[j, dst + group:dst + group + ONES_ROWS, :] = jnp.ones((ONES_ROWS, tile), o_ref.dtype)


def _project(x2, g, cos, sin, w_in):
    t, d = x2.shape
    qk = DIFF_HEADS * 2 * HEAD_DIM
    kv = NSA_KV_GROUPS * HEAD_DIM
    splits = (qk, qk, DIFF_HEADS * DIFF_VDIM, NSA_HEADS * HEAD_DIM, kv, kv, kv, kv, kv, kv, 3 * NSA_HEADS)
    offs = np.cumsum((0,) + splits)
    dq, dk, dv, nq, kc, vc, ks, vs, kw, vw, ng = (w_in[:, offs[i]:offs[i + 1]] for i in range(len(splits)))
    ng = jnp.pad(ng, ((0, 0), (0, LANES - ng.shape[1])))
    wr = jnp.concatenate([dk, ks, kw, kc, vc, ng], axis=1)
    wt = jnp.concatenate([dq, nq, dv, vs, vw], axis=1).T
    tm = min(t, ROW_TILE)
    assert tm % KEY_TILE == 0
    half = HEAD_DIM // 2
    out_specs, out_shape = [], []
    for name, w, dt, _ in _ROW_SEGS:
        fold = CMP_STRIDE if name in _STRIDED_SEGS else 1
        w_out = w * fold * (2 if name in _DUP_SEGS else 1)
        out_specs.append(pl.BlockSpec((tm // fold, w_out), lambda i: (i, 0)))
        out_shape.append(jax.ShapeDtypeStruct((t // fold, w_out), dt))
    for _, rows, _, tile, group in _COL_SEGS:
        if tile:
            assert tm % tile == 0
            padded = rows // group * (group + ONES_ROWS)
            out_specs.append(pl.BlockSpec((tm // tile, padded, tile), lambda i: (i, 0, 0)))
            out_shape.append(jax.ShapeDtypeStruct((t // tile, padded, tile), BF16))
        else:
            out_specs.append(pl.BlockSpec((rows, tm), lambda i: (0, i)))
            out_shape.append(jax.ShapeDtypeStruct((rows, t), BF16))
    outs = pl.pallas_call(
        _proj_kernel,
        grid=(t // tm,),
        in_specs=[
            pl.BlockSpec((tm, d), lambda i: (i, 0)),
            _const_spec((1, d)),
            pl.BlockSpec((tm * half // LANES, LANES), lambda i: (i, 0)),
            pl.BlockSpec((tm * half // LANES, LANES), lambda i: (i, 0)),
            _const_spec(wr.shape),
            _const_spec(wt.shape),
        ],
        out_specs=out_specs,
        out_shape=out_shape,
        scratch_shapes=[pltpu.VMEM((tm, LANES), F32)],
        compiler_params=_params(1),
        name="in_proj",
    )(x2, g[None, :], cos, sin, wr, wt)
    return dict(zip([n for n, *_ in _ROW_SEGS] + [n for n, *_ in _COL_SEGS], outs))


def _compress_kernel(x_ref, pea_ref, peb_ref, w1a_ref, w1b_ref, w2_ref, o_ref):
    x = x_ref[...]
    a = jnp.dot((x + pea_ref[...]).astype(BF16), w1a_ref[...], preferred_element_type=F32)
    b = jnp.dot((x + peb_ref[...]).astype(BF16), w1b_ref[...], preferred_element_type=F32)
    ncp = x.shape[0]
    h = a + pltpu.roll(b, ncp - 1, 0)
    h = h * jax.nn.sigmoid(h)
    o_ref[...] = jnp.dot(h.astype(BF16), w2_ref[...], preferred_element_type=F32).astype(o_ref.dtype)


def _compress(xc, pe, w1, w2, b, s):
    g_, dh = NSA_KV_GROUPS, HEAD_DIM
    ncp = s // CMP_STRIDE
    hid = w1.shape[1]
    halves = CMP_LEN // CMP_STRIDE
    assert halves == 2
    xr = xc.reshape(b, ncp, CMP_STRIDE * g_ * dh)
    eye = jnp.eye(g_, dtype=F32)
    pe_r = pe.reshape(halves, CMP_STRIDE, 1, dh)
    pe_t = jnp.broadcast_to(pe_r, (halves, CMP_STRIDE, g_, dh)).reshape(halves, 1, CMP_STRIDE * g_ * dh)
    w1r = w1.reshape(halves, CMP_STRIDE, dh, hid).astype(BF16)
    w1bd = jnp.zeros((halves, CMP_STRIDE, g_, dh, g_ * hid), BF16)
    for g in range(g_):
        w1bd = w1bd.at[:, :, g, :, g * hid:(g + 1) * hid].set(w1r)
    w1bd = w1bd.reshape(halves, CMP_STRIDE * g_ * dh, g_ * hid)
    w2bd = jnp.einsum("jd,pg,r->pjgrd", w2, eye, jnp.ones((2,), F32)).reshape(g_ * hid, g_ * 2 * dh).astype(BF16)
    kdim = CMP_STRIDE * g_ * dh
    return pl.pallas_call(
        _compress_kernel,
        grid=(b,),
        in_specs=[
            pl.BlockSpec((None, ncp, kdim), lambda i: (i, 0, 0)),
            _const_spec((1, kdim)),
            _const_spec((1, kdim)),
            _const_spec((kdim, g_ * hid)),
            _const_spec((kdim, g_ * hid)),
            _const_spec((g_ * hid, g_ * 2 * dh)),
        ],
        out_specs=pl.BlockSpec((None, ncp, g_ * 2 * dh), lambda i: (i, 0, 0)),
        out_shape=jax.ShapeDtypeStruct((b, ncp, g_ * 2 * dh), BF16),
        compiler_params=_params(1),
        name="compress",
    )(xr, pe_t[0], pe_t[1], w1bd[0], w1bd[1], w2bd)


def _split_heads(qt):
    row = lax.broadcasted_iota(jnp.int32, qt.shape, 0)
    zero = jnp.zeros_like(qt)
    return jnp.concatenate([jnp.where(row < HEAD_DIM, qt, zero), jnp.where(row >= HEAD_DIM, qt, zero)], axis=1)


def _flash_scratch(tk, m_cols, dv):
    row = pltpu.VMEM((1, m_cols), F32)
    sc = pltpu.VMEM((tk, m_cols), F32)
    return [sc, sc, row, row, pltpu.VMEM((2 * tk, m_cols), BF16), row, row, pltpu.VMEM((dv + ONES_ROWS, m_cols), F32)]


class _Stream(NamedTuple):
    qst: jax.Array
    load_k: Callable
    load_vt: Callable
    scratch: Sequence
    mask_fn: Callable
    inner_fn: Callable
    skip_last_tile: bool = False
    last_tile_live: Sequence = ()
    live_mask_fn: Callable = None


def _flash_tiles(streams, tk, n_k, hi, first_step):
    def scores(st, t, slot, masked):
        s_ref, t_ref = st.scratch[slot], st.scratch[2 + slot]
        if masked and slot == 1 and st.skip_last_tile:
            t_ref[...] = jnp.full(t_ref.shape, NEG_INF, F32)
            s_ref[...] = jnp.full(s_ref.shape, NEG_INF, F32)
            return
        k = st.load_k(pl.multiple_of(jnp.clip(t, 0, n_k - 1) * tk, tk))
        if masked and slot == 1 and st.last_tile_live:
            q_live = jnp.concatenate([st.qst[:, a:b] for a, b in st.last_tile_live], axis=1)
            s_live = st.live_mask_fn(t, jnp.dot(k, q_live, preferred_element_type=F32))
            off, done = 0, 0
            for a, b in tuple(st.last_tile_live) + ((s_ref.shape[1], s_ref.shape[1]),):
                if a > done:
                    s_ref[:, done:a] = jnp.full((tk, a - done), NEG_INF, F32)
                    t_ref[:, done:a] = jnp.full((1, a - done), NEG_INF, F32)
                if b > a:
                    s_ref[:, a:b] = s_live[:, off:off + b - a]
                    t_ref[:, a:b] = jnp.max(s_live[:, off:off + b - a], axis=0, keepdims=True)
                off, done = off + b - a, b
            return
        s = (st.mask_fn if masked else st.inner_fn)(t, jnp.dot(k, st.qst, preferred_element_type=F32))
        t_ref[...] = jnp.max(s, axis=0, keepdims=True)
        s_ref[...] = s

    def probs(st):
        sa_ref, sb_ref, ta_ref, tb_ref, p_ref, al_ref, m_ref, _ = st.scratch
        m_old = m_ref[...]
        m_new = jnp.maximum(m_old, jnp.maximum(ta_ref[...], tb_ref[...]))
        p_ref[:tk, :] = jnp.exp2(sa_ref[...] - m_new).astype(BF16)
        p_ref[tk:, :] = jnp.exp2(sb_ref[...] - m_new).astype(BF16)
        al_ref[...] = jnp.exp2(m_old - m_new)
        m_ref[...] = m_new

    def values(st, pair):
        p_ref, al_ref, acc_ref = st.scratch[4], st.scratch[5], st.scratch[7]
        vt = st.load_vt(jnp.clip(pair, 0, n_k // 2 - 1))
        acc_ref[...] = al_ref[...] * acc_ref[...] + jnp.dot(vt, p_ref[...], preferred_element_type=F32)

    def each(fn, *args):
        for st in streams:
            fn(st, *args)

    def init(st):
        m_ref, al_ref = st.scratch[6], st.scratch[5]
        m_ref[...] = jnp.full(m_ref.shape, NEG_INF, F32)
        al_ref[...] = jnp.ones(al_ref.shape, F32)

    @pl.when(first_step)
    def _():
        for st in streams:
            st.scratch[4][...] = jnp.zeros(st.scratch[4].shape, BF16)
            st.scratch[7][...] = jnp.zeros(st.scratch[7].shape, F32)

    n_pairs = hi // 2 + 1

    def pair_at(r):
        return jnp.where(r == 0, n_pairs - 1, r - 1)

    each(init)
    each(scores, 2 * pair_at(0), 0, True)
    each(scores, 2 * pair_at(0) + 1, 1, True)

    def body(r, carry):
        each(values, pair_at(r - 1))
        each(probs)
        each(scores, 2 * r, 0, False)
        each(scores, 2 * r + 1, 1, False)
        return carry

    lax.fori_loop(0, n_pairs - 1, body, 0)
    each(values, pair_at(n_pairs - 2))
    each(probs)
    each(values, pair_at(n_pairs - 1))
    out = []
    for st in streams:
        acc = st.scratch[7][...]
        dv = acc.shape[0] - ONES_ROWS
        out.append((acc[dv:dv + 1], acc[:dv]))
    return out


def _diff_kernel(lam_ref, g_ref, qt_ref, k_ref, vt_ref, o_ref, *scratch, tq, tk, heads, lam_init):
    qi = pl.program_id(2)
    m_cols = 2 * tq
    qpos = qi * tq + lax.broadcasted_iota(jnp.int32, (1, m_cols), 1) % tq
    krel = lax.broadcasted_iota(jnp.int32, (tk, m_cols), 0)
    hw = 2 * HEAD_DIM
    vrows = DIFF_VDIM + ONES_ROWS
    n_sc = len(scratch) // heads

    def causal(t, s):
        return jnp.where(krel <= qpos - t * tk, s, NEG_INF)

    live = tuple((c * tq + tk, (c + 1) * tq) for c in range(2))
    n_live = 2 * (tq - tk)
    qpos_live = qi * tq + tk + lax.broadcasted_iota(jnp.int32, (1, n_live), 1) % (tq - tk)
    krel_live = lax.broadcasted_iota(jnp.int32, (tk, n_live), 0)

    def causal_live(t, s):
        return jnp.where(krel_live <= qpos_live - t * tk, s, NEG_INF)

    streams = [
        _Stream(
            qst=_split_heads(qt_ref[h * hw:(h + 1) * hw, :]),
            load_k=lambda row, h=h: k_ref[pl.ds(row, tk), h * hw:(h + 1) * hw],
            load_vt=lambda pair, h=h: vt_ref[pair, h * vrows:(h + 1) * vrows, :],
            scratch=scratch[h * n_sc:(h + 1) * n_sc],
            mask_fn=causal,
            inner_fn=lambda t, s: s,
            last_tile_live=live,
            live_mask_fn=causal_live,
        )
        for h in range(heads)
    ]
    first_step = (pl.program_id(0) == 0) & (pl.program_id(1) == 0) & (qi == 0)
    results = _flash_tiles(streams, tk, k_ref.shape[0] // tk, (qi * tq + tq - 1) // tk, first_step)
    lf = lam_ref[...]
    lam = (jnp.exp(jnp.sum(lf[0:1] * lf[1:2], axis=1, keepdims=True))
           - jnp.exp(jnp.sum(lf[2:3] * lf[3:4], axis=1, keepdims=True)) + lam_init)
    for h, (l, acc) in enumerate(results):
        o = acc * (1.0 / l)
        y = o[:, :tq] - lam * o[:, tq:]
        y = y * lax.rsqrt(jnp.mean(y * y, axis=0, keepdims=True) + NORM_EPS)
        y = y.T * g_ref[...] * (1.0 - lam_init)
        o_ref[:, h * DIFF_VDIM:(h + 1) * DIFF_VDIM] = y.astype(o_ref.dtype)


def _diff_attention(dqt, dk, dvt, diff_lambda, subln_g, b, s, layer):
    tk = KEY_TILE
    tq = 2 * tk
    heads = DIFF_HEADS
    hw = heads * 2 * HEAD_DIM
    vrows = heads * (DIFF_VDIM + ONES_ROWS)
    nq = s // tq
    return pl.pallas_call(
        functools.partial(_diff_kernel, tq=tq, tk=tk, heads=heads, lam_init=_lambda_init(layer)),
        grid=(b, DIFF_HEADS // heads, nq),
        in_specs=[
            _const_spec(diff_lambda.shape),
            _const_spec((1, DIFF_VDIM)),
            pl.BlockSpec((hw, tq), lambda bi, h, i: (h, bi * nq + i)),
            pl.BlockSpec((None, s, hw), lambda bi, h, i: (bi, 0, h)),
            pl.BlockSpec((None, s // KEY_PAIR, vrows, KEY_PAIR), lambda bi, h, i: (bi, 0, h, 0)),
        ],
        out_specs=pl.BlockSpec((None, tq, heads * DIFF_VDIM), lambda bi, h, i: (bi, i, h)),
        out_shape=jax.ShapeDtypeStruct((b, s, DIFF_HEADS * DIFF_VDIM), BF16),
        scratch_shapes=_flash_scratch(tk, 2 * tq, DIFF_VDIM) * heads,
        compiler_params=_params(3),
        name="diff_attn",
    )(diff_lambda, subln_g[None, :], dqt, dk.reshape(b, s, -1), dvt.reshape(b, s // KEY_PAIR, -1, KEY_PAIR))


def _nsa_kernel(ovl_ref, qt_ref, gate_ref, kc_ref, vc_ref, ks_ref, vst_ref, kw_ref, vwt_ref, o_ref, imp_ref, selb_ref, cnt_ref, *scratch, tq, tk, nsub):
    step = pl.program_id(1)
    m_cols = NSA_HPG * tq
    dh = HEAD_DIM
    gw = NSA_HPG * dh
    kwid = 2 * dh
    vrows = dh + ONES_ROWS
    units = [(j, g) for j in range(nsub) for g in range(NSA_KV_GROUPS)]
    n_sc = len(scratch) // len(units)
    ncp = kc_ref.shape[0]
    nsb = imp_ref.shape[1]
    topk = min(SEL_TOPK, nsb)
    bpt = tk // SEL_LEN
    n_back = -(-(WINDOW - 1) // tk)
    krel = lax.broadcasted_iota(jnp.int32, (tk, m_cols), 0)
    blk = lax.broadcasted_iota(jnp.int32, (nsb, tq), 0)
    crow = lax.broadcasted_iota(jnp.int32, (ncp, m_cols), 0)

    q0s = [(step * nsub + j) * tq for j in range(nsub)]
    qpos_ts = [q0 + lax.broadcasted_iota(jnp.int32, (1, tq), 1) for q0 in q0s]
    qposs = [jnp.concatenate([qp] * NSA_HPG, axis=1) for qp in qpos_ts]
    cmasks = [crow <= lax.shift_right_arithmetic(qp - (CMP_LEN - 1), int(math.log2(CMP_STRIDE))) for qp in qposs]
    qblks = [qp // SEL_LEN for qp in qpos_ts]
    forceds = [(blk == 0) | (blk == qb) | (blk == qb - 1) for qb in qblks]

    qsts = []
    for j, g in units:
        qt = qt_ref[g * gw:(g + 1) * gw, j * tq:(j + 1) * tq]
        qsts.append(jnp.concatenate([_split_heads(qt[:LANES]), _split_heads(qt[LANES:])], axis=1))
    scs = [jnp.where(cmasks[j], jnp.dot(kc_ref[:, g * kwid:(g + 1) * kwid], qsts[u], preferred_element_type=F32), NEG_INF)
           for u, (j, g) in enumerate(units)]
    pcs = [jnp.exp2(sc - jnp.max(sc, axis=0, keepdims=True)) for sc in scs]
    pcs = [pc * jnp.where(qposs[j] >= CMP_LEN - 1, 1.0 / jnp.sum(pc, axis=0, keepdims=True), 0.0)
           for pc, (j, g) in zip(pcs, units)]
    o_cs = [lax.dot_general(vc_ref[:, g * kwid:(g + 1) * kwid], pcs[u].astype(BF16), (((0,), (0,)), ((), ())),
                            preferred_element_type=F32)[:dh] for u, (j, g) in enumerate(units)]
    for u, (j, g) in enumerate(units):
        pc_sum = functools.reduce(jnp.add, [pcs[u][:, h * tq:(h + 1) * tq] for h in range(NSA_HPG)])
        imp = jnp.dot(ovl_ref[...], pc_sum, preferred_element_type=F32, precision=lax.Precision.HIGHEST)
        imp = jnp.where(forceds[j], FORCE_SCORE, imp)
        imp_ref[u] = jnp.where(blk <= qblks[j], imp, NEG_INF)
        cnt_ref[u] = jnp.zeros((nsb, tq), jnp.int32)

    sub = 8
    for g0 in range(0, nsb, sub):
        for j in range(nsub):
            @pl.when(g0 <= (q0s[j] + tq - 1) // SEL_LEN)
            def _(g0=g0, j=j):
                for u in [u for u, (ju, _) in enumerate(units) if ju == j]:
                    cnts = [cnt_ref[u, b0:b0 + sub, :] for b0 in range(0, nsb, sub)]
                    for i in range(g0, g0 + sub):
                        row = jnp.broadcast_to(imp_ref[u, i:i + 1, :], (sub, tq))
                        for n, b0 in enumerate(range(0, nsb, sub)):
                            cur = imp_ref[u, b0:b0 + sub, :]
                            if b0 > i:
                                beats = row >= cur
                            elif b0 + sub - 1 < i:
                                beats = row > cur
                            else:
                                beats = (row > cur) | ((row == cur) & (lax.broadcasted_iota(jnp.int32, (sub, tq), 0) > i - b0))
                            cnts[n] = cnts[n] + jnp.where(beats, 1, 0)
                    for n, b0 in enumerate(range(0, nsb, sub)):
                        cnt_ref[u, b0:b0 + sub, :] = cnts[n]
    for u in range(len(units)):
        selb_ref[u] = jnp.where(cnt_ref[u] < topk, 0.0, NEG_INF)

    def biased(u):
        def fn(t, s):
            first_blk = jnp.clip(t, 0, nsb // bpt - 1) * bpt
            rows = []
            for r in range(bpt):
                brow = selb_ref[u, pl.ds(first_blk + r, 1), :]
                brow = jnp.concatenate([brow] * NSA_HPG, axis=1)
                rows.append(jnp.broadcast_to(brow, (SEL_LEN, m_cols)))
            return s + jnp.concatenate(rows, axis=0)
        return fn

    def selected(u, j):
        return lambda t, s: jnp.where(krel <= qposs[j] - t * tk, biased(u)(t, s), NEG_INF)

    streams = [
        _Stream(
            qst=qsts[u],
            load_k=lambda row, g=g: ks_ref[pl.ds(row, tk), g * kwid:(g + 1) * kwid],
            load_vt=lambda pair, g=g: vst_ref[pair, g * vrows:(g + 1) * vrows, :],
            scratch=scratch[u * n_sc:(u + 1) * n_sc],
            mask_fn=selected(u, j),
            inner_fn=biased(u),
            skip_last_tile=j < nsub - 1,
        )
        for u, (j, g) in enumerate(units)
    ]
    last = (q0s[-1] + tq - 1) // tk
    sel_out = _flash_tiles(streams, tk, ks_ref.shape[0] // tk, last, (pl.program_id(0) == 0) & (step == 0))
    o_ss = [acc_s * (1.0 / l_s) for l_s, acc_s in sel_out]

    n_win = n_back + tq // tk
    win_tile = lambda j, r: (step * nsub + j) * (tq // tk) - n_back + r
    s_ws = [[] for _ in units]
    for r in range(n_win):
        d_min = (n_back - r) * tk - (tk - 1)
        d_max = (n_back - r) * tk + (tq - 1)
        for u, (j, g) in enumerate(units):
            t = win_tile(j, r)
            k = kw_ref[pl.ds(pl.multiple_of(jnp.maximum(t, 0) * tk, tk), tk), g * kwid:(g + 1) * kwid]
            s = jnp.dot(k, qsts[u], preferred_element_type=F32)
            if d_min < 0:
                s = jnp.where(krel <= qposs[j] - t * tk, s, NEG_INF)
            if d_max >= WINDOW:
                s = jnp.where(krel > qposs[j] - t * tk - WINDOW, s, NEG_INF)
            if r < n_back:
                s = s + jnp.where(t >= 0, 0.0, NEG_INF)
            s_ws[u].append(s)
    m_ws = [functools.reduce(jnp.maximum, [jnp.max(s, axis=0, keepdims=True) for s in s_w]) for s_w in s_ws]
    acc_ws = [jnp.zeros((vrows, m_cols), F32) for _ in units]
    for r in range(n_win):
        for u, (j, g) in enumerate(units):
            p = jnp.exp2(s_ws[u][r] - m_ws[u]).astype(BF16)
            vt = vwt_ref[jnp.maximum(win_tile(j, r), 0), g * vrows:(g + 1) * vrows, :]
            acc_ws[u] = acc_ws[u] + jnp.dot(vt, p, preferred_element_type=F32)
    o_ws = [acc_w[:dh] * (1.0 / acc_w[dh:dh + 1]) for acc_w in acc_ws]

    gate_ts = [gate_ref[j * tq:(j + 1) * tq, :].T for j in range(nsub)]
    for u, (j, g) in enumerate(units):
        gates = gate_ts[j]
        outs = []
        for h in range(NSA_HPG):
            sl = slice(h * tq, (h + 1) * tq)
            gate = lambda branch, h=h: gates[branch * NSA_HEADS + g * NSA_HPG + h:branch * NSA_HEADS + g * NSA_HPG + h + 1, :]
            outs.append(gate(0) * o_cs[u][:, sl] + gate(1) * o_ss[u][:, sl] + gate(2) * o_ws[u][:, sl])
        y = jnp.concatenate(outs, axis=0)
        o_ref[j * tq:(j + 1) * tq, g * gw:(g + 1) * gw] = y.T.astype(o_ref.dtype)


def _overlap_matrix(s):
    nc = s // CMP_STRIDE
    nsb = s // SEL_LEN
    ci = np.arange(nc)[None, :] * CMP_STRIDE
    sj = np.arange(nsb)[:, None] * SEL_LEN
    ovl = ((ci < sj + SEL_LEN) & (ci + CMP_LEN > sj)).astype(np.float32)
    ovl[:, (s - CMP_LEN) // CMP_STRIDE + 1:] = 0.0
    return jnp.asarray(ovl)


def _nsa_attention(nqt, ng, kcmp, vcmp, ksd, vst, kwd, vwt, b, s):
    tk = tq = KEY_TILE
    nsub = KEY_PAIR // tq
    tstep = nsub * tq
    assert tq == tk and tstep == 2 * tk
    nq = s // tstep
    g_ = NSA_KV_GROUPS
    qw = NSA_HEADS * HEAD_DIM
    kw_ = g_ * 2 * HEAD_DIM
    vrows = g_ * (HEAD_DIM + ONES_ROWS)
    ncp = s // CMP_STRIDE
    nsb = s // SEL_LEN
    seq3 = lambda a: a.reshape(b, s, a.shape[-1])
    vt4 = lambda a: a.reshape(b, -1, a.shape[-2], a.shape[-1])
    k_spec = pl.BlockSpec((None, s, kw_), lambda bi, i: (bi, 0, 0))
    vt_spec = lambda tile: pl.BlockSpec((None, s // tile, vrows, tile), lambda bi, i: (bi, 0, 0, 0))
    cmp_spec = pl.BlockSpec((None, ncp, kw_), lambda bi, i: (bi, 0, 0))
    n_units = g_ * nsub
    per_unit = lambda dt: pltpu.VMEM((n_units, nsb, tq), dt)
    return pl.pallas_call(
        functools.partial(_nsa_kernel, tq=tq, tk=tk, nsub=nsub),
        grid=(b, nq),
        in_specs=[
            _const_spec((nsb, ncp)),
            pl.BlockSpec((qw, tstep), lambda bi, i: (0, bi * nq + i)),
            pl.BlockSpec((None, tstep, LANES), lambda bi, i: (bi, i, 0)),
            cmp_spec, cmp_spec, k_spec, vt_spec(KEY_PAIR), k_spec, vt_spec(KEY_TILE),
        ],
        out_specs=pl.BlockSpec((None, tstep, qw), lambda bi, i: (bi, i, 0)),
        out_shape=jax.ShapeDtypeStruct((b, s, qw), BF16),
        scratch_shapes=[per_unit(F32), per_unit(F32), per_unit(jnp.int32)] + _flash_scratch(tk, NSA_HPG * tq, HEAD_DIM) * n_units,
        compiler_params=_params(2),
        name="nsa_attn",
    )(_overlap_matrix(s), nqt, seq3(ng), kcmp, vcmp, seq3(ksd), vt4(vst), seq3(kwd), vt4(vwt))


def _merge_kernel(x_ref, g_ref, ya_ref, yb_ref, wmg_ref, wa_ref, wb_ref, wo_ref, o_ref):
    x = x_ref[...]
    d = x.shape[1]
    ms = jnp.mean(x * x, axis=-1, keepdims=True)
    u = (x * lax.rsqrt(ms + NORM_EPS) * g_ref[...]).astype(BF16)
    ya = jnp.dot(ya_ref[...], wa_ref[...], preferred_element_type=F32)
    yb = jnp.dot(yb_ref[...], wb_ref[...], preferred_element_type=F32)
    mg_a = jax.nn.sigmoid(jnp.dot(u, wmg_ref[:, :d], preferred_element_type=F32))
    mg_b = jax.nn.sigmoid(jnp.dot(u, wmg_ref[:, d:], preferred_element_type=F32))
    merged = (mg_a * ya + mg_b * yb).astype(BF16)
    o_ref[...] = x + jnp.dot(merged, wo_ref[...], preferred_element_type=F32)


def _merge(x2, g, ya, yb, w_mg, w_a, w_b, w_o):
    t, d = x2.shape
    tm = min(t, ROW_TILE)
    ws = [w.astype(BF16) for w in (w_mg, w_a, w_b, w_o)]
    return pl.pallas_call(
        _merge_kernel,
        grid=(t // tm,),
        in_specs=[
            pl.BlockSpec((tm, d), lambda i: (i, 0)),
            _const_spec((1, d)),
            pl.BlockSpec((tm, ya.shape[1]), lambda i: (i, 0)),
            pl.BlockSpec((tm, yb.shape[1]), lambda i: (i, 0)),
        ] + [_const_spec(w.shape) for w in ws],
        out_specs=pl.BlockSpec((tm, d), lambda i: (i, 0)),
        out_shape=jax.ShapeDtypeStruct((t, d), F32),
        compiler_params=_params(1),
        name="merge_out_proj",
    )(x2, g[None, :], ya, yb, *ws)


def _ffn_kernel(h_ref, g_ref, gf_ref, wg_ref, wu_ref, wd_ref, o_ref, *, chunk, final_norm):
    h = h_ref[...]
    ms = jnp.mean(h * h, axis=-1, keepdims=True)
    u = (h * lax.rsqrt(ms + NORM_EPS) * g_ref[...]).astype(BF16)
    d_ff = wg_ref.shape[1]
    acc = h
    for c0 in range(0, d_ff, chunk):
        c1 = min(c0 + chunk, d_ff)
        a = jnp.dot(u, wg_ref[:, c0:c1], preferred_element_type=F32)
        up = jnp.dot(u, wu_ref[:, c0:c1], preferred_element_type=F32)
        act = (a * jax.nn.sigmoid(a) * up).astype(BF16)
        acc = acc + jnp.dot(act, wd_ref[c0:c1, :], preferred_element_type=F32)
    if final_norm:
        ms = jnp.mean(acc * acc, axis=-1, keepdims=True)
        acc = acc * lax.rsqrt(ms + NORM_EPS) * gf_ref[...]
    o_ref[...] = acc


def _ffn(h2, g, g_final, w_gate, w_up, w_down, final_norm):
    t, d = h2.shape
    tm = min(t, ROW_TILE)
    d_ff = w_gate.shape[1]
    chunk = 256
    ws = [w.astype(BF16) for w in (w_gate, w_up, w_down)]
    return pl.pallas_call(
        functools.partial(_ffn_kernel, chunk=chunk, final_norm=final_norm),
        grid=(t // tm,),
        in_specs=[pl.BlockSpec((tm, d), lambda i: (i, 0)), _const_spec((1, d)), _const_spec((1, d))]
        + [_const_spec(w.shape) for w in ws],
        out_specs=pl.BlockSpec((tm, d), lambda i: (i, 0)),
        out_shape=jax.ShapeDtypeStruct((t, d), F32),
        compiler_params=_params(1),
        name="swiglu_ffn",
    )(h2, g[None, :], g_final[None, :], *ws)


def kernel(x, positions, attn_norm_g, w_in, diff_lambda, diff_subln_g, cmp_pe_k, cmp_pe_v, cmp_k_w1, cmp_k_w2, cmp_v_w1, cmp_v_w2, w_branch_a, w_branch_b, w_out, ffn_norm_g, w_gate, w_up, w_down, final_norm_g):
    b, s, d = x.shape
    depth = w_in.shape[0]
    cos, sin = _rope_tables(positions)
    h = x.reshape(b * s, d)
    gates_off = w_in.shape[2] - 2 * d
    for layer in range(depth):
        w_in_b = w_in[layer].astype(BF16)
        p = _project(h, attn_norm_g[layer], cos, sin, w_in_b[:, :gates_off])
        kcmp = _compress(p["kc"], cmp_pe_k[layer], cmp_k_w1[layer], cmp_k_w2[layer], b, s)
        vcmp = _compress(p["vc"], cmp_pe_v[layer], cmp_v_w1[layer], cmp_v_w2[layer], b, s)
        ya = _diff_attention(p["dqT"], p["dk"], p["dvT"], diff_lambda[layer], diff_subln_g[layer], b, s, layer)
        yb = _nsa_attention(p["nqT"], p["ng"], kcmp, vcmp, p["ksd"], p["vsT"], p["kwd"], p["vwT"], b, s)
        h = _merge(h, attn_norm_g[layer], ya.reshape(b * s, -1), yb.reshape(b * s, -1),
                   w_in_b[:, gates_off:], w_branch_a[layer], w_branch_b[layer], w_out[layer])
        h = _ffn(h, ffn_norm_g[layer], final_norm_g, w_gate[layer], w_up[layer], w_down[layer], layer == depth - 1)
    return h.reshape(b, s, d)
```
